```python
import math
import jax, jax.numpy as jnp
from jax import lax
import numpy as np

D_MODEL = 1024
BATCH = 8
SEQ = 2048
DEPTH = 2

GRID_W = 64
CTX_LEN = 256
ROPE_THETA = 10000.0
EPS = 1e-6
Q_BLOCK = 128

MLA_HEADS = 8
MLA_Q_RANK = 256
MLA_KV_RANK = 128
MLA_NOPE = 64
MLA_ROPE = 32
MLA_V = 64
MLA_QK = MLA_NOPE + MLA_ROPE

NA_HEADS = 8
NA_HEAD_DIM = 64
NA_ROWS_MAX = 8
NA_COLS = 16
NA_DIM = NA_HEADS * NA_HEAD_DIM

GQA_HEADS = 8
GQA_KV_HEADS = 2
GQA_HEAD_DIM = 64
GQA_WINDOW = 128

DIFF_HEADS = 4
DIFF_HEAD_DIM = 64

FFN_HIDDEN = -(-8 * D_MODEL // (3 * 256)) * 256

EVEN_IN = [MLA_Q_RANK, MLA_KV_RANK, MLA_ROPE, NA_DIM, NA_DIM, NA_DIM]
EVEN_MIX = MLA_HEADS * MLA_V + NA_DIM
ODD_IN = [GQA_HEADS * GQA_HEAD_DIM, GQA_KV_HEADS * GQA_HEAD_DIM, GQA_KV_HEADS * GQA_HEAD_DIM,
          DIFF_HEADS * 2 * DIFF_HEAD_DIM, DIFF_HEADS * 2 * DIFF_HEAD_DIM, DIFF_HEADS * 2 * DIFF_HEAD_DIM]
ODD_MIX = GQA_HEADS * GQA_HEAD_DIM + DIFF_HEADS * 2 * DIFF_HEAD_DIM

kernel_name = "hybrid_diffusion_mla_na_swa_diff"


def rms_norm(x, g):
    xf = x.astype(jnp.float32)
    y = xf * lax.rsqrt(jnp.mean(xf * xf, axis=-1, keepdims=True) + EPS)
    return (y * g.astype(jnp.float32)).astype(x.dtype)


def split_cols(z, sizes):
    offs = np.cumsum(sizes)[:-1].tolist()
    return jnp.split(z, offs, axis=-1)


def rope_1d(x, pos):
    d = x.shape[-1]
    freqs = ROPE_THETA ** (-jnp.arange(0, d, 2, dtype=jnp.float32) / d)
    ang = pos.astype(jnp.float32)[:, None] * freqs[None, :]
    bshape = (pos.shape[0],) + (1,) * (x.ndim - 3) + (d // 2,)
    cos = jnp.cos(ang).reshape(bshape).astype(x.dtype)
    sin = jnp.sin(ang).reshape(bshape).astype(x.dtype)
    x1, x2 = x[..., : d // 2], x[..., d // 2:]
    return jnp.concatenate([x1 * cos - x2 * sin, x2 * cos + x1 * sin], axis=-1)


def rope_2d(x, rows, cols):
    h = x.shape[-1] // 2
    return jnp.concatenate([rope_1d(x[..., :h], rows), rope_1d(x[..., h:], cols)], axis=-1)


def to_blocks(x):
    b, s = x.shape[:2]
    return jnp.moveaxis(x.reshape((b, s // Q_BLOCK, Q_BLOCK) + x.shape[2:]), 1, 0)


def from_blocks(y):
    nb, b, qb = y.shape[:3]
    return jnp.moveaxis(y, 0, 1).reshape((b, nb * qb) + y.shape[3:])


def dense_attend(q, k, v, scale):
    s = jnp.einsum('bqhd,bkhd->bhqk', q, k).astype(jnp.float32) * scale
    p = jax.nn.softmax(s, axis=-1).astype(v.dtype)
    return jnp.einsum('bhqk,bkhd->bqhd', p, v)


def ada_modulation(cond, w, b):
    return jnp.split(jax.nn.silu(cond) @ w + b, 6, axis=-1)


def modulate(x, g, shift, scale):
    return rms_norm(x, g) * (1 + scale) + shift


def swiglu(h, w_gate, w_up, w_down):
    return (jax.nn.silu(h @ w_gate) * (h @ w_up)) @ w_down


def mla_queries(cq, qa_g, w_uq, qn_g, rows, cols):
    b, t = cq.shape[:2]
    q = rms_norm((rms_norm(cq, qa_g) @ w_uq).reshape(b, t, MLA_HEADS, MLA_QK), qn_g)
    if rows is not None:
        q = jnp.concatenate([q[..., :MLA_NOPE], rope_2d(q[..., MLA_NOPE:], rows, cols)], axis=-1)
    return q


def mla_keys_values(ckv, kr, kva_g, w_ukv, kn_g, rows, cols):
    b, t = ckv.shape[:2]
    kv = (rms_norm(ckv, kva_g) @ w_ukv).reshape(b, t, MLA_HEADS, MLA_NOPE + MLA_V)
    k_rope = jnp.broadcast_to(kr[:, :, None, :], (b, t, MLA_HEADS, MLA_ROPE))
    k = rms_norm(jnp.concatenate([kv[..., :MLA_NOPE], k_rope], axis=-1), kn_g)
    if rows is not None:
        k = jnp.concatenate([k[..., :MLA_NOPE], rope_2d(k[..., MLA_NOPE:], rows, cols)], axis=-1)
    return k, kv[..., MLA_NOPE:]


def neighbourhood_tables(rows_n, rpb):
    kr = min(NA_ROWS_MAX, rows_n)
    kc = NA_COLS
    r = jnp.arange(rows_n)
    col = jnp.arange(GRID_W)
    key_r = jnp.clip(r - kr // 2, 0, rows_n - kr)[:, None] + jnp.arange(kr)[None, :]
    key_c = jnp.clip(col - kc // 2, 0, GRID_W - kc)[:, None] + jnp.arange(kc)[None, :]
    idx = (key_r[:, None, :, None] * GRID_W + key_c[None, :, None, :]).reshape(rows_n, GRID_W, kr * kc)
    off_r = key_r - r[:, None] + (NA_ROWS_MAX - 1)
    off_c = key_c - col[:, None] + (NA_COLS - 1)
    bias = rpb[:, off_r[:, None, :, None], off_c[None, :, None, :]]
    bias = jnp.moveaxis(bias.reshape(NA_HEADS, rows_n, GRID_W, kr * kc), 0, 1)
    return idx, bias


def neighbourhood_attention(q, k, v, k_ctx, v_ctx, idx, bias):
    b, s, h, d = q.shape
    rows_n = s // GRID_W
    n_nb = idx.shape[-1]
    scale = d ** -0.5
    q_rows = jnp.moveaxis(q.reshape(b, rows_n, GRID_W, h, d), 1, 0)

    def row_block(args):
        qb, idx_b, bias_b = args
        kg = k[:, idx_b]
        vg = v[:, idx_b]
        s_nb = jnp.einsum('bqhd,bqkhd->bhqk', qb, kg).astype(jnp.float32) * scale + bias_b[None].astype(jnp.float32)
        s_ctx = jnp.einsum('bqhd,bkhd->bhqk', qb, k_ctx).astype(jnp.float32) * scale
        p = jax.nn.softmax(jnp.concatenate([s_nb, s_ctx], axis=-1), axis=-1).astype(v.dtype)
        return (jnp.einsum('bhqk,bqkhd->bqhd', p[..., :n_nb], vg)
                + jnp.einsum('bhqk,bkhd->bqhd', p[..., n_nb:], v_ctx))

    out = lax.map(row_block, (q_rows, idx, bias))
    return jnp.moveaxis(out, 0, 1).reshape(b, s, h, d)


def windowed_gqa_latent(q, k, v, k_ctx, v_ctx, sink):
    b, s, hq, d = q.shape
    hkv = k.shape[2]
    g = hq // hkv
    nb = s // Q_BLOCK
    band = 3 * Q_BLOCK
    n_ctx = k_ctx.shape[1]
    scale = d ** -0.5

    def bands(z):
        zp = jnp.pad(z, ((0, 0), (Q_BLOCK, Q_BLOCK), (0, 0), (0, 0))).reshape(b, nb + 2, Q_BLOCK, hkv, d)
        return jnp.concatenate([zp[:, :-2], zp[:, 1:-1], zp[:, 2:]], axis=2)

    kb, vb = bands(k), bands(v)
    qb = q.reshape(b, nb, Q_BLOCK, hkv, g, d)
    t = jnp.arange(s).reshape(nb, Q_BLOCK)
    s_pos = (jnp.arange(nb)[:, None] - 1) * Q_BLOCK + jnp.arange(band)[None, :]
    valid = ((s_pos[:, None, :] >= 0) & (s_pos[:, None, :] < s)
             & (jnp.abs(t[:, :, None] - s_pos[:, None, :]) <= GQA_WINDOW))
    s_band = jnp.einsum('bnqhgd,bnkhd->bhgnqk', qb, kb).astype(jnp.float32) * scale
    s_band = jnp.where(valid, s_band, -jnp.inf)
    s_ctx = jnp.einsum('bnqhgd,bkhd->bhgnqk', qb, k_ctx).astype(jnp.float32) * scale
    s_sink = jnp.broadcast_to(sink.astype(jnp.float32).reshape(1, hkv, g, 1, 1, 1), s_band.shape[:-1] + (1,))
    p = jax.nn.softmax(jnp.concatenate([s_band, s_ctx, s_sink], axis=-1), axis=-1).astype(v.dtype)
    o = (jnp.einsum('bhgnqk,bnkhd->bnqhgd', p[..., :band], vb)
         + jnp.einsum('bhgnqk,bkhd->bnqhgd', p[..., band:band + n_ctx], v_ctx))
    return o.reshape(b, s, hq, d)


def gqa_context(q, k, v, sink):
    b, n, hq, d = q.shape
    hkv = k.shape[2]
    g = hq // hkv
    qg = q.reshape(b, n, hkv, g, d)
    s = jnp.einsum('bqhgd,bkhd->bhgqk', qg, k).astype(jnp.float32) * (d ** -0.5)
    s_sink = jnp.broadcast_to(sink.astype(jnp.float32).reshape(1, hkv, g, 1, 1), s.shape[:-1] + (1,))
    p = jax.nn.softmax(jnp.concatenate([s, s_sink], axis=-1), axis=-1)[..., :-1].astype(v.dtype)
    return jnp.einsum('bhgqk,bkhd->bqhgd', p, v).reshape(b, n, hq, d)


def diff_attend(q, k, v, lam, scale):
    s = jnp.einsum('bqhcd,bkhcd->bhcqk', q, k).astype(jnp.float32) * scale
    p = jax.nn.softmax(s, axis=-1)
    w = (p[:, :, 0] - lam * p[:, :, 1]).astype(v.dtype)
    return jnp.einsum('bhqk,bkhe->bqhe', w, v)


def even_mixer(h, hc, p, rows, cols, need_ctx):
    b, s = h.shape[:2]
    cq, ckv, kr, qn, kn, vn = split_cols(h @ p['w_in'], EVEN_IN)
    ccq, cckv, ckr, cqn, ckn, cvn = split_cols(hc @ p['w_in'], EVEN_IN)

    scale_a = MLA_QK ** -0.5
    q_a = mla_queries(cq, p['mla_qa_g'], p['mla_w_uq'], p['mla_qn_g'], rows, cols)
    k_a, v_a = mla_keys_values(ckv, kr, p['mla_kva_g'], p['mla_w_ukv'], p['mla_kn_g'], rows, cols)
    kc_a, vc_a = mla_keys_values(cckv, ckr, p['mla_kva_g'], p['mla_w_ukv'], p['mla_kn_g'], None, None)
    k_all = jnp.concatenate([k_a, kc_a], axis=1)
    v_all = jnp.concatenate([v_a, vc_a], axis=1)
    o_a = from_blocks(lax.map(lambda qb: dense_attend(qb, k_all, v_all, scale_a), to_blocks(q_a)))

    def na_heads(z):
        return z.reshape(z.shape[:2] + (NA_HEADS, NA_HEAD_DIM))
    q_b = rms_norm(na_heads(qn), p['na_qn_g'])
    k_b = rms_norm(na_heads(kn), p['na_kn_g'])
    kc_b = rms_norm(na_heads(ckn), p['na_kn_g'])
    vc_b = na_heads(cvn)
    idx, bias = neighbourhood_tables(s // GRID_W, p['na_rpb'])
    o_b = neighbourhood_attention(q_b, k_b, na_heads(vn), kc_b, vc_b, idx, bias)

    o = jnp.concatenate([o_a.reshape(b, s, -1), o_b.reshape(b, s, -1)], axis=-1) @ p['w_out']
    oc = None
    if need_ctx:
        n = hc.shape[1]
        qc_a = mla_queries(ccq, p['mla_qa_g'], p['mla_w_uq'], p['mla_qn_g'], None, None)
        oc_a = dense_attend(qc_a, kc_a, vc_a, scale_a)
        qc_b = rms_norm(na_heads(cqn), p['na_qn_g'])
        oc_b = dense_attend(qc_b, kc_b, vc_b, NA_HEAD_DIM ** -0.5)
        oc = jnp.concatenate([oc_a.reshape(b, n, -1), oc_b.reshape(b, n, -1)], axis=-1) @ p['w_out']
    return o, oc


def odd_mixer(h, hc, p, layer_idx, rows, cols, need_ctx):
    b, s = h.shape[:2]
    n = hc.shape[1]
    qg, kg, vg, qd, kd, vd = split_cols(h @ p['w_in'], ODD_IN)
    cqg, ckg, cvg, cqd, ckd, cvd = split_cols(hc @ p['w_in'], ODD_IN)

    def gq(z, nh):
        return z.reshape(z.shape[:2] + (nh, GQA_HEAD_DIM))
    q_c = rope_2d(rms_norm(gq(qg, GQA_HEADS), p['gqa_qn_g']), rows, cols)
    k_c = rope_2d(rms_norm(gq(kg, GQA_KV_HEADS), p['gqa_kn_g']), rows, cols)
    kc_c = rms_norm(gq(ckg, GQA_KV_HEADS), p['gqa_kn_g'])
    vc_c = gq(cvg, GQA_KV_HEADS)
    o_c = windowed_gqa_latent(q_c, k_c, gq(vg, GQA_KV_HEADS), kc_c, vc_c, p['gqa_sink'])

    def dqk(z):
        return z.reshape(z.shape[:2] + (DIFF_HEADS, 2, DIFF_HEAD_DIM))
    def dv(z):
        return z.reshape(z.shape[:2] + (DIFF_HEADS, 2 * DIFF_HEAD_DIM))
    lam_init = 0.8 - 0.6 * math.exp(-0.3 * layer_idx)
    f32 = jnp.float32
    lam = (jnp.exp(jnp.sum(p['diff_lq1'].astype(f32) * p['diff_lk1'].astype(f32)))
           - jnp.exp(jnp.sum(p['diff_lq2'].astype(f32) * p['diff_lk2'].astype(f32))) + lam_init)
    scale_d = DIFF_HEAD_DIM ** -0.5
    q_d = rope_2d(rms_norm(dqk(qd), p['diff_qn_g']), rows, cols)
    k_d = rope_2d(rms_norm(dqk(kd), p['diff_kn_g']), rows, cols)
    kc_d = rms_norm(dqk(ckd), p['diff_kn_g'])
    vc_d = dv(cvd)
    k_all = jnp.concatenate([k_d, kc_d], axis=1)
    v_all = jnp.concatenate([dv(vd), vc_d], axis=1)
    o_d = from_blocks(lax.map(lambda qb: diff_attend(qb, k_all, v_all, lam, scale_d), to_blocks(q_d)))
    o_d = rms_norm(o_d, p['diff_subln_g']) * (1 - lam_init)

    o = jnp.concatenate([o_c.reshape(b, s, -1), o_d.reshape(b, s, -1)], axis=-1) @ p['w_out']
    oc = None
    if need_ctx:
        qc_c = rms_norm(gq(cqg, GQA_HEADS), p['gqa_qn_g'])
        oc_c = gqa_context(qc_c, kc_c, vc_c, p['gqa_sink'])
        qc_d = rms_norm(dqk(cqd), p['diff_qn_g'])
        oc_d = rms_norm(diff_attend(qc_d, kc_d, vc_d, lam, scale_d), p['diff_subln_g']) * (1 - lam_init)
        oc = jnp.concatenate([oc_c.reshape(b, n, -1), oc_d.reshape(b, n, -1)], axis=-1) @ p['w_out']
    return o, oc


def trunk_layer(x, xc, cond, cond_ctx, p, layer_idx, rows, cols, need_ctx):
    sh1, sc1, g1, sh2, sc2, g2 = ada_modulation(cond, p['ada_w'], p['ada_b'])
    csh1, csc1, cg1, csh2, csc2, cg2 = ada_modulation(cond_ctx, p['ada_w'], p['ada_b'])
    h = modulate(x, p['norm1_g'], sh1, sc1)
    hc = modulate(xc, p['norm1_g'], csh1, csc1)
    if layer_idx % 2 == 0:
        o, oc = even_mixer(h, hc, p, rows, cols, need_ctx)
    else:
        o, oc = odd_mixer(h, hc, p, layer_idx, rows, cols, need_ctx)
    x = x + g1 * o
    x = x + g2 * swiglu(modulate(x, p['norm2_g'], sh2, sc2), p['ffn_w_gate'], p['ffn_w_up'], p['ffn_w_down'])
    if need_ctx:
        xc = xc + cg1 * oc
        xc = xc + cg2 * swiglu(modulate(xc, p['norm2_g'], csh2, csc2), p['ffn_w_gate'], p['ffn_w_up'], p['ffn_w_down'])
    return x, xc


def setup_inputs(seed: int = 0) -> dict:
    key = jax.random.key(seed)
    keys = list(jax.random.split(key, 48))

    def nrm(shape, s):
        return jax.random.normal(keys.pop(), shape, jnp.float32) * s

    def gain(n):
        return 1.0 + nrm((n,), 0.02)

    d = D_MODEL
    inp = {}
    inp['x'] = nrm((BATCH, SEQ, d), 1.0)
    inp['c'] = nrm((BATCH, d), 1.0)
    inp['ctx'] = nrm((BATCH, CTX_LEN, d), 1.0)
    inp['c_ctx'] = nrm((d,), 1.0)
    inp['l0_ada_w'] = nrm((d, 6 * d), 0.5 * d ** -0.5)
    inp['l0_ada_b'] = nrm((6 * d,), 0.02)
    inp['l0_norm1_g'] = gain(d)
    inp['l0_norm2_g'] = gain(d)
    inp['l0_w_in'] = nrm((d, sum(EVEN_IN)), d ** -0.5)
    inp['l0_mla_qa_g'] = gain(MLA_Q_RANK)
    inp['l0_mla_w_uq'] = nrm((MLA_Q_RANK, MLA_HEADS * MLA_QK), MLA_Q_RANK ** -0.5)
    inp['l0_mla_kva_g'] = gain(MLA_KV_RANK)
    inp['l0_mla_w_ukv'] = nrm((MLA_KV_RANK, MLA_HEADS * (MLA_NOPE + MLA_V)), MLA_KV_RANK ** -0.5)
    inp['l0_mla_qn_g'] = gain(MLA_QK)
    inp['l0_mla_kn_g'] = gain(MLA_QK)
    inp['l0_na_qn_g'] = gain(NA_HEAD_DIM)
    inp['l0_na_kn_g'] = gain(NA_HEAD_DIM)
    inp['l0_na_rpb'] = nrm((NA_HEADS, 2 * NA_ROWS_MAX - 1, 2 * NA_COLS - 1), 0.1)
    inp['l0_w_out'] = nrm((EVEN_MIX, d), EVEN_MIX ** -0.5)
    inp['l0_ffn_w_gate'] = nrm((d, FFN_HIDDEN), d ** -0.5)
    inp['l0_ffn_w_up'] = nrm((d, FFN_HIDDEN), d ** -0.5)
    inp['l0_ffn_w_down'] = nrm((FFN_HIDDEN, d), FFN_HIDDEN ** -0.5)
    inp['l1_ada_w'] = nrm((d, 6 * d), 0.5 * d ** -0.5)
    inp['l1_ada_b'] = nrm((6 * d,), 0.02)
    inp['l1_norm1_g'] = gain(d)
    inp['l1_norm2_g'] = gain(d)
    inp['l1_w_in'] = nrm((d, sum(ODD_IN)), d ** -0.5)
    inp['l1_gqa_qn_g'] = gain(GQA_HEAD_DIM)
    inp['l1_gqa_kn_g'] = gain(GQA_HEAD_DIM)
    inp['l1_gqa_sink'] = nrm((GQA_HEADS,), 0.5)
    inp['l1_diff_qn_g'] = gain(DIFF_HEAD_DIM)
    inp['l1_diff_kn_g'] = gain(DIFF_HEAD_DIM)
    inp['l1_diff_lq1'] = nrm((DIFF_HEAD_DIM,), 0.1)
    inp['l1_diff_lk1'] = nrm((DIFF_HEAD_DIM,), 0.1)
    inp['l1_diff_lq2'] = nrm((DIFF_HEAD_DIM,), 0.1)
    inp['l1_diff_lk2'] = nrm((DIFF_HEAD_DIM,), 0.1)
    inp['l1_diff_subln_g'] = gain(2 * DIFF_HEAD_DIM)
    inp['l1_w_out'] = nrm((ODD_MIX, d), ODD_MIX ** -0.5)
    inp['l1_ffn_w_gate'] = nrm((d, FFN_HIDDEN), d ** -0.5)
    inp['l1_ffn_w_up'] = nrm((d, FFN_HIDDEN), d ** -0.5)
    inp['l1_ffn_w_down'] = nrm((FFN_HIDDEN, d), FFN_HIDDEN ** -0.5)
    return inp


def reference(x, c, ctx, c_ctx,
              l0_ada_w, l0_ada_b, l0_norm1_g, l0_norm2_g, l0_w_in,
              l0_mla_qa_g, l0_mla_w_uq, l0_mla_kva_g, l0_mla_w_ukv, l0_mla_qn_g, l0_mla_kn_g,
              l0_na_qn_g, l0_na_kn_g, l0_na_rpb, l0_w_out,
              l0_ffn_w_gate, l0_ffn_w_up, l0_ffn_w_down,
              l1_ada_w, l1_ada_b, l1_norm1_g, l1_norm2_g, l1_w_in,
              l1_gqa_qn_g, l1_gqa_kn_g, l1_gqa_sink,
              l1_diff_qn_g, l1_diff_kn_g, l1_diff_lq1, l1_diff_lk1, l1_diff_lq2, l1_diff_lk2, l1_diff_subln_g,
              l1_w_out, l1_ffn_w_gate, l1_ffn_w_up, l1_ffn_w_down):
    s = x.shape[1]
    t = jnp.arange(s)
    rows, cols = t // GRID_W, t % GRID_W
    layers = (
        dict(ada_w=l0_ada_w, ada_b=l0_ada_b, norm1_g=l0_norm1_g, norm2_g=l0_norm2_g, w_in=l0_w_in,
             mla_qa_g=l0_mla_qa_g, mla_w_uq=l0_mla_w_uq, mla_kva_g=l0_mla_kva_g, mla_w_ukv=l0_mla_w_ukv,
             mla_qn_g=l0_mla_qn_g, mla_kn_g=l0_mla_kn_g,
             na_qn_g=l0_na_qn_g, na_kn_g=l0_na_kn_g, na_rpb=l0_na_rpb, w_out=l0_w_out,
             ffn_w_gate=l0_ffn_w_gate, ffn_w_up=l0_ffn_w_up, ffn_w_down=l0_ffn_w_down),
        dict(ada_w=l1_ada_w, ada_b=l1_ada_b, norm1_g=l1_norm1_g, norm2_g=l1_norm2_g, w_in=l1_w_in,
             gqa_qn_g=l1_gqa_qn_g, gqa_kn_g=l1_gqa_kn_g, gqa_sink=l1_gqa_sink,
             diff_qn_g=l1_diff_qn_g, diff_kn_g=l1_diff_kn_g, diff_lq1=l1_diff_lq1, diff_lk1=l1_diff_lk1,
             diff_lq2=l1_diff_lq2, diff_lk2=l1_diff_lk2, diff_subln_g=l1_diff_subln_g, w_out=l1_w_out,
             ffn_w_gate=l1_ffn_w_gate, ffn_w_up=l1_ffn_w_up, ffn_w_down=l1_ffn_w_down),
    )
    cond = c[:, None, :]
    cond_ctx = c_ctx[None, None, :]
    xc = ctx
    for i in range(DEPTH):
        x, xc = trunk_layer(x, xc, cond, cond_ctx, layers[i], i, rows, cols, i < DEPTH - 1)
    return x
```

```python
import functools
import math

import numpy as np
import jax
import jax.numpy as jnp
from jax import lax
from jax.experimental import pallas as pl
from jax.experimental.pallas import tpu as pltpu

F32 = jnp.float32
BF16 = jnp.bfloat16

D_MODEL = 1024
SEQ = 2048
GRID_W = 64
CTX_LEN = 256
N_TOK = SEQ + CTX_LEN
ROPE_THETA = 10000.0
EPS = 1e-6

MLA_HEADS = 8
MLA_Q_RANK = 256
MLA_KV_RANK = 128
MLA_NOPE = 64
MLA_ROPE = 32
MLA_V = 64
MLA_QK = MLA_NOPE + MLA_ROPE

NA_HEADS = 8
NA_HEAD_DIM = 64
NA_ROWS_MAX = 8
NA_COLS = 16
NA_DIM = NA_HEADS * NA_HEAD_DIM

GQA_HEADS = 8
GQA_KV_HEADS = 2
GQA_HEAD_DIM = 64
GQA_WINDOW = 128

DIFF_HEADS = 4
DIFF_HEAD_DIM = 64

FFN_HIDDEN = 2816

LANES = 128
TOK_TILE = 256
N_TILES = N_TOK // TOK_TILE
N_LAT_TILES = SEQ // TOK_TILE
VMEM_LIMIT = 56 * 1024 * 1024
LOG2E = 1.4426950408889634
NEG_BIG = -1e30

NA_Q_ROWS = 2
NA_TQ = NA_Q_ROWS * GRID_W
NA_WIN_ROWS = 10
NA_WIN = NA_WIN_ROWS * GRID_W
NA_STEPS = SEQ // NA_TQ
GRID_ROWS = SEQ // GRID_W


def _dot(a, b):
    return jnp.dot(a, b, preferred_element_type=F32)


def _dot_nt(a, b):
    return lax.dot_general(a, b, (((1,), (1,)), ((), ())), preferred_element_type=F32)


def _dot_hilo(a, m):
    hi = a.astype(BF16)
    lo = (a - hi.astype(F32)).astype(BF16)
    return _dot(hi, m) + _dot(lo, m)


def _rms_rows(x):
    return lax.rsqrt(jnp.mean(x * x, axis=-1, keepdims=True) + EPS)


def _const_spec(a):
    nd = a.ndim
    return pl.BlockSpec(a.shape, lambda *_: (0,) * nd)


def _params(n_grid):
    return pltpu.CompilerParams(dimension_semantics=("arbitrary",) * n_grid,
                                vmem_limit_bytes=VMEM_LIMIT)


def _softmax_parts(s_list, extra=None):
    m = jnp.max(s_list[0], axis=-1, keepdims=True)
    for s in s_list[1:]:
        m = jnp.maximum(m, jnp.max(s, axis=-1, keepdims=True))
    if extra is not None:
        m = jnp.maximum(m, extra)
    ps = [jnp.exp2(s - m) for s in s_list]
    l = jnp.sum(ps[0], axis=-1, keepdims=True)
    for p in ps[1:]:
        l = l + jnp.sum(p, axis=-1, keepdims=True)
    if extra is not None:
        l = l + jnp.exp2(extra - m)
    return ps, l


def _stack_halves(q):
    lane = lax.broadcasted_iota(jnp.int32, q.shape, 1)
    zero = jnp.zeros_like(q)
    return jnp.concatenate([jnp.where(lane < 64, q, zero), jnp.where(lane >= 64, q, zero)], axis=0)


def _merge_halves(r, tq):
    lane = lax.broadcasted_iota(jnp.int32, (tq, LANES), 1)
    return jnp.where(lane < 64, r[:tq], r[tq:])


def _ada_kernel(c_ref, w_ref, b_ref, o_ref):
    c = c_ref[...]
    a = (c * (1.0 / (1.0 + jnp.exp(-c)))).astype(BF16)
    o_ref[...] = _dot(a, w_ref[...].astype(BF16)) + b_ref[...]


def _ada(cond, w, b):
    n = w.shape[1]
    tn = 1024
    return pl.pallas_call(
        _ada_kernel,
        grid=(n // tn,),
        in_specs=[pl.BlockSpec(cond.shape, lambda j: (0, 0)),
                  pl.BlockSpec((w.shape[0], tn), lambda j: (0, j)),
                  pl.BlockSpec((1, tn), lambda j: (0, j))],
        out_specs=pl.BlockSpec((cond.shape[0], tn), lambda j: (0, j)),
        out_shape=jax.ShapeDtypeStruct((cond.shape[0], n), F32),
        compiler_params=_params(1),
        name="ada_modulation",
    )(cond, w, b.reshape(1, n))


def _proj0_kernel(x_ref, mod_ref, g1_ref, w0_ref, qag_ref, wuq_ref, kvag_ref, wukv_ref, e_ref,
                  sa_ref, sat_ref, sb_ref, sbt_ref, aq_ref, bq_ref, ak_ref, bk_ref, gqb_ref, gkb_ref,
                  qa_ref, ka_ref, va_ref, qb_ref, kb_ref, vb_ref):
    x = x_ref[0]
    mod = mod_ref[0, 0]
    h = x * _rms_rows(x) * g1_ref[...] * (1.0 + mod[1:2]) + mod[0:1]
    z = _dot(h.astype(BF16), w0_ref[...])

    cq = z[:, 0:256]
    cqn = (cq * _rms_rows(cq) * qag_ref[...]).astype(BF16)
    zq = _dot(cqn, wuq_ref[...])
    qm, qr = zq[:, :1024], zq[:, 1024:]
    inv = lax.rsqrt(_dot_hilo(qm * qm, sa_ref[...]) * (1.0 / MLA_QK) + EPS)
    qa_ref[0] = ((qm * aq_ref[...] + qr * bq_ref[...]) * _dot_hilo(inv, sat_ref[...])).astype(BF16)

    ckv = z[:, 256:384]
    ckvn = (ckv * _rms_rows(ckv) * kvag_ref[...]).astype(BF16)
    zkv = _dot(ckvn, wukv_ref[...])
    krp = _dot_hilo(z[:, 384:512], e_ref[...])
    km = zkv[:, :1024] + krp[:, :1024]
    kr = krp[:, 1024:]
    inv = lax.rsqrt(_dot_hilo(km * km, sa_ref[...]) * (1.0 / MLA_QK) + EPS)
    ka_ref[0] = ((km * ak_ref[...] + kr * bk_ref[...]) * _dot_hilo(inv, sat_ref[...])).astype(BF16)
    va_ref[0] = zkv[:, 1024:].astype(BF16)

    qn, kn = z[:, 512:1024], z[:, 1024:1536]
    inv = lax.rsqrt(_dot_hilo(qn * qn, sb_ref[...]) * (1.0 / NA_HEAD_DIM) + EPS)
    qb_ref[0] = (qn * gqb_ref[...] * _dot_hilo(inv, sbt_ref[...])).astype(BF16)
    inv = lax.rsqrt(_dot_hilo(kn * kn, sb_ref[...]) * (1.0 / NA_HEAD_DIM) + EPS)
    kb_ref[0] = (kn * gkb_ref[...] * _dot_hilo(inv, sbt_ref[...])).astype(BF16)
    vb_ref[0] = z[:, 1536:2048].astype(BF16)


def _tok_spec(width):
    return pl.BlockSpec((1, TOK_TILE, width), lambda t, b: (b, t, 0))


def _mod_spec():
    return pl.BlockSpec((1, 1, 6, D_MODEL), lambda t, b: (b, t // N_LAT_TILES, 0, 0))


def _table_spec(width):
    return pl.BlockSpec((TOK_TILE, width), lambda t, b: (t, 0))


def _proj0(xa, mod, p):
    bsz = xa.shape[0]
    consts1 = [p['g1'], p['w0'], p['qag'], p['wuq'], p['kvag'], p['wukv'], p['e'],
               p['sa'], p['sat'], p['sb'], p['sbt']]
    tables = [p['aq'], p['bq'], p['ak'], p['bk']]
    consts2 = [p['gqb'], p['gkb']]
    widths = [1024, 1024, 512, 512, 512, 512]
    return pl.pallas_call(
        _proj0_kernel,
        grid=(N_TILES, bsz),
        in_specs=([_tok_spec(D_MODEL), _mod_spec()] + [_const_spec(a) for a in consts1]
                  + [_table_spec(1024) for _ in tables] + [_const_spec(a) for a in consts2]),
        out_specs=[_tok_spec(w) for w in widths],
        out_shape=[jax.ShapeDtypeStruct((bsz, N_TOK, w), BF16) for w in widths],
        compiler_params=_params(2),
        name="proj_layer0",
    )(xa, mod, *consts1, *tables, *consts2)


N1_NORM = 1664
N1_MAIN = 2304


def _proj1_kernel(x_ref, mod_ref, g1_ref, w1_ref, s_ref, st_ref, a_ref, b_ref,
                  qc_ref, kc_ref, vc_ref, qd_ref, kd_ref, vd_ref):
    x = x_ref[0]
    mod = mod_ref[0, 0]
    h = x * _rms_rows(x) * g1_ref[...] * (1.0 + mod[1:2]) + mod[0:1]
    z = _dot(h.astype(BF16), w1_ref[...])
    zm, zr = z[:, :N1_NORM], z[:, N1_MAIN:]
    inv = lax.rsqrt(_dot_hilo(zm * zm, s_ref[...]) * (1.0 / 64.0) + EPS)
    y = ((zm * a_ref[...] + zr * b_ref[...]) * _dot_hilo(inv, st_ref[...])).astype(BF16)
    qc_ref[0] = y[:, 0:512]
    kc_ref[0] = y[:, 512:640]
    qd_ref[0] = y[:, 640:1152]
    kd_ref[0] = y[:, 1152:1664]
    vc_ref[0] = z[:, 1664:1792].astype(BF16)
    vd_ref[0] = z[:, 1792:2304].astype(BF16)


def _proj1(xa, mod, p):
    bsz = xa.shape[0]
    consts = [p['g1'], p['w1'], p['s'], p['st']]
    widths = [512, 128, 128, 512, 512, 512]
    return pl.pallas_call(
        _proj1_kernel,
        grid=(N_TILES, bsz),
        in_specs=([_tok_spec(D_MODEL), _mod_spec()] + [_const_spec(a) for a in consts]
                  + [_table_spec(N1_NORM), _table_spec(N1_NORM)]),
        out_specs=[_tok_spec(w) for w in widths],
        out_shape=[jax.ShapeDtypeStruct((bsz, N_TOK, w), BF16) for w in widths],
        compiler_params=_params(2),
        name="proj_layer1",
    )(xa, mod, *consts, p['a'], p['b'])


def _mla_attend(q_ref, k, v, o_ref, tq):
    outs = []
    for hh in range(2):
        q = q_ref[0, :, hh * LANES:(hh + 1) * LANES]
        s = _dot_nt(q, k[hh])
        (p,), l = _softmax_parts([s])
        outs.append(_dot(p.astype(BF16), v) / l)
    o_ref[0] = _merge_halves(jnp.concatenate(outs, axis=0), tq).astype(BF16)


def _mla_kernel(q_ref, k_ref, v_ref, o_ref, *, tq):
    t = pl.program_id(2)

    @pl.when(t < N_LAT_TILES)
    def _():
        k = [k_ref[0, :, hh * LANES:(hh + 1) * LANES] for hh in range(2)]
        _mla_attend(q_ref, k, v_ref[0], o_ref, tq)

    @pl.when(t >= N_LAT_TILES)
    def _():
        k = [k_ref[0, SEQ:, hh * LANES:(hh + 1) * LANES] for hh in range(2)]
        _mla_attend(q_ref, k, v_ref[0, SEQ:, :], o_ref, tq)


def _mla(qa, ka, va):
    bsz = qa.shape[0]
    tq = TOK_TILE
    return pl.pallas_call(
        functools.partial(_mla_kernel, tq=tq),
        grid=(bsz, MLA_HEADS // 2, N_TOK // tq),
        in_specs=[pl.BlockSpec((1, tq, 2 * LANES), lambda b, j, t: (b, t, j)),
                  pl.BlockSpec((1, N_TOK, 2 * LANES), lambda b, j, t: (b, 0, j)),
                  pl.BlockSpec((1, N_TOK, LANES), lambda b, j, t: (b, 0, j))],
        out_specs=pl.BlockSpec((1, tq, LANES), lambda b, j, t: (b, t, j)),
        out_shape=jax.ShapeDtypeStruct((bsz, N_TOK, MLA_HEADS * MLA_V), BF16),
        compiler_params=_params(3),
        name="mla_attention",
    )(qa, ka, va)


def _na_kernel(q_ref, k_ref, v_ref, t_ref, o_ref):
    u = pl.program_id(1)

    @pl.when(u < NA_STEPS)
    def _():
        w0 = jnp.minimum(jnp.clip(NA_Q_ROWS * u - NA_ROWS_MAX // 2, 0, GRID_ROWS - NA_ROWS_MAX),
                         GRID_ROWS - NA_WIN_ROWS)
        ws = pl.multiple_of(w0 * GRID_W, GRID_W)
        for j in range(NA_HEADS // 2):
            cols = slice(j * LANES, (j + 1) * LANES)
            qq = _stack_halves(q_ref[0, :, cols])
            s_nb = _dot_nt(qq, k_ref[0, pl.ds(ws, NA_WIN), cols]) + t_ref[0, j]
            s_cx = _dot_nt(qq, k_ref[0, SEQ:, cols])
            (p_nb, p_cx), l = _softmax_parts([s_nb, s_cx])
            r = (_dot(p_nb.astype(BF16), v_ref[0, pl.ds(ws, NA_WIN), cols])
                 + _dot(p_cx.astype(BF16), v_ref[0, SEQ:, cols])) / l
            o_ref[0, :, cols] = _merge_halves(r, NA_TQ).astype(BF16)

    @pl.when(u >= NA_STEPS)
    def _():
        for j in range(NA_HEADS // 2):
            cols = slice(j * LANES, (j + 1) * LANES)
            qq = _stack_halves(q_ref[0, :, cols])
            s = _dot_nt(qq, k_ref[0, SEQ:, cols])
            (p,), l = _softmax_parts([s])
            r = _dot(p.astype(BF16), v_ref[0, SEQ:, cols]) / l
            o_ref[0, :, cols] = _merge_halves(r, NA_TQ).astype(BF16)


def _na_variant(u):
    return jnp.minimum(u, 2) + jnp.maximum(jnp.minimum(u, NA_STEPS - 1) - (NA_STEPS - 3), 0)


def _na(qb, kb, vb, table):
    bsz = qb.shape[0]
    return pl.pallas_call(
        _na_kernel,
        grid=(bsz, N_TOK // NA_TQ),
        in_specs=[pl.BlockSpec((1, NA_TQ, NA_DIM), lambda b, u: (b, u, 0)),
                  pl.BlockSpec((1, N_TOK, NA_DIM), lambda b, u: (b, 0, 0)),
                  pl.BlockSpec((1, N_TOK, NA_DIM), lambda b, u: (b, 0, 0)),
                  pl.BlockSpec((1, NA_HEADS // 2, 2 * NA_TQ, NA_WIN), lambda b, u: (_na_variant(u), 0, 0, 0))],
        out_specs=pl.BlockSpec((1, NA_TQ, NA_DIM), lambda b, u: (b, u, 0)),
        out_shape=jax.ShapeDtypeStruct((bsz, N_TOK, NA_DIM), BF16),
        compiler_params=_params(2),
        name="neighbourhood_attention",
    )(qb, kb, vb, table)


GQA_TQ = 128
GQA_BAND = 3 * GQA_TQ


def _gqa_kernel(sink_ref, q_ref, k_ref, v_ref, o_ref):
    n = pl.program_id(1)
    start = pl.multiple_of(jnp.clip((n - 1) * GQA_TQ, 0, SEQ - GQA_BAND), GQA_TQ)
    row = lax.broadcasted_iota(jnp.int32, (2 * GQA_TQ, GQA_BAND), 0)
    col = lax.broadcasted_iota(jnp.int32, (2 * GQA_TQ, GQA_BAND), 1)
    tpos = n * GQA_TQ + jnp.where(row >= GQA_TQ, row - GQA_TQ, row)
    valid = jnp.abs(tpos - (start + col)) <= GQA_WINDOW
    hi_rows = lax.broadcasted_iota(jnp.int32, (2 * GQA_TQ, 1), 0) >= GQA_TQ
    kb = k_ref[0, pl.ds(start, GQA_BAND), :]
    vb = v_ref[0, pl.ds(start, GQA_BAND), :]
    kc = k_ref[0, SEQ:, :]
    vc = v_ref[0, SEQ:, :]
    for j in range(GQA_HEADS // 2):
        cols = slice(j * LANES, (j + 1) * LANES)
        qq = _stack_halves(q_ref[0, :, cols])
        s_b = jnp.where(valid, _dot_nt(qq, kb), NEG_BIG)
        s_c = _dot_nt(qq, kc)
        sink = jnp.where(hi_rows, sink_ref[j + GQA_HEADS // 2], sink_ref[j]) * LOG2E
        (p_b, p_c), l = _softmax_parts([s_b, s_c], extra=sink)
        r = (_dot(p_b.astype(BF16), vb) + _dot(p_c.astype(BF16), vc)) / l
        o_ref[0, :, cols] = _merge_halves(r, GQA_TQ).astype(BF16)


def _gqa(sink, qc, kc, vc):
    bsz = qc.shape[0]
    return pl.pallas_call(
        _gqa_kernel,
        grid=(bsz, SEQ // GQA_TQ),
        in_specs=[pl.BlockSpec(memory_space=pltpu.SMEM),
                  pl.BlockSpec((1, GQA_TQ, 512), lambda b, n: (b, n, 0)),
                  pl.BlockSpec((1, N_TOK, LANES), lambda b, n: (b, 0, 0)),
                  pl.BlockSpec((1, N_TOK, LANES), lambda b, n: (b, 0, 0))],
        out_specs=pl.BlockSpec((1, GQA_TQ, 512), lambda b, n: (b, n, 0)),
        out_shape=jax.ShapeDtypeStruct((bsz, SEQ, 512), BF16),
        compiler_params=_params(2),
        name="windowed_gqa",
    )(sink, qc, kc, vc)


def _diff_kernel(lv_ref, q_ref, k_ref, v_ref, g_ref, o_ref, *, tq, lam_init):
    lv = lv_ref[...]
    lam = (jnp.exp(jnp.sum(lv[0:1] * lv[1:2], axis=-1, keepdims=True))
           - jnp.exp(jnp.sum(lv[2:3] * lv[3:4], axis=-1, keepdims=True)) + lam_init)
    qq = _stack_halves(q_ref[0])
    s = _dot_nt(qq, k_ref[0])
    (p,), l = _softmax_parts([s])
    hi_rows = lax.broadcasted_iota(jnp.int32, (2 * tq, 1), 0) >= tq
    pn = p * (jnp.where(hi_rows, -lam, 1.0) / l)
    w = (pn[:tq] + pn[tq:]).astype(BF16)
    o = _dot(w, v_ref[0])
    o_ref[0] = (o * _rms_rows(o) * g_ref[...] * (1.0 - lam_init)).astype(BF16)


def _diff(lv, qd, kd, vd, g, lam_init):
    bsz = qd.shape[0]
    tq = TOK_TILE
    return pl.pallas_call(
        functools.partial(_diff_kernel, tq=tq, lam_init=lam_init),
        grid=(bsz, DIFF_HEADS, SEQ // tq),
        in_specs=[_const_spec(lv),
                  pl.BlockSpec((1, tq, LANES), lambda b, h, t: (b, t, h)),
                  pl.BlockSpec((1, N_TOK, LANES), lambda b, h, t: (b, 0, h)),
                  pl.BlockSpec((1, N_TOK, LANES), lambda b, h, t: (b, 0, h)),
                  _const_spec(g)],
        out_specs=pl.BlockSpec((1, tq, LANES), lambda b, h, t: (b, t, h)),
        out_shape=jax.ShapeDtypeStruct((bsz, SEQ, DIFF_HEADS * 2 * DIFF_HEAD_DIM), BF16),
        compiler_params=_params(3),
        name="diff_attention",
    )(lv, qd, kd, vd, g)


def _ffn_kernel(x_ref, o1_ref, o2_ref, mod_ref, g2_ref, wo1_ref, wo2_ref, wg_ref, wu_ref, wd_ref, y_ref):
    mod = mod_ref[0, 0]
    attn = _dot(o1_ref[0], wo1_ref[...]) + _dot(o2_ref[0], wo2_ref[...])
    x1 = x_ref[0] + mod[2:3] * attn
    h = (x1 * _rms_rows(x1) * g2_ref[...] * (1.0 + mod[4:5]) + mod[3:4]).astype(BF16)
    gate = _dot(h, wg_ref[...])
    up = _dot(h, wu_ref[...])
    a = (gate * (1.0 / (1.0 + jnp.exp(-gate))) * up).astype(BF16)
    y_ref[0] = x1 + mod[5:6] * _dot(a, wd_ref[...])


def _ffn(xa, o1, o2, mod, p, n_tiles):
    bsz = xa.shape[0]
    consts = [p['g2'], p['wo1'], p['wo2'], p['wg'], p['wu'], p['wd']]
    return pl.pallas_call(
        _ffn_kernel,
        grid=(n_tiles, bsz),
        in_specs=([_tok_spec(D_MODEL), _tok_spec(512), _tok_spec(512), _mod_spec()]
                  + [_const_spec(a) for a in consts]),
        out_specs=_tok_spec(D_MODEL),
        out_shape=jax.ShapeDtypeStruct((bsz, n_tiles * TOK_TILE, D_MODEL), F32),
        compiler_params=_params(2),
        name="outproj_ffn",
    )(xa, o1, o2, mod, *consts)


def _rope_spec(dr):
    h = dr // 2
    hh = h // 2
    i = np.arange(dr)
    loc = i % h
    partner = np.where(loc < hh, i + hh, i - hh)
    sign = np.where(loc < hh, -1.0, 1.0).astype(np.float32)
    fidx = loc % hh
    axis = i // h
    return partner, sign, fidx, axis, h


def _rope_cos_sin(dr):
    _, _, fidx, axis, h = _rope_spec(dr)
    freqs = ROPE_THETA ** (-jnp.arange(0, h, 2, dtype=F32) / h)
    t = jnp.arange(SEQ)
    pos = jnp.stack([t // GRID_W, t % GRID_W], axis=1).astype(F32)
    ang = pos[:, axis] * freqs[fidx][None, :]
    cos = jnp.concatenate([jnp.cos(ang), jnp.ones((CTX_LEN, dr), F32)], axis=0)
    sin = jnp.concatenate([jnp.sin(ang), jnp.zeros((CTX_LEN, dr), F32)], axis=0)
    return cos, sin


def _take_cols(w, src):
    wz = jnp.concatenate([w, jnp.zeros((w.shape[0], 1), w.dtype)], axis=1)
    return jnp.take(wz, jnp.asarray(np.where(src < 0, w.shape[1], src)), axis=1)


def _seg_onehot(width, seg):
    s = np.zeros((width, LANES), np.float32)
    s[np.arange(width), np.arange(width) // seg] = 1.0
    return s


def _layer0_params(w_in, qa_g, w_uq, kva_g, w_ukv, qn_g, kn_g, na_qn_g, na_kn_g, norm1_g):
    partner, sign, _, _, _ = _rope_spec(MLA_ROPE)
    cos, sin = _rope_cos_sin(MLA_ROPE)
    o_kr = MLA_Q_RANK + MLA_KV_RANK
    src = np.concatenate([np.arange(0, o_kr), o_kr + np.arange(MLA_ROPE), o_kr + partner,
                          np.full(64, -1), np.arange(o_kr + MLA_ROPE, o_kr + MLA_ROPE + 3 * NA_DIM)])
    w0 = _take_cols(w_in, src).astype(BF16)
    src_m = np.full(2 * 1024, -1)
    a_idx = np.full(1024, -1)
    b_idx = np.full(1024, -1)
    r_idx = np.full(1024, -1)
    for hd in range(MLA_HEADS):
        base = hd * LANES
        src_m[base:base + MLA_QK] = hd * MLA_QK + np.arange(MLA_QK)
        src_m[1024 + base + MLA_NOPE:1024 + base + MLA_QK] = hd * MLA_QK + MLA_NOPE + partner
        a_idx[base:base + MLA_QK] = np.arange(MLA_QK)
        b_idx[base + MLA_NOPE:base + MLA_QK] = MLA_NOPE + partner
        r_idx[base + MLA_NOPE:base + MLA_QK] = np.arange(MLA_ROPE)
    wuq = _take_cols(w_uq, src_m).astype(BF16)

    def tables(g, c):
        gz = jnp.concatenate([g, jnp.zeros((1,), F32)])
        ga = gz[np.where(a_idx < 0, MLA_QK, a_idx)]
        gb = gz[np.where(b_idx < 0, MLA_QK, b_idx)]
        rot = r_idx >= 0
        ri = np.where(rot, r_idx, 0)
        cosl = jnp.where(rot[None, :], cos[:, ri], 1.0)
        sinl = jnp.where(rot[None, :], sin[:, ri] * sign[ri][None, :], 0.0)
        return (ga[None, :] * cosl * c).astype(F32), (gb[None, :] * sinl * c).astype(F32)

    aq, bq = tables(qn_g, MLA_QK ** -0.5 * LOG2E)
    ak, bk = tables(kn_g, 1.0)
    src_kv = np.full(1024 + 512, -1)
    for hd in range(MLA_HEADS):
        src_kv[hd * LANES:hd * LANES + MLA_NOPE] = hd * (MLA_NOPE + MLA_V) + np.arange(MLA_NOPE)
        src_kv[1024 + hd * MLA_V:1024 + (hd + 1) * MLA_V] = hd * (MLA_NOPE + MLA_V) + MLA_NOPE + np.arange(MLA_V)
    wukv = _take_cols(w_ukv, src_kv).astype(BF16)
    e = np.zeros((LANES, 2048), np.float32)
    for hd in range(MLA_HEADS):
        for i in range(MLA_ROPE):
            e[i, hd * LANES + MLA_NOPE + i] = 1.0
            e[MLA_ROPE + i, 1024 + hd * LANES + MLA_NOPE + i] = 1.0
    sa = _seg_onehot(1024, LANES)
    sb = _seg_onehot(NA_DIM, NA_HEAD_DIM)
    return dict(
        g1=norm1_g.reshape(1, -1), w0=w0, qag=qa_g.reshape(1, -1), wuq=wuq, kvag=kva_g.reshape(1, -1),
        wukv=wukv, e=jnp.asarray(e, BF16), sa=jnp.asarray(sa, BF16), sat=jnp.asarray(sa.T, BF16),
        sb=jnp.asarray(sb, BF16), sbt=jnp.asarray(sb.T, BF16), aq=aq, bq=bq, ak=ak, bk=bk,
        gqb=(jnp.tile(na_qn_g, NA_HEADS) * (NA_HEAD_DIM ** -0.5 * LOG2E)).reshape(1, -1),
        gkb=jnp.tile(na_kn_g, NA_HEADS).reshape(1, -1))


GQA_PAIR_HEADS = np.array([0, 4, 1, 5, 2, 6, 3, 7])


def _layer1_params(w_in, gqa_qn_g, gqa_kn_g, diff_qn_g, diff_kn_g, norm1_g):
    partner, sign, _, _, _ = _rope_spec(64)
    cos, sin = _rope_cos_sin(64)
    i64 = np.arange(64)
    qc_cols = np.concatenate([hd * 64 + i64 for hd in GQA_PAIR_HEADS])
    seg = [(qc_cols, gqa_qn_g, GQA_HEAD_DIM ** -0.5 * LOG2E),
           (512 + np.arange(128), gqa_kn_g, 1.0),
           (768 + np.arange(512), diff_qn_g, DIFF_HEAD_DIM ** -0.5 * LOG2E),
           (1280 + np.arange(512), diff_kn_g, 1.0)]
    main = np.concatenate([s[0] for s in seg])
    rot = np.concatenate([s[0].reshape(-1, 64)[:, partner].reshape(-1) for s in seg])
    src = np.concatenate([main, 640 + np.arange(128), 1792 + np.arange(512), rot])
    w1 = jnp.take(w_in, jnp.asarray(src), axis=1).astype(BF16)
    a_parts, b_parts = [], []
    for cols, g, c in seg:
        nh = cols.shape[0] // 64
        a_parts.append(jnp.tile(g[None, :] * cos * c, (1, nh)))
        b_parts.append(jnp.tile(g[partner][None, :] * sin * sign[None, :] * c, (1, nh)))
    s = _seg_onehot(N1_NORM, 64)
    return dict(g1=norm1_g.reshape(1, -1), w1=w1, s=jnp.asarray(s, BF16), st=jnp.asarray(s.T, BF16),
                a=jnp.concatenate(a_parts, axis=1).astype(F32), b=jnp.concatenate(b_parts, axis=1).astype(F32))


def _na_table(rpb):
    us = np.array([0, 1, 2, NA_STEPS - 2, NA_STEPS - 1])
    ri = np.arange(2 * NA_TQ)
    kk = np.arange(NA_WIN)
    head = 2 * np.arange(NA_HEADS // 2)[None, :, None, None] + (ri // NA_TQ)[None, None, :, None]
    qi = ri % NA_TQ
    r = NA_Q_ROWS * us[:, None, None, None] + (qi // GRID_W)[None, None, :, None]
    c = (qi % GRID_W)[None, None, :, None]
    w0 = np.minimum(np.clip(NA_Q_ROWS * us - NA_ROWS_MAX // 2, 0, GRID_ROWS - NA_ROWS_MAX), GRID_ROWS - NA_WIN_ROWS)
    key_r = w0[:, None, None, None] + (kk // GRID_W)[None, None, None, :]
    key_c = (kk % GRID_W)[None, None, None, :]
    r0 = np.clip(r - NA_ROWS_MAX // 2, 0, GRID_ROWS - NA_ROWS_MAX)
    c0 = np.clip(c - NA_COLS // 2, 0, GRID_W - NA_COLS)
    valid = (key_r >= r0) & (key_r < r0 + NA_ROWS_MAX) & (key_c >= c0) & (key_c < c0 + NA_COLS)
    off_r = np.clip(key_r - r + (NA_ROWS_MAX - 1), 0, 2 * NA_ROWS_MAX - 2)
    off_c = np.clip(key_c - c + (NA_COLS - 1), 0, 2 * NA_COLS - 2)
    shape = (len(us), NA_HEADS // 2, 2 * NA_TQ, NA_WIN)
    head, off_r, off_c, valid = (np.broadcast_to(a, shape) for a in (head, off_r, off_c, valid))
    bias = rpb[jnp.asarray(head), jnp.asarray(off_r), jnp.asarray(off_c)] * LOG2E
    return jnp.where(jnp.asarray(valid), bias, NEG_BIG).astype(F32)


def _ffn_params(norm2_g, w_out, w_gate, w_up, w_down, row_perm=None):
    wo = w_out if row_perm is None else jnp.take(w_out, jnp.asarray(row_perm), axis=0)
    return dict(g2=norm2_g.reshape(1, -1), wo1=wo[:512].astype(BF16), wo2=wo[512:].astype(BF16),
                wg=w_gate.astype(BF16), wu=w_up.astype(BF16), wd=w_down.astype(BF16))


def _mod_rows(mod_all, bsz):
    lat = mod_all[:bsz].reshape(bsz, 1, 6, D_MODEL)
    cx = jnp.broadcast_to(mod_all[bsz].reshape(1, 1, 6, D_MODEL), (bsz, 1, 6, D_MODEL))
    return jnp.concatenate([lat, cx], axis=1)


def kernel(x, c, ctx, c_ctx, l0_ada_w, l0_ada_b, l0_norm1_g, l0_norm2_g, l0_w_in, l0_mla_qa_g, l0_mla_w_uq, l0_mla_kva_g, l0_mla_w_ukv, l0_mla_qn_g, l0_mla_kn_g, l0_na_qn_g, l0_na_kn_g, l0_na_rpb, l0_w_out, l0_ffn_w_gate, l0_ffn_w_up, l0_ffn_w_down, l1_ada_w, l1_ada_b, l1_norm1_g, l1_norm2_g, l1_w_in, l1_gqa_qn_g, l1_gqa_kn_g, l1_gqa_sink, l1_diff_qn_g, l1_diff_kn_g, l1_diff_lq1, l1_diff_lk1, l1_diff_lq2, l1_diff_lk2, l1_diff_subln_g, l1_w_out, l1_ffn_w_gate, l1_ffn_w_up, l1_ffn_w_down):
    bsz = x.shape[0]
    assert x.shape[1:] == (SEQ, D_MODEL) and ctx.shape[1:] == (CTX_LEN, D_MODEL)
    rows = -(-(bsz + 1) // 8) * 8
    cond = jnp.concatenate([c, c_ctx[None, :], jnp.zeros((rows - bsz - 1, D_MODEL), F32)], axis=0)
    mod0 = _mod_rows(_ada(cond, l0_ada_w, l0_ada_b), bsz)
    mod1 = _mod_rows(_ada(cond, l1_ada_w, l1_ada_b), bsz)
    xa = jnp.concatenate([x, ctx], axis=1)

    p0 = _layer0_params(l0_w_in, l0_mla_qa_g, l0_mla_w_uq, l0_mla_kva_g, l0_mla_w_ukv, l0_mla_qn_g,
                        l0_mla_kn_g, l0_na_qn_g, l0_na_kn_g, l0_norm1_g)
    qa, ka, va, qb, kb, vb = _proj0(xa, mod0, p0)
    o_a = _mla(qa, ka, va)
    o_b = _na(qb, kb, vb, _na_table(l0_na_rpb))
    f0 = _ffn_params(l0_norm2_g, l0_w_out, l0_ffn_w_gate, l0_ffn_w_up, l0_ffn_w_down)
    xa = _ffn(xa, o_a, o_b, mod0, f0, N_TILES)

    p1 = _layer1_params(l1_w_in, l1_gqa_qn_g, l1_gqa_kn_g, l1_diff_qn_g, l1_diff_kn_g, l1_norm1_g)
    qc, kc, vc, qd, kd, vd = _proj1(xa, mod1, p1)
    o_c = _gqa(l1_gqa_sink, qc, kc, vc)
    lv = jnp.zeros((8, LANES), F32).at[:4, :DIFF_HEAD_DIM].set(
        jnp.stack([l1_diff_lq1, l1_diff_lk1, l1_diff_lq2, l1_diff_lk2]))
    lam_init = 0.8 - 0.6 * math.exp(-0.3 * 1)
    o_d = _diff(lv, qd, kd, vd, l1_diff_subln_g.reshape(1, -1), lam_init)
    gqa_rows = np.concatenate([hd * 64 + np.arange(64) for hd in GQA_PAIR_HEADS] + [512 + np.arange(512)])
    f1 = _ffn_params(l1_norm2_g, l1_w_out, l1_ffn_w_gate, l1_ffn_w_up, l1_ffn_w_down, row_perm=gqa_rows)
    return _ffn(xa, o_c, o_d, mod1, f1, N_LAT_TILES)
```

```python
import functools
import math

import numpy as np
import jax
import jax.numpy as jnp
from jax import lax
from jax.experimental import pallas as pl
from jax.experimental.pallas import tpu as pltpu

F32 = jnp.float32
BF16 = jnp.bfloat16

D_MODEL = 1024
SEQ = 2048
GRID_W = 64
CTX_LEN = 256
N_TOK = SEQ + CTX_LEN
ROPE_THETA = 10000.0
EPS = 1e-6

MLA_HEADS = 8
MLA_Q_RANK = 256
MLA_KV_RANK = 128
MLA_NOPE = 64
MLA_ROPE = 32
MLA_V = 64
MLA_QK = MLA_NOPE + MLA_ROPE

NA_HEADS = 8
NA_HEAD_DIM = 64
NA_ROWS_MAX = 8
NA_COLS = 16
NA_DIM = NA_HEADS * NA_HEAD_DIM

GQA_HEADS = 8
GQA_KV_HEADS = 2
GQA_HEAD_DIM = 64
GQA_WINDOW = 128

DIFF_HEADS = 4
DIFF_HEAD_DIM = 64

FFN_HIDDEN = 2816

LANES = 128
TOK_TILE = 256
N_TILES = N_TOK // TOK_TILE
N_LAT_TILES = SEQ // TOK_TILE
VMEM_LIMIT = 56 * 1024 * 1024
LOG2E = 1.4426950408889634
NEG_BIG = -1e30

NA_Q_ROWS = 2
NA_TQ = NA_Q_ROWS * GRID_W
NA_WIN_ROWS = 10
NA_WIN = NA_WIN_ROWS * GRID_W
NA_STEPS = SEQ // NA_TQ
GRID_ROWS = SEQ // GRID_W


def _dot(a, b):
    return jnp.dot(a, b, preferred_element_type=F32)


def _dot_nt(a, b):
    return lax.dot_general(a, b, (((1,), (1,)), ((), ())), preferred_element_type=F32)


def _dot_hilo(a, m):
    hi = a.astype(BF16)
    lo = (a - hi.astype(F32)).astype(BF16)
    return _dot(hi, m) + _dot(lo, m)


def _rms_rows(x):
    return lax.rsqrt(jnp.mean(x * x, axis=-1, keepdims=True) + EPS)


def _const_spec(a):
    nd = a.ndim
    return pl.BlockSpec(a.shape, lambda *_: (0,) * nd)


def _params(n_grid):
    return pltpu.CompilerParams(dimension_semantics=("arbitrary",) * n_grid,
                                vmem_limit_bytes=VMEM_LIMIT)


def _softmax_parts(s_list, extra=None):
    m = jnp.max(s_list[0], axis=-1, keepdims=True)
    for s in s_list[1:]:
        m = jnp.maximum(m, jnp.max(s, axis=-1, keepdims=True))
    if extra is not None:
        m = jnp.maximum(m, extra)
    ps = [jnp.exp2(s - m) for s in s_list]
    l = jnp.sum(ps[0], axis=-1, keepdims=True)
    for p in ps[1:]:
        l = l + jnp.sum(p, axis=-1, keepdims=True)
    if extra is not None:
        l = l + jnp.exp2(extra - m)
    return ps, l


def _stack_halves(q):
    lane = lax.broadcasted_iota(jnp.int32, q.shape, 1)
    zero = jnp.zeros_like(q)
    return jnp.concatenate([jnp.where(lane < 64, q, zero), jnp.where(lane >= 64, q, zero)], axis=0)


def _merge_halves(r, tq):
    lane = lax.broadcasted_iota(jnp.int32, (tq, LANES), 1)
    return jnp.where(lane < 64, r[:tq], r[tq:])


def _ada_kernel(c_ref, w_ref, b_ref, o_ref):
    c = c_ref[...]
    a = (c * (1.0 / (1.0 + jnp.exp(-c)))).astype(BF16)
    o_ref[...] = _dot(a, w_ref[...].astype(BF16)) + b_ref[...]


def _ada(cond, w, b):
    n = w.shape[1]
    tn = 1024
    return pl.pallas_call(
        _ada_kernel,
        grid=(n // tn,),
        in_specs=[pl.BlockSpec(cond.shape, lambda j: (0, 0)),
                  pl.BlockSpec((w.shape[0], tn), lambda j: (0, j)),
                  pl.BlockSpec((1, tn), lambda j: (0, j))],
        out_specs=pl.BlockSpec((cond.shape[0], tn), lambda j: (0, j)),
        out_shape=jax.ShapeDtypeStruct((cond.shape[0], n), F32),
        compiler_params=_params(1),
        name="ada_modulation",
    )(cond, w, b.reshape(1, n))


def _proj0_kernel(x_ref, mod_ref, g1_ref, w0_ref, qag_ref, wuq_ref, kvag_ref, wukv_ref, e_ref,
                  sa_ref, sat_ref, sb_ref, sbt_ref, aq_ref, bq_ref, ak_ref, bk_ref, gqb_ref, gkb_ref,
                  qa_ref, ka_ref, va_ref, qb_ref, kb_ref, vb_ref):
    x = x_ref[0]
    mod = mod_ref[0, 0]
    h = x * _rms_rows(x) * g1_ref[...] * (1.0 + mod[1:2]) + mod[0:1]
    z = _dot(h.astype(BF16), w0_ref[...])

    cq = z[:, 0:256]
    cqn = (cq * _rms_rows(cq) * qag_ref[...]).astype(BF16)
    zq = _dot(cqn, wuq_ref[...])
    qm, qr = zq[:, :1024], zq[:, 1024:]
    inv = lax.rsqrt(_dot_hilo(qm * qm, sa_ref[...]) * (1.0 / MLA_QK) + EPS)
    qa_ref[0] = ((qm * aq_ref[...] + qr * bq_ref[...]) * _dot_hilo(inv, sat_ref[...])).astype(BF16)

    ckv = z[:, 256:384]
    ckvn = (ckv * _rms_rows(ckv) * kvag_ref[...]).astype(BF16)
    zkv = _dot(ckvn, wukv_ref[...])
    krp = _dot_hilo(z[:, 384:512], e_ref[...])
    km = zkv[:, :1024] + krp[:, :1024]
    kr = krp[:, 1024:]
    inv = lax.rsqrt(_dot_hilo(km * km, sa_ref[...]) * (1.0 / MLA_QK) + EPS)
    ka_ref[0] = ((km * ak_ref[...] + kr * bk_ref[...]) * _dot_hilo(inv, sat_ref[...])).astype(BF16)
    va_ref[0] = zkv[:, 1024:].astype(BF16)

    qn, kn = z[:, 512:1024], z[:, 1024:1536]
    inv = lax.rsqrt(_dot_hilo(qn * qn, sb_ref[...]) * (1.0 / NA_HEAD_DIM) + EPS)
    qb_ref[0] = (qn * gqb_ref[...] * _dot_hilo(inv, sbt_ref[...])).astype(BF16)
    inv = lax.rsqrt(_dot_hilo(kn * kn, sb_ref[...]) * (1.0 / NA_HEAD_DIM) + EPS)
    kb_ref[0] = (kn * gkb_ref[...] * _dot_hilo(inv, sbt_ref[...])).astype(BF16)
    vb_ref[0] = z[:, 1536:2048].astype(BF16)


def _tok_spec(width):
    return pl.BlockSpec((1, TOK_TILE, width), lambda t, b: (b, t, 0))


def _mod_spec():
    return pl.BlockSpec((1, 1, 6, D_MODEL), lambda t, b: (b, t // N_LAT_TILES, 0, 0))


def _table_spec(width):
    return pl.BlockSpec((TOK_TILE, width), lambda t, b: (t, 0))


def _proj0(xa, mod, p):
    bsz = xa.shape[0]
    consts1 = [p['g1'], p['w0'], p['qag'], p['wuq'], p['kvag'], p['wukv'], p['e'],
               p['sa'], p['sat'], p['sb'], p['sbt']]
    tables = [p['aq'], p['bq'], p['ak'], p['bk']]
    consts2 = [p['gqb'], p['gkb']]
    widths = [1024, 1024, 512, 512, 512, 512]
    return pl.pallas_call(
        _proj0_kernel,
        grid=(N_TILES, bsz),
        in_specs=([_tok_spec(D_MODEL), _mod_spec()] + [_const_spec(a) for a in consts1]
                  + [_table_spec(1024) for _ in tables] + [_const_spec(a) for a in consts2]),
        out_specs=[_tok_spec(w) for w in widths],
        out_shape=[jax.ShapeDtypeStruct((bsz, N_TOK, w), BF16) for w in widths],
        compiler_params=_params(2),
        name="proj_layer0",
    )(xa, mod, *consts1, *tables, *consts2)


N1_NORM = 1664
N1_MAIN = 2304


def _proj1_kernel(x_ref, mod_ref, g1_ref, w1_ref, s_ref, st_ref, a_ref, b_ref,
                  qc_ref, kc_ref, vc_ref, qd_ref, kd_ref, vd_ref):
    x = x_ref[0]
    mod = mod_ref[0, 0]
    h = x * _rms_rows(x) * g1_ref[...] * (1.0 + mod[1:2]) + mod[0:1]
    z = _dot(h.astype(BF16), w1_ref[...])
    zm, zr = z[:, :N1_NORM], z[:, N1_MAIN:]
    inv = lax.rsqrt(_dot_hilo(zm * zm, s_ref[...]) * (1.0 / 64.0) + EPS)
    y = ((zm * a_ref[...] + zr * b_ref[...]) * _dot_hilo(inv, st_ref[...])).astype(BF16)
    qc_ref[0] = y[:, 0:512]
    kc_ref[0] = y[:, 512:640]
    qd_ref[0] = y[:, 640:1152]
    kd_ref[0] = y[:, 1152:1664]
    vc_ref[0] = z[:, 1664:1792].astype(BF16)
    vd_ref[0] = z[:, 1792:2304].astype(BF16)


def _proj1(xa, mod, p):
    bsz = xa.shape[0]
    consts = [p['g1'], p['w1'], p['s'], p['st']]
    widths = [512, 128, 128, 512, 512, 512]
    return pl.pallas_call(
        _proj1_kernel,
        grid=(N_TILES, bsz),
        in_specs=([_tok_spec(D_MODEL), _mod_spec()] + [_const_spec(a) for a in consts]
                  + [_table_spec(N1_NORM), _table_spec(N1_NORM)]),
        out_specs=[_tok_spec(w) for w in widths],
        out_shape=[jax.ShapeDtypeStruct((bsz, N_TOK, w), BF16) for w in widths],
        compiler_params=_params(2),
        name="proj_layer1",
    )(xa, mod, *consts, p['a'], p['b'])


def _mla_attend(q_ref, k, v, o_ref, tq):
    outs = []
    for hh in range(2):
        q = q_ref[0, :, hh * LANES:(hh + 1) * LANES]
        s = _dot_nt(q, k[hh])
        (p,), l = _softmax_parts([s])
        outs.append(_dot(p.astype(BF16), v) / l)
    o_ref[0] = _merge_halves(jnp.concatenate(outs, axis=0), tq).astype(BF16)


def _mla_kernel(q_ref, k_ref, v_ref, o_ref, *, tq):
    t = pl.program_id(2)

    @pl.when(t < N_LAT_TILES)
    def _():
        k = [k_ref[0, :, hh * LANES:(hh + 1) * LANES] for hh in range(2)]
        _mla_attend(q_ref, k, v_ref[0], o_ref, tq)

    @pl.when(t >= N_LAT_TILES)
    def _():
        k = [k_ref[0, SEQ:, hh * LANES:(hh + 1) * LANES] for hh in range(2)]
        _mla_attend(q_ref, k, v_ref[0, SEQ:, :], o_ref, tq)


def _mla(qa, ka, va):
    bsz = qa.shape[0]
    tq = TOK_TILE
    return pl.pallas_call(
        functools.partial(_mla_kernel, tq=tq),
        grid=(bsz, MLA_HEADS // 2, N_TOK // tq),
        in_specs=[pl.BlockSpec((1, tq, 2 * LANES), lambda b, j, t: (b, t, j)),
                  pl.BlockSpec((1, N_TOK, 2 * LANES), lambda b, j, t: (b, 0, j)),
                  pl.BlockSpec((1, N_TOK, LANES), lambda b, j, t: (b, 0, j))],
        out_specs=pl.BlockSpec((1, tq, LANES), lambda b, j, t: (b, t, j)),
        out_shape=jax.ShapeDtypeStruct((bsz, N_TOK, MLA_HEADS * MLA_V), BF16),
        compiler_params=_params(3),
        name="mla_attention",
    )(qa, ka, va)


def _na_kernel(q_ref, k_ref, v_ref, t_ref, o_ref):
    u = pl.program_id(1)

    @pl.when(u < NA_STEPS)
    def _():
        w0 = jnp.minimum(jnp.clip(NA_Q_ROWS * u - NA_ROWS_MAX // 2, 0, GRID_ROWS - NA_ROWS_MAX),
                         GRID_ROWS - NA_WIN_ROWS)
        ws = pl.multiple_of(w0 * GRID_W, GRID_W)
        for j in range(NA_HEADS // 2):
            cols = slice(j * LANES, (j + 1) * LANES)
            qq = _stack_halves(q_ref[0, :, cols])
            s_nb = _dot_nt(qq, k_ref[0, pl.ds(ws, NA_WIN), cols]) + t_ref[0, j]
            s_cx = _dot_nt(qq, k_ref[0, SEQ:, cols])
            (p_nb, p_cx), l = _softmax_parts([s_nb, s_cx])
            r = (_dot(p_nb.astype(BF16), v_ref[0, pl.ds(ws, NA_WIN), cols])
                 + _dot(p_cx.astype(BF16), v_ref[0, SEQ:, cols])) / l
            o_ref[0, :, cols] = _merge_halves(r, NA_TQ).astype(BF16)

    @pl.when(u >= NA_STEPS)
    def _():
        for j in range(NA_HEADS // 2):
            cols = slice(j * LANES, (j + 1) * LANES)
            qq = _stack_halves(q_ref[0, :, cols])
            s = _dot_nt(qq, k_ref[0, SEQ:, cols])
            (p,), l = _softmax_parts([s])
            r = _dot(p.astype(BF16), v_ref[0, SEQ:, cols]) / l
            o_ref[0, :, cols] = _merge_halves(r, NA_TQ).astype(BF16)


def _na_variant(u):
    return jnp.minimum(u, 2) + jnp.maximum(jnp.minimum(u, NA_STEPS - 1) - (NA_STEPS - 3), 0)


def _na(qb, kb, vb, table):
    bsz = qb.shape[0]
    return pl.pallas_call(
        _na_kernel,
        grid=(bsz, N_TOK // NA_TQ),
        in_specs=[pl.BlockSpec((1, NA_TQ, NA_DIM), lambda b, u: (b, u, 0)),
                  pl.BlockSpec((1, N_TOK, NA_DIM), lambda b, u: (b, 0, 0)),
                  pl.BlockSpec((1, N_TOK, NA_DIM), lambda b, u: (b, 0, 0)),
                  pl.BlockSpec((1, NA_HEADS // 2, 2 * NA_TQ, NA_WIN), lambda b, u: (_na_variant(u), 0, 0, 0))],
        out_specs=pl.BlockSpec((1, NA_TQ, NA_DIM), lambda b, u: (b, u, 0)),
        out_shape=jax.ShapeDtypeStruct((bsz, N_TOK, NA_DIM), BF16),
        compiler_params=_params(2),
        name="neighbourhood_attention",
    )(qb, kb, vb, table)


GQA_TQ = 128
GQA_BAND = 3 * GQA_TQ


def _gqa_kernel(sink_ref, q_ref, k_ref, v_ref, o_ref):
    n = pl.program_id(1)
    start = pl.multiple_of(jnp.clip((n - 1) * GQA_TQ, 0, SEQ - GQA_BAND), GQA_TQ)
    row = lax.broadcasted_iota(jnp.int32, (2 * GQA_TQ, GQA_BAND), 0)
    col = lax.broadcasted_iota(jnp.int32, (2 * GQA_TQ, GQA_BAND), 1)
    tpos = n * GQA_TQ + jnp.where(row >= GQA_TQ, row - GQA_TQ, row)
    valid = jnp.abs(tpos - (start + col)) <= GQA_WINDOW
    hi_rows = lax.broadcasted_iota(jnp.int32, (2 * GQA_TQ, 1), 0) >= GQA_TQ
    kb = k_ref[0, pl.ds(start, GQA_BAND), :]
    vb = v_ref[0, pl.ds(start, GQA_BAND), :]
    kc = k_ref[0, SEQ:, :]
    vc = v_ref[0, SEQ:, :]
    for j in range(GQA_HEADS // 2):
        cols = slice(j * LANES, (j + 1) * LANES)
        qq = _stack_halves(q_ref[0, :, cols])
        s_b = jnp.where(valid, _dot_nt(qq, kb), NEG_BIG)
        s_c = _dot_nt(qq, kc)
        sink = jnp.where(hi_rows, sink_ref[j + GQA_HEADS // 2], sink_ref[j]) * LOG2E
        (p_b, p_c), l = _softmax_parts([s_b, s_c], extra=sink)
        r = (_dot(p_b.astype(BF16), vb) + _dot(p_c.astype(BF16), vc)) / l
        o_ref[0, :, cols] = _merge_halves(r, GQA_TQ).astype(BF16)


def _gqa(sink, qc, kc, vc):
    bsz = qc.shape[0]
    return pl.pallas_call(
        _gqa_kernel,
        grid=(bsz, SEQ // GQA_TQ),
        in_specs=[pl.BlockSpec(memory_space=pltpu.SMEM),
                  pl.BlockSpec((1, GQA_TQ, 512), lambda b, n: (b, n, 0)),
                  pl.BlockSpec((1, N_TOK, LANES), lambda b, n: (b, 0, 0)),
                  pl.BlockSpec((1, N_TOK, LANES), lambda b, n: (b, 0, 0))],
        out_specs=pl.BlockSpec((1, GQA_TQ, 512), lambda b, n: (b, n, 0)),
        out_shape=jax.ShapeDtypeStruct((bsz, SEQ, 512), BF16),
        compiler_params=_params(2),
        name="windowed_gqa",
    )(sink, qc, kc, vc)


def _diff_kernel(lv_ref, q_ref, k_ref, v_ref, g_ref, o_ref, *, tq, lam_init):
    lv = lv_ref[...]
    lam = (jnp.exp(jnp.sum(lv[0:1] * lv[1:2], axis=-1, keepdims=True))
           - jnp.exp(jnp.sum(lv[2:3] * lv[3:4], axis=-1, keepdims=True)) + lam_init)
    qq = _stack_halves(q_ref[0])
    s = _dot_nt(qq, k_ref[0])
    (p,), l = _softmax_parts([s])
    hi_rows = lax.broadcasted_iota(jnp.int32, (2 * tq, 1), 0) >= tq
    pn = p * (jnp.where(hi_rows, -lam, 1.0) / l)
    w = (pn[:tq] + pn[tq:]).astype(BF16)
    o = _dot(w, v_ref[0])
    o_ref[0] = (o * _rms_rows(o) * g_ref[...] * (1.0 - lam_init)).astype(BF16)


def _diff(lv, qd, kd, vd, g, lam_init):
    bsz = qd.shape[0]
    tq = TOK_TILE
    return pl.pallas_call(
        functools.partial(_diff_kernel, tq=tq, lam_init=lam_init),
        grid=(bsz, DIFF_HEADS, SEQ // tq),
        in_specs=[_const_spec(lv),
                  pl.BlockSpec((1, tq, LANES), lambda b, h, t: (b, t, h)),
                  pl.BlockSpec((1, N_TOK, LANES), lambda b, h, t: (b, 0, h)),
                  pl.BlockSpec((1, N_TOK, LANES), lambda b, h, t: (b, 0, h)),
                  _const_spec(g)],
        out_specs=pl.BlockSpec((1, tq, LANES), lambda b, h, t: (b, t, h)),
        out_shape=jax.ShapeDtypeStruct((bsz, SEQ, DIFF_HEADS * 2 * DIFF_HEAD_DIM), BF16),
        compiler_params=_params(3),
        name="diff_attention",
    )(lv, qd, kd, vd, g)


def _ffn_kernel(x_ref, o1_ref, o2_ref, mod_ref, g2_ref, wo1_ref, wo2_ref, wg_ref, wu_ref, wd_ref, y_ref):
    mod = mod_ref[0, 0]
    attn = _dot(o1_ref[0], wo1_ref[...]) + _dot(o2_ref[0], wo2_ref[...])
    x1 = x_ref[0] + mod[2:3] * attn
    h = (x1 * _rms_rows(x1) * g2_ref[...] * (1.0 + mod[4:5]) + mod[3:4]).astype(BF16)
    gate = _dot(h, wg_ref[...])
    up = _dot(h, wu_ref[...])
    a = (gate * (1.0 / (1.0 + jnp.exp(-gate))) * up).astype(BF16)
    y_ref[0] = x1 + mod[5:6] * _dot(a, wd_ref[...])


def _ffn(xa, o1, o2, mod, p, n_tiles):
    bsz = xa.shape[0]
    consts = [p['g2'], p['wo1'], p['wo2'], p['wg'], p['wu'], p['wd']]
    return pl.pallas_call(
        _ffn_kernel,
        grid=(n_tiles, bsz),
        in_specs=([_tok_spec(D_MODEL), _tok_spec(512), _tok_spec(512), _mod_spec()]
                  + [_const_spec(a) for a in consts]),
        out_specs=_tok_spec(D_MODEL),
        out_shape=jax.ShapeDtypeStruct((bsz, n_tiles * TOK_TILE, D_MODEL), F32),
        compiler_params=_params(2),
        name="outproj_ffn",
    )(xa, o1, o2, mod, *consts)


def _rope_spec(dr):
    h = dr // 2
    hh = h // 2
    i = np.arange(dr)
    loc = i % h
    partner = np.where(loc < hh, i + hh, i - hh)
    sign = np.where(loc < hh, -1.0, 1.0).astype(np.float32)
    fidx = loc % hh
    axis = i // h
    return partner, sign, fidx, axis, h


def _rope_cos_sin(dr):
    _, _, fidx, axis, h = _rope_spec(dr)
    freqs = ROPE_THETA ** (-jnp.arange(0, h, 2, dtype=F32) / h)
    t = jnp.arange(SEQ)
    pos = jnp.stack([t // GRID_W, t % GRID_W], axis=1).astype(F32)
    ang = pos[:, axis] * freqs[fidx][None, :]
    cos = jnp.concatenate([jnp.cos(ang), jnp.ones((CTX_LEN, dr), F32)], axis=0)
    sin = jnp.concatenate([jnp.sin(ang), jnp.zeros((CTX_LEN, dr), F32)], axis=0)
    return cos, sin


def _take_cols(w, src):
    wz = jnp.concatenate([w, jnp.zeros((w.shape[0], 1), w.dtype)], axis=1)
    return jnp.take(wz, jnp.asarray(np.where(src < 0, w.shape[1], src)), axis=1)


def _seg_onehot(width, seg):
    s = np.zeros((width, LANES), np.float32)
    s[np.arange(width), np.arange(width) // seg] = 1.0
    return s


def _layer0_params(w_in, qa_g, w_uq, kva_g, w_ukv, qn_g, kn_g, na_qn_g, na_kn_g, norm1_g):
    partner, sign, _, _, _ = _rope_spec(MLA_ROPE)
    cos, sin = _rope_cos_sin(MLA_ROPE)
    o_kr = MLA_Q_RANK + MLA_KV_RANK
    src = np.concatenate([np.arange(0, o_kr), o_kr + np.arange(MLA_ROPE), o_kr + partner,
                          np.full(64, -1), np.arange(o_kr + MLA_ROPE, o_kr + MLA_ROPE + 3 * NA_DIM)])
    w0 = _take_cols(w_in, src).astype(BF16)
    src_m = np.full(2 * 1024, -1)
    a_idx = np.full(1024, -1)
    b_idx = np.full(1024, -1)
    r_idx = np.full(1024, -1)
    for hd in range(MLA_HEADS):
        base = hd * LANES
        src_m[base:base + MLA_QK] = hd * MLA_QK + np.arange(MLA_QK)
        src_m[1024 + base + MLA_NOPE:1024 + base + MLA_QK] = hd * MLA_QK + MLA_NOPE + partner
        a_idx[base:base + MLA_QK] = np.arange(MLA_QK)
        b_idx[base + MLA_NOPE:base + MLA_QK] = MLA_NOPE + partner
        r_idx[base + MLA_NOPE:base + MLA_QK] = np.arange(MLA_ROPE)
    wuq = _take_cols(w_uq, src_m).astype(BF16)

    def tables(g, c):
        gz = jnp.concatenate([g, jnp.zeros((1,), F32)])
        ga = gz[np.where(a_idx < 0, MLA_QK, a_idx)]
        gb = gz[np.where(b_idx < 0, MLA_QK, b_idx)]
        rot = r_idx >= 0
        ri = np.where(rot, r_idx, 0)
        cosl = jnp.where(rot[None, :], cos[:, ri], 1.0)
        sinl = jnp.where(rot[None, :], sin[:, ri] * sign[ri][None, :], 0.0)
        return (ga[None, :] * cosl * c).astype(F32), (gb[None, :] * sinl * c).astype(F32)

    aq, bq = tables(qn_g, MLA_QK ** -0.5 * LOG2E)
    ak, bk = tables(kn_g, 1.0)
    src_kv = np.full(1024 + 512, -1)
    for hd in range(MLA_HEADS):
        src_kv[hd * LANES:hd * LANES + MLA_NOPE] = hd * (MLA_NOPE + MLA_V) + np.arange(MLA_NOPE)
        src_kv[1024 + hd * MLA_V:1024 + (hd + 1) * MLA_V] = hd * (MLA_NOPE + MLA_V) + MLA_NOPE + np.arange(MLA_V)
    wukv = _take_cols(w_ukv, src_kv).astype(BF16)
    e = np.zeros((LANES, 2048), np.float32)
    for hd in range(MLA_HEADS):
        for i in range(MLA_ROPE):
            e[i, hd * LANES + MLA_NOPE + i] = 1.0
            e[MLA_ROPE + i, 1024 + hd * LANES + MLA_NOPE + i] = 1.0
    sa = _seg_onehot(1024, LANES)
    sb = _seg_onehot(NA_DIM, NA_HEAD_DIM)
    return dict(
        g1=norm1_g.reshape(1, -1), w0=w0, qag=qa_g.reshape(1, -1), wuq=wuq, kvag=kva_g.reshape(1, -1),
        wukv=wukv, e=jnp.asarray(e, BF16), sa=jnp.asarray(sa, BF16), sat=jnp.asarray(sa.T, BF16),
        sb=jnp.asarray(sb, BF16), sbt=jnp.asarray(sb.T, BF16), aq=aq, bq=bq, ak=ak, bk=bk,
        gqb=(jnp.tile(na_qn_g, NA_HEADS) * (NA_HEAD_DIM ** -0.5 * LOG2E)).reshape(1, -1),
        gkb=jnp.tile(na_kn_g, NA_HEADS).reshape(1, -1))


GQA_PAIR_HEADS = np.array([0, 4, 1, 5, 2, 6, 3, 7])


def _layer1_params(w_in, gqa_qn_g, gqa_kn_g, diff_qn_g, diff_kn_g, norm1_g):
    partner, sign, _, _, _ = _rope_spec(64)
    cos, sin = _rope_cos_sin(64)
    i64 = np.arange(64)
    qc_cols = np.concatenate([hd * 64 + i64 for hd in GQA_PAIR_HEADS])
    seg = [(qc_cols, gqa_qn_g, GQA_HEAD_DIM ** -0.5 * LOG2E),
           (512 + np.arange(128), gqa_kn_g, 1.0),
           (768 + np.arange(512), diff_qn_g, DIFF_HEAD_DIM ** -0.5 * LOG2E),
           (1280 + np.arange(512), diff_kn_g, 1.0)]
    main = np.concatenate([s[0] for s in seg])
    rot = np.concatenate([s[0].reshape(-1, 64)[:, partner].reshape(-1) for s in seg])
    src = np.concatenate([main, 640 + np.arange(128), 1792 + np.arange(512), rot])
    w1 = jnp.take(w_in, jnp.asarray(src), axis=1).astype(BF16)
    a_parts, b_parts = [], []
    for cols, g, c in seg:
        nh = cols.shape[0] // 64
        a_parts.append(jnp.tile(g[None, :] * cos * c, (1, nh)))
        b_parts.append(jnp.tile(g[partner][None, :] * sin * sign[None, :] * c, (1, nh)))
    s = _seg_onehot(N1_NORM, 64)
    return dict(g1=norm1_g.reshape(1, -1), w1=w1, s=jnp.asarray(s, BF16), st=jnp.asarray(s.T, BF16),
                a=jnp.concatenate(a_parts, axis=1).astype(F32), b=jnp.concatenate(b_parts, axis=1).astype(F32))


def _na_table(rpb):
    us = np.array([0, 1, 2, NA_STEPS - 2, NA_STEPS - 1])
    ri = np.arange(2 * NA_TQ)
    kk = np.arange(NA_WIN)
    head = 2 * np.arange(NA_HEADS // 2)[None, :, None, None] + (ri // NA_TQ)[None, None, :, None]
    qi = ri % NA_TQ
    r = NA_Q_ROWS * us[:, None, None, None] + (qi // GRID_W)[None, None, :, None]
    c = (qi % GRID_W)[None, None, :, None]
    w0 = np.minimum(np.clip(NA_Q_ROWS * us - NA_ROWS_MAX // 2, 0, GRID_ROWS - NA_ROWS_MAX), GRID_ROWS - NA_WIN_ROWS)
    key_r = w0[:, None, None, None] + (kk // GRID_W)[None, None, None, :]
    key_c = (kk % GRID_W)[None, None, None, :]
    r0 = np.clip(r - NA_ROWS_MAX // 2, 0, GRID_ROWS - NA_ROWS_MAX)
    c0 = np.clip(c - NA_COLS // 2, 0, GRID_W - NA_COLS)
    valid = (key_r >= r0) & (key_r < r0 + NA_ROWS_MAX) & (key_c >= c0) & (key_c < c0 + NA_COLS)
    shape = (len(us), NA_HEADS // 2, 2 * NA_TQ, NA_WIN)
    valid = np.broadcast_to(valid, shape)
    n_off = 2 * NA_COLS - 1
    pad = GRID_W - NA_COLS
    w = jnp.pad(rpb, ((0, 0), (0, 0), (pad, 2 * GRID_W - pad - n_off)))
    flat = jnp.tile(w, (1, 1, GRID_W))[:, :, :GRID_W * (2 * GRID_W - 1)]
    toep = flat.reshape(NA_HEADS, 2 * NA_ROWS_MAX - 1, GRID_W, 2 * GRID_W - 1)[:, :, :, GRID_W - 1:]
    a_row = np.arange(NA_Q_ROWS)
    j_row = np.arange(NA_WIN_ROWS)
    off_r = np.clip(w0[:, None, None] + j_row[None, None, :] - (NA_Q_ROWS * us[:, None, None] + a_row[None, :, None])
                    + (NA_ROWS_MAX - 1), 0, 2 * NA_ROWS_MAX - 2)
    sel = jnp.stack([toep[:, int(o)] for o in off_r.reshape(-1)], axis=1)
    sel = sel.reshape(NA_HEADS // 2, 2, len(us), NA_Q_ROWS, NA_WIN_ROWS, GRID_W, GRID_W)
    bias = jnp.transpose(sel, (2, 0, 1, 3, 5, 4, 6)).reshape(shape) * LOG2E
    return jnp.where(jnp.asarray(valid), bias, NEG_BIG).astype(F32)


def _ffn_params(norm2_g, w_out, w_gate, w_up, w_down, row_perm=None):
    wo = w_out if row_perm is None else jnp.take(w_out, jnp.asarray(row_perm), axis=0)
    return dict(g2=norm2_g.reshape(1, -1), wo1=wo[:512].astype(BF16), wo2=wo[512:].astype(BF16),
                wg=w_gate.astype(BF16), wu=w_up.astype(BF16), wd=w_down.astype(BF16))


def _mod_rows(mod_all, bsz):
    lat = mod_all[:bsz].reshape(bsz, 1, 6, D_MODEL)
    cx = jnp.broadcast_to(mod_all[bsz].reshape(1, 1, 6, D_MODEL), (bsz, 1, 6, D_MODEL))
    return jnp.concatenate([lat, cx], axis=1)


def kernel(x, c, ctx, c_ctx, l0_ada_w, l0_ada_b, l0_norm1_g, l0_norm2_g, l0_w_in, l0_mla_qa_g, l0_mla_w_uq, l0_mla_kva_g, l0_mla_w_ukv, l0_mla_qn_g, l0_mla_kn_g, l0_na_qn_g, l0_na_kn_g, l0_na_rpb, l0_w_out, l0_ffn_w_gate, l0_ffn_w_up, l0_ffn_w_down, l1_ada_w, l1_ada_b, l1_norm1_g, l1_norm2_g, l1_w_in, l1_gqa_qn_g, l1_gqa_kn_g, l1_gqa_sink, l1_diff_qn_g, l1_diff_kn_g, l1_diff_lq1, l1_diff_lk1, l1_diff_lq2, l1_diff_lk2, l1_diff_subln_g, l1_w_out, l1_ffn_w_gate, l1_ffn_w_up, l1_ffn_w_down):
    bsz = x.shape[0]
    assert x.shape[1:] == (SEQ, D_MODEL) and ctx.shape[1:] == (CTX_LEN, D_MODEL)
    rows = -(-(bsz + 1) // 8) * 8
    cond = jnp.concatenate([c, c_ctx[None, :], jnp.zeros((rows - bsz - 1, D_MODEL), F32)], axis=0)
    mod0 = _mod_rows(_ada(cond, l0_ada_w, l0_ada_b), bsz)
    mod1 = _mod_rows(_ada(cond, l1_ada_w, l1_ada_b), bsz)
    xa = jnp.concatenate([x, ctx], axis=1)

    p0 = _layer0_params(l0_w_in, l0_mla_qa_g, l0_mla_w_uq, l0_mla_kva_g, l0_mla_w_ukv, l0_mla_qn_g,
                        l0_mla_kn_g, l0_na_qn_g, l0_na_kn_g, l0_norm1_g)
    qa, ka, va, qb, kb, vb = _proj0(xa, mod0, p0)
    o_a = _mla(qa, ka, va)
    o_b = _na(qb, kb, vb, _na_table(l0_na_rpb))
    f0 = _ffn_params(l0_norm2_g, l0_w_out, l0_ffn_w_gate, l0_ffn_w_up, l0_ffn_w_down)
    xa = _ffn(xa, o_a, o_b, mod0, f0, N_TILES)

    p1 = _layer1_params(l1_w_in, l1_gqa_qn_g, l1_gqa_kn_g, l1_diff_qn_g, l1_diff_kn_g, l1_norm1_g)
    qc, kc, vc, qd, kd, vd = _proj1(xa, mod1, p1)
    o_c = _gqa(l1_gqa_sink, qc, kc, vc)
    lv = jnp.zeros((8, LANES), F32).at[:4, :DIFF_HEAD_DIM].set(
        jnp.stack([l1_diff_lq1, l1_diff_lk1, l1_diff_lq2, l1_diff_lk2]))
    lam_init = 0.8 - 0.6 * math.exp(-0.3 * 1)
    o_d = _diff(lv, qd, kd, vd, l1_diff_subln_g.reshape(1, -1), lam_init)
    gqa_rows = np.concatenate([hd * 64 + np.arange(64) for hd in GQA_PAIR_HEADS] + [512 + np.arange(512)])
    f1 = _ffn_params(l1_norm2_g, l1_w_out, l1_ffn_w_gate, l1_ffn_w_up, l1_ffn_w_down, row_perm=gqa_rows)
    return _ffn(xa, o_c, o_d, mod1, f1, N_LAT_TILES)
```

```python
import functools
import math

import numpy as np
import jax
import jax.numpy as jnp
from jax import lax
from jax.experimental import pallas as pl
from jax.experimental.pallas import tpu as pltpu

F32 = jnp.float32
BF16 = jnp.bfloat16

D_MODEL = 1024
SEQ = 2048
GRID_W = 64
CTX_LEN = 256
N_TOK = SEQ + CTX_LEN
ROPE_THETA = 10000.0
EPS = 1e-6

MLA_HEADS = 8
MLA_Q_RANK = 256
MLA_KV_RANK = 128
MLA_NOPE = 64
MLA_ROPE = 32
MLA_V = 64
MLA_QK = MLA_NOPE + MLA_ROPE

NA_HEADS = 8
NA_HEAD_DIM = 64
NA_ROWS_MAX = 8
NA_COLS = 16
NA_DIM = NA_HEADS * NA_HEAD_DIM

GQA_HEADS = 8
GQA_KV_HEADS = 2
GQA_HEAD_DIM = 64
GQA_WINDOW = 128

DIFF_HEADS = 4
DIFF_HEAD_DIM = 64

FFN_HIDDEN = 2816

LANES = 128
TOK_TILE = 256
N_TILES = N_TOK // TOK_TILE
N_LAT_TILES = SEQ // TOK_TILE
VMEM_LIMIT = 56 * 1024 * 1024
LOG2E = 1.4426950408889634
NEG_BIG = -1e30

ATT_TQ = 256
NA_Q_ROWS = 2
NA_TQ = NA_Q_ROWS * GRID_W
NA_WIN_ROWS = 10
NA_WIN = NA_WIN_ROWS * GRID_W
NA_STEPS = SEQ // NA_TQ
GRID_ROWS = SEQ // GRID_W
GQA_TQ = 128
GQA_BAND = 3 * GQA_TQ


def _dot(a, b):
    return jnp.dot(a, b, preferred_element_type=F32)


def _dot_nt(a, b):
    return lax.dot_general(a, b, (((1,), (1,)), ((), ())), preferred_element_type=F32)


def _dot_hilo(a, m):
    hi = a.astype(BF16)
    lo = (a - hi.astype(F32)).astype(BF16)
    return _dot(hi, m) + _dot(lo, m)


def _rms_rows(x):
    return lax.rsqrt(jnp.mean(x * x, axis=-1, keepdims=True) + EPS)


def _const_spec(a):
    nd = a.ndim
    return pl.BlockSpec(a.shape, lambda *_: (0,) * nd)


def _params(n_grid):
    return pltpu.CompilerParams(dimension_semantics=("arbitrary",) * n_grid,
                                vmem_limit_bytes=VMEM_LIMIT)


def _row0(i, size):
    return i * size if isinstance(i, int) else pl.multiple_of(i * size, size)


def _fold_lanes(xs, op):
    r = None
    for x in xs:
        for lo in range(0, x.shape[1], LANES):
            blk = x[:, lo:lo + LANES]
            r = blk if r is None else op(r, blk)
    return r


def _scores(q, keys, biases, s_ref, mx_ref):
    parts = []
    lo = 0
    for k, bias in zip(keys, biases):
        sc = _dot_nt(q, k)
        if bias is not None:
            sc = sc + bias
        s_ref[:, lo:lo + k.shape[0]] = sc
        lo += k.shape[0]
        parts.append(sc)
    mx_ref[...] = _fold_lanes(parts, jnp.maximum)


def _softmax_pv(s_ref, mx_ref, vals, extra=None):
    m = jnp.max(mx_ref[...], axis=-1, keepdims=True)
    if extra is not None:
        m = jnp.maximum(m, extra)
    ps = []
    lo = 0
    for v in vals:
        ps.append(jnp.exp2(s_ref[:, lo:lo + v.shape[0]] - m))
        lo += v.shape[0]
    l = jnp.sum(_fold_lanes(ps, jnp.add), axis=-1, keepdims=True)
    if extra is not None:
        l = l + jnp.exp2(extra - m)
    acc = _dot(ps[0].astype(BF16), vals[0])
    for p, v in zip(ps[1:], vals[1:]):
        acc = acc + _dot(p.astype(BF16), v)
    return acc, l


def _skewed_loop(n, scores, finish):
    scores(0, 0)

    def body(j, carry):
        i = 2 * j
        scores(i + 1, 1)
        finish(i, 0)
        scores(jnp.minimum(i + 2, n - 1), 0)
        finish(i + 1, 1)
        return carry

    lax.fori_loop(0, n // 2, body, 0)


def _stack_halves(q):
    lane = lax.broadcasted_iota(jnp.int32, q.shape, 1)
    zero = jnp.zeros_like(q)
    return jnp.concatenate([jnp.where(lane < 64, q, zero), jnp.where(lane >= 64, q, zero)], axis=0)


def _merge_halves(lo, hi):
    lane = lax.broadcasted_iota(jnp.int32, lo.shape, 1)
    return jnp.where(lane < 64, lo, hi)


def _ada_kernel(c_ref, w_ref, b_ref, o_ref):
    c = c_ref[...]
    a = (c * (1.0 / (1.0 + jnp.exp(-c)))).astype(BF16)
    o_ref[...] = _dot(a, w_ref[...].astype(BF16)) + b_ref[...]


def _ada(cond, w, b):
    n = w.shape[1]
    tn = 1024
    return pl.pallas_call(
        _ada_kernel,
        grid=(n // tn,),
        in_specs=[pl.BlockSpec(cond.shape, lambda j: (0, 0)),
                  pl.BlockSpec((w.shape[0], tn), lambda j: (0, j)),
                  pl.BlockSpec((1, tn), lambda j: (0, j))],
        out_specs=pl.BlockSpec((cond.shape[0], tn), lambda j: (0, j)),
        out_shape=jax.ShapeDtypeStruct((cond.shape[0], n), F32),
        compiler_params=_params(1),
        name="ada_modulation",
    )(cond, w, b.reshape(1, n))


def _tok_spec(width):
    return pl.BlockSpec((1, TOK_TILE, width), lambda t, b: (b, t, 0))


def _lat_spec():
    return pl.BlockSpec((1, TOK_TILE, D_MODEL), lambda t, b: (b, jnp.minimum(t, N_LAT_TILES - 1), 0))


def _ctx_spec():
    return pl.BlockSpec((1, TOK_TILE, D_MODEL), lambda t, b: (b, 0, 0))


def _mod_spec():
    return pl.BlockSpec((1, 1, 6, D_MODEL), lambda t, b: (b, t // N_LAT_TILES, 0, 0))


def _table_spec(width):
    return pl.BlockSpec((TOK_TILE, width), lambda t, b: (t, 0))


def _read_tokens(x_ref, c_ref):
    return jnp.where(pl.program_id(0) < N_LAT_TILES, x_ref[0], c_ref[0])


def _proj0_kernel(x_ref, c_ref, mod_ref, g1_ref, w0_ref, qag_ref, wuq_ref, kvag_ref, wukv_ref, e_ref,
                  sa_ref, sat_ref, sb_ref, sbt_ref, aq_ref, bq_ref, ak_ref, bk_ref, gqb_ref, gkb_ref,
                  qa_ref, ka_ref, va_ref, qb_ref, kb_ref, vb_ref):
    x = _read_tokens(x_ref, c_ref)
    mod = mod_ref[0, 0]
    h = x * _rms_rows(x) * g1_ref[...] * (1.0 + mod[1:2]) + mod[0:1]
    z = _dot(h.astype(BF16), w0_ref[...])

    cq = z[:, 0:256]
    cqn = (cq * _rms_rows(cq) * qag_ref[...]).astype(BF16)
    zq = _dot(cqn, wuq_ref[...])
    qm, qr = zq[:, :1024], zq[:, 1024:]
    inv = lax.rsqrt(_dot_hilo(qm * qm, sa_ref[...]) * (1.0 / MLA_QK) + EPS)
    invf = _dot_hilo(inv, sat_ref[...])
    aq, bq = aq_ref[...], bq_ref[...]
    for hd in range(MLA_HEADS):
        cols = slice(hd * LANES, (hd + 1) * LANES)
        qa_ref[0, :, cols] = ((qm[:, cols] * aq + qr[:, cols] * bq) * invf[:, cols]).astype(BF16)

    ckv = z[:, 256:384]
    ckvn = (ckv * _rms_rows(ckv) * kvag_ref[...]).astype(BF16)
    zkv = _dot(ckvn, wukv_ref[...])
    krp = _dot_hilo(z[:, 384:512], e_ref[...])
    km = zkv[:, :1024] + krp[:, :1024]
    kr = krp[:, 1024:]
    inv = lax.rsqrt(_dot_hilo(km * km, sa_ref[...]) * (1.0 / MLA_QK) + EPS)
    invf = _dot_hilo(inv, sat_ref[...])
    ak, bk = ak_ref[...], bk_ref[...]
    for hd in range(MLA_HEADS):
        cols = slice(hd * LANES, (hd + 1) * LANES)
        ka_ref[0, :, cols] = ((km[:, cols] * ak + kr[:, cols] * bk) * invf[:, cols]).astype(BF16)
    va_ref[0] = zkv[:, 1024:].astype(BF16)

    qn, kn = z[:, 512:1024], z[:, 1024:1536]
    inv = lax.rsqrt(_dot_hilo(qn * qn, sb_ref[...]) * (1.0 / NA_HEAD_DIM) + EPS)
    qb_ref[0] = (qn * gqb_ref[...] * _dot_hilo(inv, sbt_ref[...])).astype(BF16)
    inv = lax.rsqrt(_dot_hilo(kn * kn, sb_ref[...]) * (1.0 / NA_HEAD_DIM) + EPS)
    kb_ref[0] = (kn * gkb_ref[...] * _dot_hilo(inv, sbt_ref[...])).astype(BF16)
    vb_ref[0] = z[:, 1536:2048].astype(BF16)


def _proj0(x, ctx, mod, p):
    bsz = x.shape[0]
    consts1 = [p['g1'], p['w0'], p['qag'], p['wuq'], p['kvag'], p['wukv'], p['e'],
               p['sa'], p['sat'], p['sb'], p['sbt']]
    tables = [p['aq'], p['bq'], p['ak'], p['bk']]
    consts2 = [p['gqb'], p['gkb']]
    widths = [1024, 1024, 512, 512, 512, 512]
    return pl.pallas_call(
        _proj0_kernel,
        grid=(N_TILES, bsz),
        in_specs=([_lat_spec(), _ctx_spec(), _mod_spec()] + [_const_spec(a) for a in consts1]
                  + [_table_spec(LANES) for _ in tables] + [_const_spec(a) for a in consts2]),
        out_specs=[_tok_spec(w) for w in widths],
        out_shape=[jax.ShapeDtypeStruct((bsz, N_TOK, w), BF16) for w in widths],
        compiler_params=_params(2),
        name="proj_layer0",
    )(x, ctx, mod, *consts1, *tables, *consts2)


N1_SEG = (4, 1, 4, 4)
N1_NORM = 1664
N1_MAIN = 2304


def _proj1_kernel(x_ref, mod_ref, g1_ref, w1_ref, s_ref, st_ref, a_ref, b_ref,
                  qc_ref, kc_ref, vc_ref, qd_ref, kd_ref, vd_ref):
    x = x_ref[0]
    mod = mod_ref[0, 0]
    h = x * _rms_rows(x) * g1_ref[...] * (1.0 + mod[1:2]) + mod[0:1]
    z = _dot(h.astype(BF16), w1_ref[...])
    zm = z[:, :N1_NORM]
    inv = lax.rsqrt(_dot_hilo(zm * zm, s_ref[...]) * (1.0 / 64.0) + EPS)
    invf = _dot_hilo(inv, st_ref[...])
    outs = (qc_ref, kc_ref, qd_ref, kd_ref)
    blk = 0
    for seg, n_blk in enumerate(N1_SEG):
        a = a_ref[:, seg * LANES:(seg + 1) * LANES]
        b = b_ref[:, seg * LANES:(seg + 1) * LANES]
        for j in range(n_blk):
            cols = slice(blk * LANES, (blk + 1) * LANES)
            rot = z[:, N1_MAIN + blk * LANES:N1_MAIN + (blk + 1) * LANES]
            outs[seg][0, :, j * LANES:(j + 1) * LANES] = ((z[:, cols] * a + rot * b) * invf[:, cols]).astype(BF16)
            blk += 1
    vc_ref[0] = z[:, 1664:1792].astype(BF16)
    vd_ref[0] = z[:, 1792:2304].astype(BF16)


def _proj1(xa, mod, p):
    bsz = xa.shape[0]
    consts = [p['g1'], p['w1'], p['s'], p['st']]
    widths = [512, 128, 128, 512, 512, 512]
    return pl.pallas_call(
        _proj1_kernel,
        grid=(N_TILES, bsz),
        in_specs=([_tok_spec(D_MODEL), _mod_spec()] + [_const_spec(a) for a in consts]
                  + [_table_spec(4 * LANES), _table_spec(4 * LANES)]),
        out_specs=[_tok_spec(w) for w in widths],
        out_shape=[jax.ShapeDtypeStruct((bsz, N_TOK, w), BF16) for w in widths],
        compiler_params=_params(2),
        name="proj_layer1",
    )(xa, mod, *consts, p['a'], p['b'])


def _mla_kernel(q_ref, k_ref, v_ref, o_ref, s_scr, mx_scr):
    def scores(r0, slot, key_lo):
        for hh in range(2):
            cols = slice(hh * LANES, (hh + 1) * LANES)
            _scores(q_ref[0, pl.ds(r0, ATT_TQ), cols], [k_ref[0, key_lo:, cols]], [None],
                    s_scr.at[slot, hh], mx_scr.at[slot, hh])

    def finish(r0, slot, key_lo):
        outs = []
        for hh in range(2):
            acc, l = _softmax_pv(s_scr.at[slot, hh], mx_scr.at[slot, hh], [v_ref[0, key_lo:, :]])
            outs.append(acc / l)
        o_ref[0, pl.ds(r0, ATT_TQ), :] = _merge_halves(outs[0], outs[1]).astype(BF16)

    _skewed_loop(SEQ // ATT_TQ,
                 lambda i, slot: scores(_row0(i, ATT_TQ), slot, 0),
                 lambda i, slot: finish(_row0(i, ATT_TQ), slot, 0))
    scores(SEQ, 0, SEQ)
    finish(SEQ, 0, SEQ)


def _mla(qa, ka, va):
    bsz = qa.shape[0]
    assert CTX_LEN == ATT_TQ
    return pl.pallas_call(
        _mla_kernel,
        grid=(bsz, MLA_HEADS // 2),
        in_specs=[pl.BlockSpec((1, N_TOK, 2 * LANES), lambda b, j: (b, 0, j)),
                  pl.BlockSpec((1, N_TOK, 2 * LANES), lambda b, j: (b, 0, j)),
                  pl.BlockSpec((1, N_TOK, LANES), lambda b, j: (b, 0, j))],
        out_specs=pl.BlockSpec((1, N_TOK, LANES), lambda b, j: (b, 0, j)),
        out_shape=jax.ShapeDtypeStruct((bsz, N_TOK, MLA_HEADS * MLA_V), BF16),
        scratch_shapes=[pltpu.VMEM((2, 2, ATT_TQ, N_TOK), F32), pltpu.VMEM((2, 2, ATT_TQ, LANES), F32)],
        compiler_params=_params(2),
        name="mla_attention",
    )(qa, ka, va)


def _na_kernel(q_ref, k_ref, v_ref, t_ref, o_ref, s_scr, mx_scr):
    def window_start(u):
        w0 = jnp.minimum(jnp.clip(NA_Q_ROWS * u - NA_ROWS_MAX // 2, 0, GRID_ROWS - NA_ROWS_MAX),
                         GRID_ROWS - NA_WIN_ROWS)
        return pl.multiple_of(w0 * GRID_W, GRID_W)

    def scores(u, slot):
        ws = window_start(u)
        r0 = _row0(u, NA_TQ)
        var = jnp.minimum(u, 2) + jnp.maximum(u - (NA_STEPS - 3), 0)
        for j in range(NA_HEADS // 2):
            cols = slice(j * LANES, (j + 1) * LANES)
            qq = _stack_halves(q_ref[0, pl.ds(r0, NA_TQ), cols])
            _scores(qq, [k_ref[0, pl.ds(ws, NA_WIN), cols], k_ref[0, SEQ:, cols]], [t_ref[var, j], None],
                    s_scr.at[slot, j], mx_scr.at[slot, j])

    def finish(u, slot):
        ws = window_start(u)
        r0 = _row0(u, NA_TQ)
        for j in range(NA_HEADS // 2):
            cols = slice(j * LANES, (j + 1) * LANES)
            acc, l = _softmax_pv(s_scr.at[slot, j], mx_scr.at[slot, j],
                                 [v_ref[0, pl.ds(ws, NA_WIN), cols], v_ref[0, SEQ:, cols]])
            r = acc / l
            o_ref[0, pl.ds(r0, NA_TQ), cols] = _merge_halves(r[:NA_TQ], r[NA_TQ:]).astype(BF16)

    _skewed_loop(NA_STEPS, scores, finish)
    for r0 in range(SEQ, N_TOK, NA_TQ):
        for j in range(NA_HEADS // 2):
            cols = slice(j * LANES, (j + 1) * LANES)
            qq = _stack_halves(q_ref[0, r0:r0 + NA_TQ, cols])
            _scores(qq, [k_ref[0, SEQ:, cols]], [None], s_scr.at[0, j], mx_scr.at[0, j])
            acc, l = _softmax_pv(s_scr.at[0, j], mx_scr.at[0, j], [v_ref[0, SEQ:, cols]])
            r = acc / l
            o_ref[0, r0:r0 + NA_TQ, cols] = _merge_halves(r[:NA_TQ], r[NA_TQ:]).astype(BF16)


def _na(qb, kb, vb, table):
    bsz = qb.shape[0]
    tok = pl.BlockSpec((1, N_TOK, NA_DIM), lambda b: (b, 0, 0))
    n_pairs = NA_HEADS // 2
    return pl.pallas_call(
        _na_kernel,
        grid=(bsz,),
        in_specs=[tok, tok, tok, _const_spec(table)],
        out_specs=tok,
        out_shape=jax.ShapeDtypeStruct((bsz, N_TOK, NA_DIM), BF16),
        scratch_shapes=[pltpu.VMEM((2, n_pairs, 2 * NA_TQ, NA_WIN + CTX_LEN), F32),
                        pltpu.VMEM((2, n_pairs, 2 * NA_TQ, LANES), F32)],
        compiler_params=_params(1),
        name="neighbourhood_attention",
    )(qb, kb, vb, table)


def _gqa_kernel(sink_ref, q_ref, k_ref, v_ref, o_ref, s_scr, mx_scr):
    row = lax.broadcasted_iota(jnp.int32, (2 * GQA_TQ, GQA_BAND), 0)
    col = lax.broadcasted_iota(jnp.int32, (2 * GQA_TQ, GQA_BAND), 1)
    rel = jnp.where(row >= GQA_TQ, row - GQA_TQ, row) - col
    hi_rows = lax.broadcasted_iota(jnp.int32, (2 * GQA_TQ, 1), 0) >= GQA_TQ

    def band_start(r0):
        return pl.multiple_of(jnp.clip(r0 - GQA_TQ, 0, SEQ - GQA_BAND), GQA_TQ)

    def scores(n, slot):
        r0 = _row0(n, GQA_TQ)
        start = band_start(r0)
        mask = jnp.where(jnp.abs(rel + (r0 - start)) <= GQA_WINDOW, 0.0, NEG_BIG)
        keys = [k_ref[0, pl.ds(start, GQA_BAND), :], k_ref[0, SEQ:, :]]
        for j in range(GQA_HEADS // 2):
            qq = _stack_halves(q_ref[0, pl.ds(r0, GQA_TQ), j * LANES:(j + 1) * LANES])
            _scores(qq, keys, [mask, None], s_scr.at[slot, j], mx_scr.at[slot, j])

    def finish(n, slot):
        r0 = _row0(n, GQA_TQ)
        start = band_start(r0)
        vals = [v_ref[0, pl.ds(start, GQA_BAND), :], v_ref[0, SEQ:, :]]
        for j in range(GQA_HEADS // 2):
            sink = jnp.where(hi_rows, sink_ref[j + GQA_HEADS // 2], sink_ref[j]) * LOG2E
            acc, l = _softmax_pv(s_scr.at[slot, j], mx_scr.at[slot, j], vals, extra=sink)
            r = acc / l
            o_ref[0, pl.ds(r0, GQA_TQ), j * LANES:(j + 1) * LANES] = _merge_halves(r[:GQA_TQ], r[GQA_TQ:]).astype(BF16)

    _skewed_loop(SEQ // GQA_TQ, scores, finish)


def _gqa(sink, qc, kc, vc):
    bsz = qc.shape[0]
    return pl.pallas_call(
        _gqa_kernel,
        grid=(bsz,),
        in_specs=[pl.BlockSpec(memory_space=pltpu.SMEM),
                  pl.BlockSpec((1, SEQ, 512), lambda b: (b, 0, 0)),
                  pl.BlockSpec((1, N_TOK, LANES), lambda b: (b, 0, 0)),
                  pl.BlockSpec((1, N_TOK, LANES), lambda b: (b, 0, 0))],
        out_specs=pl.BlockSpec((1, SEQ, 512), lambda b: (b, 0, 0)),
        out_shape=jax.ShapeDtypeStruct((bsz, SEQ, 512), BF16),
        scratch_shapes=[pltpu.VMEM((2, GQA_HEADS // 2, 2 * GQA_TQ, GQA_BAND + CTX_LEN), F32),
                        pltpu.VMEM((2, GQA_HEADS // 2, 2 * GQA_TQ, LANES), F32)],
        compiler_params=_params(1),
        name="windowed_gqa",
    )(sink, qc, kc, vc)


def _diff_kernel(lv_ref, q_ref, k_ref, v_ref, g_ref, o_ref, s_scr, mx_scr, *, lam_init):
    lv = lv_ref[...]
    lam = (jnp.exp(jnp.sum(lv[0:1] * lv[1:2], axis=-1, keepdims=True))
           - jnp.exp(jnp.sum(lv[2:3] * lv[3:4], axis=-1, keepdims=True)) + lam_init)
    hi_rows = lax.broadcasted_iota(jnp.int32, (2 * ATT_TQ, 1), 0) >= ATT_TQ
    coef = jnp.where(hi_rows, -lam, 1.0)
    g = g_ref[...] * (1.0 - lam_init)

    def scores(i, slot):
        qq = _stack_halves(q_ref[0, pl.ds(_row0(i, ATT_TQ), ATT_TQ), :])
        _scores(qq, [k_ref[0]], [None], s_scr.at[slot], mx_scr.at[slot])

    def finish(i, slot):
        acc, l = _softmax_pv(s_scr.at[slot], mx_scr.at[slot], [v_ref[0]])
        r = acc * (coef / l)
        o = r[:ATT_TQ] + r[ATT_TQ:]
        o_ref[0, pl.ds(_row0(i, ATT_TQ), ATT_TQ), :] = (o * _rms_rows(o) * g).astype(BF16)

    _skewed_loop(SEQ // ATT_TQ, scores, finish)


def _diff(lv, qd, kd, vd, g, lam_init):
    bsz = qd.shape[0]
    return pl.pallas_call(
        functools.partial(_diff_kernel, lam_init=lam_init),
        grid=(bsz, DIFF_HEADS),
        in_specs=[_const_spec(lv),
                  pl.BlockSpec((1, SEQ, LANES), lambda b, h: (b, 0, h)),
                  pl.BlockSpec((1, N_TOK, LANES), lambda b, h: (b, 0, h)),
                  pl.BlockSpec((1, N_TOK, LANES), lambda b, h: (b, 0, h)),
                  _const_spec(g)],
        out_specs=pl.BlockSpec((1, SEQ, LANES), lambda b, h: (b, 0, h)),
        out_shape=jax.ShapeDtypeStruct((bsz, SEQ, DIFF_HEADS * 2 * DIFF_HEAD_DIM), BF16),
        scratch_shapes=[pltpu.VMEM((2, 2 * ATT_TQ, N_TOK), F32), pltpu.VMEM((2, 2 * ATT_TQ, LANES), F32)],
        compiler_params=_params(2),
        name="diff_attention",
    )(lv, qd, kd, vd, g)


def _ffn_body(x, o1_ref, o2_ref, mod_ref, g2_ref, wo1_ref, wo2_ref, wg_ref, wu_ref, wd_ref, y_ref):
    mod = mod_ref[0, 0]
    attn = _dot(o1_ref[0], wo1_ref[...]) + _dot(o2_ref[0], wo2_ref[...])
    x1 = x + mod[2:3] * attn
    h = (x1 * _rms_rows(x1) * g2_ref[...] * (1.0 + mod[4:5]) + mod[3:4]).astype(BF16)
    gate = _dot(h, wg_ref[...])
    up = _dot(h, wu_ref[...])
    a = (gate * (1.0 / (1.0 + jnp.exp(-gate))) * up).astype(BF16)
    y_ref[0] = x1 + mod[5:6] * _dot(a, wd_ref[...])


def _ffn0_kernel(x_ref, c_ref, *rest):
    _ffn_body(_read_tokens(x_ref, c_ref), *rest)


def _ffn1_kernel(x_ref, *rest):
    _ffn_body(x_ref[0], *rest)


def _ffn(xs, o1, o2, mod, p, n_tiles):
    bsz = xs[0].shape[0]
    consts = [p['g2'], p['wo1'], p['wo2'], p['wg'], p['wu'], p['wd']]
    x_specs = [_lat_spec(), _ctx_spec()] if len(xs) == 2 else [_tok_spec(D_MODEL)]
    return pl.pallas_call(
        _ffn0_kernel if len(xs) == 2 else _ffn1_kernel,
        grid=(n_tiles, bsz),
        in_specs=(x_specs + [_tok_spec(512), _tok_spec(512), _mod_spec()] + [_const_spec(a) for a in consts]),
        out_specs=_tok_spec(D_MODEL),
        out_shape=jax.ShapeDtypeStruct((bsz, n_tiles * TOK_TILE, D_MODEL), F32),
        compiler_params=_params(2),
        name="outproj_ffn",
    )(*xs, o1, o2, mod, *consts)


def _partner_cols(w, dr):
    q = dr // 4
    shp = w.shape
    return jnp.flip(w.reshape(shp[:-1] + (shp[-1] // dr, 2, 2, q)), axis=-2).reshape(shp)


def _rope_tables(dr):
    h = dr // 2
    q = dr // 4
    freqs = ROPE_THETA ** (-jnp.arange(0, h, 2, dtype=F32) / h)
    t = jnp.arange(SEQ)
    pos = jnp.stack([t // GRID_W, t % GRID_W], axis=1).astype(F32)
    ang = (pos[:, :, None, None] * freqs[None, None, None, :])
    ang = jnp.broadcast_to(ang, (SEQ, 2, 2, q)).reshape(SEQ, dr)
    sign = np.tile(np.repeat(np.array([-1.0, 1.0], np.float32), q), 2)
    cos = jnp.concatenate([jnp.cos(ang), jnp.ones((CTX_LEN, dr), F32)], axis=0)
    sin = jnp.concatenate([jnp.sin(ang) * sign, jnp.zeros((CTX_LEN, dr), F32)], axis=0)
    return cos, sin


def _seg_onehot(width, seg):
    s = np.zeros((width, LANES), np.float32)
    s[np.arange(width), np.arange(width) // seg] = 1.0
    return s


def _layer0_params(w_in, qa_g, w_uq, kva_g, w_ukv, qn_g, kn_g, na_qn_g, na_kn_g, norm1_g):
    d = w_in.shape[0]
    cos, sin = _rope_tables(MLA_ROPE)
    o_kr = MLA_Q_RANK + MLA_KV_RANK
    w0 = jnp.concatenate([w_in[:, :o_kr + MLA_ROPE], _partner_cols(w_in[:, o_kr:o_kr + MLA_ROPE], MLA_ROPE),
                          jnp.zeros((d, 64), F32), w_in[:, o_kr + MLA_ROPE:]], axis=1).astype(BF16)
    r = w_uq.shape[0]
    w3 = w_uq.reshape(r, MLA_HEADS, MLA_QK)
    main = jnp.pad(w3, ((0, 0), (0, 0), (0, LANES - MLA_QK)))
    rot = jnp.pad(_partner_cols(w3[:, :, MLA_NOPE:], MLA_ROPE), ((0, 0), (0, 0), (MLA_NOPE, LANES - MLA_QK)))
    wuq = jnp.concatenate([main.reshape(r, -1), rot.reshape(r, -1)], axis=1).astype(BF16)

    def tables(g, c):
        z32 = jnp.zeros((N_TOK, LANES - MLA_QK), F32)
        a = jnp.concatenate([jnp.broadcast_to(g[None, :MLA_NOPE], (N_TOK, MLA_NOPE)), g[None, MLA_NOPE:] * cos, z32], axis=1)
        b = jnp.concatenate([jnp.zeros((N_TOK, MLA_NOPE), F32), _partner_cols(g[None, MLA_NOPE:], MLA_ROPE) * sin, z32], axis=1)
        return a * c, b * c

    aq, bq = tables(qn_g, MLA_QK ** -0.5 * LOG2E)
    ak, bk = tables(kn_g, 1.0)
    rk = w_ukv.shape[0]
    k3 = w_ukv.reshape(rk, MLA_HEADS, MLA_NOPE + MLA_V)
    wukv = jnp.concatenate([jnp.pad(k3[:, :, :MLA_NOPE], ((0, 0), (0, 0), (0, LANES - MLA_NOPE))).reshape(rk, -1),
                            k3[:, :, MLA_NOPE:].reshape(rk, -1)], axis=1).astype(BF16)
    e = np.zeros((LANES, 2048), np.float32)
    for hd in range(MLA_HEADS):
        for i in range(MLA_ROPE):
            e[i, hd * LANES + MLA_NOPE + i] = 1.0
            e[MLA_ROPE + i, 1024 + hd * LANES + MLA_NOPE + i] = 1.0
    sa = _seg_onehot(1024, LANES)
    sb = _seg_onehot(NA_DIM, NA_HEAD_DIM)
    return dict(
        g1=norm1_g.reshape(1, -1), w0=w0, qag=qa_g.reshape(1, -1), wuq=wuq, kvag=kva_g.reshape(1, -1),
        wukv=wukv, e=jnp.asarray(e, BF16), sa=jnp.asarray(sa, BF16), sat=jnp.asarray(sa.T, BF16),
        sb=jnp.asarray(sb, BF16), sbt=jnp.asarray(sb.T, BF16), aq=aq, bq=bq, ak=ak, bk=bk,
        gqb=(jnp.tile(na_qn_g, NA_HEADS) * (NA_HEAD_DIM ** -0.5 * LOG2E)).reshape(1, -1),
        gkb=jnp.tile(na_kn_g, NA_HEADS).reshape(1, -1))


def _pair_heads(w, axis):
    g = GQA_HEADS // GQA_KV_HEADS
    shp = w.shape
    w4 = w.reshape(shp[:axis] + (GQA_KV_HEADS, g, GQA_HEAD_DIM) + shp[axis + 1:])
    return jnp.swapaxes(w4, axis, axis + 1).reshape(shp)


def _layer1_params(w_in, gqa_qn_g, gqa_kn_g, diff_qn_g, diff_kn_g, norm1_g):
    cos, sin = _rope_tables(64)
    main = jnp.concatenate([_pair_heads(w_in[:, :512], 1), w_in[:, 512:640], w_in[:, 768:1792]], axis=1)
    w1 = jnp.concatenate([main, w_in[:, 640:768], w_in[:, 1792:], _partner_cols(main, 64)], axis=1).astype(BF16)
    a_parts, b_parts = [], []
    for g, c in ((gqa_qn_g, GQA_HEAD_DIM ** -0.5 * LOG2E), (gqa_kn_g, 1.0),
                 (diff_qn_g, DIFF_HEAD_DIM ** -0.5 * LOG2E), (diff_kn_g, 1.0)):
        a_parts.append(jnp.tile(g[None, :] * cos * c, (1, 2)))
        b_parts.append(jnp.tile(_partner_cols(g[None, :], 64) * sin * c, (1, 2)))
    s = _seg_onehot(N1_NORM, 64)
    return dict(g1=norm1_g.reshape(1, -1), w1=w1, s=jnp.asarray(s, BF16), st=jnp.asarray(s.T, BF16),
                a=jnp.concatenate(a_parts, axis=1), b=jnp.concatenate(b_parts, axis=1))


def _na_table(rpb):
    us = np.array([0, 1, 2, NA_STEPS - 2, NA_STEPS - 1])
    n_var = len(us)
    qi = np.arange(NA_TQ)
    kk = np.arange(NA_WIN)
    r = NA_Q_ROWS * us[:, None, None] + (qi // GRID_W)[None, :, None]
    c = (qi % GRID_W)[None, :, None]
    w0 = np.minimum(np.clip(NA_Q_ROWS * us - NA_ROWS_MAX // 2, 0, GRID_ROWS - NA_ROWS_MAX), GRID_ROWS - NA_WIN_ROWS)
    key_r = w0[:, None, None] + (kk // GRID_W)[None, None, :]
    key_c = (kk % GRID_W)[None, None, :]
    r0 = np.clip(r - NA_ROWS_MAX // 2, 0, GRID_ROWS - NA_ROWS_MAX)
    c0 = np.clip(c - NA_COLS // 2, 0, GRID_W - NA_COLS)
    valid = (key_r >= r0) & (key_r < r0 + NA_ROWS_MAX) & (key_c >= c0) & (key_c < c0 + NA_COLS)
    valid = np.tile(valid, (1, 2, 1))[:, None]
    off_r = np.clip(w0[:, None, None] + np.arange(NA_WIN_ROWS)[None, None, :]
                    - (NA_Q_ROWS * us[:, None, None] + np.arange(NA_Q_ROWS)[None, :, None])
                    + (NA_ROWS_MAX - 1), 0, 2 * NA_ROWS_MAX - 2)
    rows = jnp.take(rpb, jnp.asarray(off_r.reshape(-1)), axis=1)
    pad = GRID_W - NA_COLS
    w = jnp.pad(rows, ((0, 0), (0, 0), (pad, 2 * GRID_W - pad - (2 * NA_COLS - 1))))
    toep = jnp.tile(w, (1, 1, GRID_W))[:, :, :GRID_W * (2 * GRID_W - 1)]
    toep = toep.reshape(NA_HEADS // 2, 2, n_var, NA_Q_ROWS, NA_WIN_ROWS, GRID_W, 2 * GRID_W - 1)[..., GRID_W - 1:]
    bias = jnp.transpose(toep, (2, 0, 1, 3, 5, 4, 6)).reshape(n_var, NA_HEADS // 2, 2 * NA_TQ, NA_WIN)
    return jnp.where(jnp.asarray(valid), bias * LOG2E, NEG_BIG)


def _ffn_params(norm2_g, w_out, w_gate, w_up, w_down, pair_gqa=False):
    wo1 = _pair_heads(w_out[:512], 0) if pair_gqa else w_out[:512]
    return dict(g2=norm2_g.reshape(1, -1), wo1=wo1.astype(BF16), wo2=w_out[512:].astype(BF16),
                wg=w_gate.astype(BF16), wu=w_up.astype(BF16), wd=w_down.astype(BF16))


def _mod_rows(mod_all, bsz):
    lat = mod_all[:bsz].reshape(bsz, 1, 6, D_MODEL)
    cx = jnp.broadcast_to(mod_all[bsz].reshape(1, 1, 6, D_MODEL), (bsz, 1, 6, D_MODEL))
    return jnp.concatenate([lat, cx], axis=1)


def kernel(x, c, ctx, c_ctx, l0_ada_w, l0_ada_b, l0_norm1_g, l0_norm2_g, l0_w_in, l0_mla_qa_g, l0_mla_w_uq, l0_mla_kva_g, l0_mla_w_ukv, l0_mla_qn_g, l0_mla_kn_g, l0_na_qn_g, l0_na_kn_g, l0_na_rpb, l0_w_out, l0_ffn_w_gate, l0_ffn_w_up, l0_ffn_w_down, l1_ada_w, l1_ada_b, l1_norm1_g, l1_norm2_g, l1_w_in, l1_gqa_qn_g, l1_gqa_kn_g, l1_gqa_sink, l1_diff_qn_g, l1_diff_kn_g, l1_diff_lq1, l1_diff_lk1, l1_diff_lq2, l1_diff_lk2, l1_diff_subln_g, l1_w_out, l1_ffn_w_gate, l1_ffn_w_up, l1_ffn_w_down):
    bsz = x.shape[0]
    assert x.shape[1:] == (SEQ, D_MODEL) and ctx.shape[1:] == (CTX_LEN, D_MODEL)
    rows = -(-(bsz + 1) // 8) * 8
    cond = jnp.concatenate([c, c_ctx[None, :], jnp.zeros((rows - bsz - 1, D_MODEL), F32)], axis=0)
    mod0 = _mod_rows(_ada(cond, l0_ada_w, l0_ada_b), bsz)
    mod1 = _mod_rows(_ada(cond, l1_ada_w, l1_ada_b), bsz)

    p0 = _layer0_params(l0_w_in, l0_mla_qa_g, l0_mla_w_uq, l0_mla_kva_g, l0_mla_w_ukv, l0_mla_qn_g,
                        l0_mla_kn_g, l0_na_qn_g, l0_na_kn_g, l0_norm1_g)
    qa, ka, va, qb, kb, vb = _proj0(x, ctx, mod0, p0)
    o_a = _mla(qa, ka, va)
    o_b = _na(qb, kb, vb, _na_table(l0_na_rpb))
    f0 = _ffn_params(l0_norm2_g, l0_w_out, l0_ffn_w_gate, l0_ffn_w_up, l0_ffn_w_down)
    xa = _ffn((x, ctx), o_a, o_b, mod0, f0, N_TILES)

    p1 = _layer1_params(l1_w_in, l1_gqa_qn_g, l1_gqa_kn_g, l1_diff_qn_g, l1_diff_kn_g, l1_norm1_g)
    qc, kc, vc, qd, kd, vd = _proj1(xa, mod1, p1)
    o_c = _gqa(l1_gqa_sink, qc, kc, vc)
    lv = jnp.zeros((8, LANES), F32).at[:4, :DIFF_HEAD_DIM].set(
        jnp.stack([l1_diff_lq1, l1_diff_lk1, l1_diff_lq2, l1_diff_lk2]))
    lam_init = 0.8 - 0.6 * math.exp(-0.3 * 1)
    o_d = _diff(lv, qd, kd, vd, l1_diff_subln_g.reshape(1, -1), lam_init)
    f1 = _ffn_params(l1_norm2_g, l1_w_out, l1_ffn_w_gate, l1_ffn_w_up, l1_ffn_w_down, pair_gqa=True)
    return _ffn((xa,), o_c, o_d, mod1, f1, N_LAT_TILES)
```

```python
import functools
import math

import numpy as np
import jax
import jax.numpy as jnp
from jax import lax
from jax.experimental import pallas as pl
from jax.experimental.pallas import tpu as pltpu

F32 = jnp.float32
BF16 = jnp.bfloat16

D_MODEL = 1024
SEQ = 2048
GRID_W = 64
CTX_LEN = 256
N_TOK = SEQ + CTX_LEN
ROPE_THETA = 10000.0
EPS = 1e-6

MLA_HEADS = 8
MLA_Q_RANK = 256
MLA_KV_RANK = 128
MLA_NOPE = 64
MLA_ROPE = 32
MLA_V = 64
MLA_QK = MLA_NOPE + MLA_ROPE

NA_HEADS = 8
NA_HEAD_DIM = 64
NA_ROWS_MAX = 8
NA_COLS = 16
NA_DIM = NA_HEADS * NA_HEAD_DIM

GQA_HEADS = 8
GQA_KV_HEADS = 2
GQA_HEAD_DIM = 64
GQA_WINDOW = 128

DIFF_HEADS = 4
DIFF_HEAD_DIM = 64

FFN_HIDDEN = 2816

LANES = 128
TOK_TILE = 256
N_TILES = N_TOK // TOK_TILE
N_LAT_TILES = SEQ // TOK_TILE
VMEM_LIMIT = 56 * 1024 * 1024
LOG2E = 1.4426950408889634
NEG_BIG = -1e30

ATT_TQ = 256
MLA_TQ = 256
NA_Q_ROWS = 2
NA_TQ = NA_Q_ROWS * GRID_W
NA_WIN_ROWS = 10
NA_WIN = NA_WIN_ROWS * GRID_W
NA_STEPS = SEQ // NA_TQ
GRID_ROWS = SEQ // GRID_W
GQA_TQ = 128
GQA_BAND = 3 * GQA_TQ


def _dot(a, b):
    return jnp.dot(a, b, preferred_element_type=F32)


def _dot_nt(a, b):
    return lax.dot_general(a, b, (((1,), (1,)), ((), ())), preferred_element_type=F32)


def _rms_rows(x):
    return lax.rsqrt(jnp.mean(x * x, axis=-1, keepdims=True) + EPS)


def _const_spec(a):
    nd = a.ndim
    return pl.BlockSpec(a.shape, lambda *_: (0,) * nd)


def _params(n_grid):
    return pltpu.CompilerParams(dimension_semantics=("arbitrary",) * n_grid,
                                vmem_limit_bytes=VMEM_LIMIT)


def _row0(i, size):
    return i * size if isinstance(i, int) else pl.multiple_of(i * size, size)


def _fold_lanes(xs, op):
    r = None
    for x in xs:
        for lo in range(0, x.shape[1], LANES):
            blk = x[:, lo:lo + LANES]
            r = blk if r is None else op(r, blk)
    return r


def _scores(q, keys, biases, s_ref, mx_ref):
    parts = []
    lo = 0
    for k, bias in zip(keys, biases):
        sc = _dot_nt(q, k)
        if bias is not None:
            sc = sc + bias
        s_ref[:, lo:lo + k.shape[0]] = sc
        lo += k.shape[0]
        parts.append(sc)
    mx_ref[...] = _fold_lanes(parts, jnp.maximum)


def _softmax_pv(s_ref, mx_ref, vals, extra=None):
    m = jnp.max(mx_ref[...], axis=-1, keepdims=True)
    if extra is not None:
        m = jnp.maximum(m, extra)
    ps = []
    lo = 0
    for v in vals:
        ps.append(jnp.exp2(s_ref[:, lo:lo + v.shape[0]] - m))
        lo += v.shape[0]
    l = jnp.sum(_fold_lanes(ps, jnp.add), axis=-1, keepdims=True)
    if extra is not None:
        l = l + jnp.exp2(extra - m)
    acc = _dot(ps[0].astype(BF16), vals[0])
    for p, v in zip(ps[1:], vals[1:]):
        acc = acc + _dot(p.astype(BF16), v)
    return acc, l


def _skewed_loop(n, scores, finish):
    scores(0, 0)

    def body(j, carry):
        i = 2 * j
        scores(i + 1, 1)
        finish(i, 0)
        scores(jnp.minimum(i + 2, n - 1), 0)
        finish(i + 1, 1)
        return carry

    lax.fori_loop(0, n // 2, body, 0)


def _att_scratch(lead, rows, n_keys):
    lead = (2,) + tuple(lead)
    return [pltpu.VMEM(lead + (rows, n_keys), F32), pltpu.VMEM(lead + (rows, LANES), F32)]


def _stack_halves(q):
    lane = lax.broadcasted_iota(jnp.int32, q.shape, 1)
    zero = jnp.zeros_like(q)
    return jnp.concatenate([jnp.where(lane < 64, q, zero), jnp.where(lane >= 64, q, zero)], axis=0)


def _merge_halves(lo, hi):
    lane = lax.broadcasted_iota(jnp.int32, lo.shape, 1)
    return jnp.where(lane < 64, lo, hi)


def _ada_kernel(c_ref, w_ref, b_ref, o_ref):
    c = c_ref[...]
    a = (c * (1.0 / (1.0 + jnp.exp(-c)))).astype(BF16)
    o_ref[...] = _dot(a, w_ref[...].astype(BF16)) + b_ref[...]


def _ada(cond, w, b):
    n = w.shape[1]
    tn = 1024
    return pl.pallas_call(
        _ada_kernel,
        grid=(n // tn,),
        in_specs=[pl.BlockSpec(cond.shape, lambda j: (0, 0)),
                  pl.BlockSpec((w.shape[0], tn), lambda j: (0, j)),
                  pl.BlockSpec((1, tn), lambda j: (0, j))],
        out_specs=pl.BlockSpec((cond.shape[0], tn), lambda j: (0, j)),
        out_shape=jax.ShapeDtypeStruct((cond.shape[0], n), F32),
        compiler_params=_params(1),
        name="ada_modulation",
    )(cond, w, b.reshape(1, n))


def _tok_spec(width):
    return pl.BlockSpec((1, TOK_TILE, width), lambda t, b: (b, t, 0))


def _lat_spec():
    return pl.BlockSpec((1, TOK_TILE, D_MODEL), lambda t, b: (b, jnp.minimum(t, N_LAT_TILES - 1), 0))


def _ctx_spec():
    return pl.BlockSpec((1, TOK_TILE, D_MODEL), lambda t, b: (b, 0, 0))


def _mod_spec():
    return pl.BlockSpec((1, 1, 6, D_MODEL), lambda t, b: (b, t // N_LAT_TILES, 0, 0))


def _table_spec(width):
    return pl.BlockSpec((TOK_TILE, width), lambda t, b: (t, 0))


def _read_tokens(x_ref, c_ref):
    return jnp.where(pl.program_id(0) < N_LAT_TILES, x_ref[0], c_ref[0])


W0_CQ, W0_CKV, W0_KR, W0_KRP, W0_QN, W0_KN, W0_VN, W0_END = 0, 256, 384, 512, 640, 1152, 1664, 2176


def _head_inv(z, n):
    return lax.rsqrt(jnp.sum(z * z, axis=-1, keepdims=True) * (1.0 / n) + EPS)


def _half_inv(z):
    z2 = z * z
    lane = lax.broadcasted_iota(jnp.int32, z.shape, 1)
    lo = jnp.sum(jnp.where(lane < 64, z2, 0.0), axis=-1, keepdims=True)
    hi = jnp.sum(z2, axis=-1, keepdims=True) - lo
    return jnp.where(lane < 64, lax.rsqrt(lo * (1.0 / 64.0) + EPS), lax.rsqrt(hi * (1.0 / 64.0) + EPS))


def _proj0_kernel(x_ref, c_ref, mod_ref, g1_ref, w0_ref, qag_ref, wuq_ref, kvag_ref, wukv_ref,
                  aq_ref, bq_ref, ak_ref, bk_ref, gqb_ref, gkb_ref,
                  qa_ref, ka_ref, va_ref, qb_ref, kb_ref, vb_ref):
    x = _read_tokens(x_ref, c_ref)
    mod = mod_ref[0, 0]
    h = x * _rms_rows(x) * g1_ref[...] * (1.0 + mod[1:2]) + mod[0:1]
    z = _dot(h.astype(BF16), w0_ref[...])

    cq = z[:, W0_CQ:W0_CKV]
    cqn = (cq * _rms_rows(cq) * qag_ref[...]).astype(BF16)
    zq = _dot(cqn, wuq_ref[...])
    aq, bq = aq_ref[...], bq_ref[...]
    for hd in range(MLA_HEADS):
        qm = zq[:, hd * LANES:(hd + 1) * LANES]
        qr = zq[:, 1024 + hd * LANES:1024 + (hd + 1) * LANES]
        qa_ref[0, :, hd * LANES:(hd + 1) * LANES] = ((qm * aq + qr * bq) * _head_inv(qm, MLA_QK)).astype(BF16)

    ckv = z[:, W0_CKV:W0_KR]
    ckvn = (ckv * _rms_rows(ckv) * kvag_ref[...]).astype(BF16)
    zkv = _dot(ckvn, wukv_ref[...])
    kr = z[:, W0_KR:W0_KRP]
    kr_rot = z[:, W0_KRP:W0_QN] * bk_ref[...]
    ak = ak_ref[...]
    for hd in range(MLA_HEADS):
        km = zkv[:, hd * LANES:(hd + 1) * LANES] + kr
        ka_ref[0, :, hd * LANES:(hd + 1) * LANES] = ((km * ak + kr_rot) * _head_inv(km, MLA_QK)).astype(BF16)
    va_ref[0] = zkv[:, 1024:].astype(BF16)

    for j in range(NA_DIM // LANES):
        cols = slice(j * LANES, (j + 1) * LANES)
        qn = z[:, W0_QN + j * LANES:W0_QN + (j + 1) * LANES]
        kn = z[:, W0_KN + j * LANES:W0_KN + (j + 1) * LANES]
        qb_ref[0, :, cols] = (qn * gqb_ref[...] * _half_inv(qn)).astype(BF16)
        kb_ref[0, :, cols] = (kn * gkb_ref[...] * _half_inv(kn)).astype(BF16)
    vb_ref[0] = z[:, W0_VN:W0_END].astype(BF16)


def _proj0(x, ctx, mod, p):
    bsz = x.shape[0]
    consts1 = [p['g1'], p['w0'], p['qag'], p['wuq'], p['kvag'], p['wukv']]
    tables = [p['aq'], p['bq'], p['ak'], p['bk']]
    consts2 = [p['gqb'], p['gkb']]
    widths = [1024, 1024, 512, 512, 512, 512]
    return pl.pallas_call(
        _proj0_kernel,
        grid=(N_TILES, bsz),
        in_specs=([_lat_spec(), _ctx_spec(), _mod_spec()] + [_const_spec(a) for a in consts1]
                  + [_table_spec(LANES) for _ in tables] + [_const_spec(a) for a in consts2]),
        out_specs=[_tok_spec(w) for w in widths],
        out_shape=[jax.ShapeDtypeStruct((bsz, N_TOK, w), BF16) for w in widths],
        compiler_params=_params(2),
        name="proj_layer0",
    )(x, ctx, mod, *consts1, *tables, *consts2)


N1_SEG = (4, 1, 4, 4)
N1_NORM = 1664
N1_MAIN = 2304


def _proj1_kernel(x_ref, mod_ref, g1_ref, w1_ref, a_ref, b_ref,
                  qc_ref, kc_ref, vc_ref, qd_ref, kd_ref, vd_ref):
    x = x_ref[0]
    mod = mod_ref[0, 0]
    h = x * _rms_rows(x) * g1_ref[...] * (1.0 + mod[1:2]) + mod[0:1]
    z = _dot(h.astype(BF16), w1_ref[...])
    outs = (qc_ref, kc_ref, qd_ref, kd_ref)
    blk = 0
    for seg, n_blk in enumerate(N1_SEG):
        a = a_ref[:, seg * LANES:(seg + 1) * LANES]
        b = b_ref[:, seg * LANES:(seg + 1) * LANES]
        for j in range(n_blk):
            zm = z[:, blk * LANES:(blk + 1) * LANES]
            rot = z[:, N1_MAIN + blk * LANES:N1_MAIN + (blk + 1) * LANES]
            outs[seg][0, :, j * LANES:(j + 1) * LANES] = ((zm * a + rot * b) * _half_inv(zm)).astype(BF16)
            blk += 1
    vc_ref[0] = z[:, 1664:1792].astype(BF16)
    vd_ref[0] = z[:, 1792:2304].astype(BF16)


def _proj1(xa, mod, p):
    bsz = xa.shape[0]
    consts = [p['g1'], p['w1']]
    widths = [512, 128, 128, 512, 512, 512]
    return pl.pallas_call(
        _proj1_kernel,
        grid=(N_TILES, bsz),
        in_specs=([_tok_spec(D_MODEL), _mod_spec()] + [_const_spec(a) for a in consts]
                  + [_table_spec(4 * LANES), _table_spec(4 * LANES)]),
        out_specs=[_tok_spec(w) for w in widths],
        out_shape=[jax.ShapeDtypeStruct((bsz, N_TOK, w), BF16) for w in widths],
        compiler_params=_params(2),
        name="proj_layer1",
    )(xa, mod, *consts, p['a'], p['b'])


def _mla_kernel(q_ref, k_ref, v_ref, o_ref, s_scr, mx_scr):
    def scores(r0, nq, slot, key_lo):
        for hh in range(2):
            cols = slice(hh * LANES, (hh + 1) * LANES)
            _scores(q_ref[0, pl.ds(r0, nq), cols], [k_ref[0, key_lo:, cols]], [None],
                    s_scr.at[slot, hh, 0:nq], mx_scr.at[slot, hh, 0:nq])

    def finish(r0, nq, slot, key_lo):
        outs = []
        for hh in range(2):
            acc, l = _softmax_pv(s_scr.at[slot, hh, 0:nq], mx_scr.at[slot, hh, 0:nq], [v_ref[0, key_lo:, :]])
            outs.append(acc / l)
        o_ref[0, pl.ds(r0, nq), :] = _merge_halves(outs[0], outs[1]).astype(BF16)

    _skewed_loop(SEQ // MLA_TQ,
                 lambda i, slot: scores(_row0(i, MLA_TQ), MLA_TQ, slot, 0),
                 lambda i, slot: finish(_row0(i, MLA_TQ), MLA_TQ, slot, 0))
    scores(SEQ, CTX_LEN, 0, SEQ)
    finish(SEQ, CTX_LEN, 0, SEQ)


def _mla(qa, ka, va):
    bsz = qa.shape[0]
    assert CTX_LEN <= MLA_TQ
    return pl.pallas_call(
        _mla_kernel,
        grid=(bsz, MLA_HEADS // 2),
        in_specs=[pl.BlockSpec((1, N_TOK, 2 * LANES), lambda b, j: (b, 0, j)),
                  pl.BlockSpec((1, N_TOK, 2 * LANES), lambda b, j: (b, 0, j)),
                  pl.BlockSpec((1, N_TOK, LANES), lambda b, j: (b, 0, j))],
        out_specs=pl.BlockSpec((1, N_TOK, LANES), lambda b, j: (b, 0, j)),
        out_shape=jax.ShapeDtypeStruct((bsz, N_TOK, MLA_HEADS * MLA_V), BF16),
        scratch_shapes=_att_scratch((2,), MLA_TQ, N_TOK),
        compiler_params=_params(2),
        name="mla_attention",
    )(qa, ka, va)


def _na_kernel(q_ref, k_ref, v_ref, t_ref, o_ref, s_scr, mx_scr):
    def window_start(u):
        w0 = jnp.minimum(jnp.clip(NA_Q_ROWS * u - NA_ROWS_MAX // 2, 0, GRID_ROWS - NA_ROWS_MAX),
                         GRID_ROWS - NA_WIN_ROWS)
        return pl.multiple_of(w0 * GRID_W, GRID_W)

    def scores(u, slot):
        ws = window_start(u)
        r0 = _row0(u, NA_TQ)
        var = jnp.minimum(u, 2) + jnp.maximum(u - (NA_STEPS - 3), 0)
        for j in range(NA_HEADS // 2):
            cols = slice(j * LANES, (j + 1) * LANES)
            qq = _stack_halves(q_ref[0, pl.ds(r0, NA_TQ), cols])
            _scores(qq, [k_ref[0, pl.ds(ws, NA_WIN), cols], k_ref[0, SEQ:, cols]], [t_ref[var, j], None],
                    s_scr.at[slot, j], mx_scr.at[slot, j])

    def finish(u, slot):
        ws = window_start(u)
        r0 = _row0(u, NA_TQ)
        for j in range(NA_HEADS // 2):
            cols = slice(j * LANES, (j + 1) * LANES)
            acc, l = _softmax_pv(s_scr.at[slot, j], mx_scr.at[slot, j],
                                 [v_ref[0, pl.ds(ws, NA_WIN), cols], v_ref[0, SEQ:, cols]])
            r = acc / l
            o_ref[0, pl.ds(r0, NA_TQ), cols] = _merge_halves(r[:NA_TQ], r[NA_TQ:]).astype(BF16)

    _skewed_loop(NA_STEPS, scores, finish)
    for r0 in range(SEQ, N_TOK, NA_TQ):
        for j in range(NA_HEADS // 2):
            cols = slice(j * LANES, (j + 1) * LANES)
            qq = _stack_halves(q_ref[0, r0:r0 + NA_TQ, cols])
            _scores(qq, [k_ref[0, SEQ:, cols]], [None], s_scr.at[0, j], mx_scr.at[0, j])
            acc, l = _softmax_pv(s_scr.at[0, j], mx_scr.at[0, j], [v_ref[0, SEQ:, cols]])
            r = acc / l
            o_ref[0, r0:r0 + NA_TQ, cols] = _merge_halves(r[:NA_TQ], r[NA_TQ:]).astype(BF16)


def _na(qb, kb, vb, table):
    bsz = qb.shape[0]
    tok = pl.BlockSpec((1, N_TOK, NA_DIM), lambda b: (b, 0, 0))
    n_pairs = NA_HEADS // 2
    return pl.pallas_call(
        _na_kernel,
        grid=(bsz,),
        in_specs=[tok, tok, tok, _const_spec(table)],
        out_specs=tok,
        out_shape=jax.ShapeDtypeStruct((bsz, N_TOK, NA_DIM), BF16),
        scratch_shapes=_att_scratch((n_pairs,), 2 * NA_TQ, NA_WIN + CTX_LEN),
        compiler_params=_params(1),
        name="neighbourhood_attention",
    )(qb, kb, vb, table)


def _gqa_kernel(sink_ref, q_ref, k_ref, v_ref, o_ref, s_scr, mx_scr):
    row = lax.broadcasted_iota(jnp.int32, (2 * GQA_TQ, GQA_BAND), 0)
    col = lax.broadcasted_iota(jnp.int32, (2 * GQA_TQ, GQA_BAND), 1)
    rel = jnp.where(row >= GQA_TQ, row - GQA_TQ, row) - col
    hi_rows = lax.broadcasted_iota(jnp.int32, (2 * GQA_TQ, 1), 0) >= GQA_TQ

    def band_start(r0):
        return pl.multiple_of(jnp.clip(r0 - GQA_TQ, 0, SEQ - GQA_BAND), GQA_TQ)

    def scores(n, slot):
        r0 = _row0(n, GQA_TQ)
        start = band_start(r0)
        mask = jnp.where(jnp.abs(rel + (r0 - start)) <= GQA_WINDOW, 0.0, NEG_BIG)
        keys = [k_ref[0, pl.ds(start, GQA_BAND), :], k_ref[0, SEQ:, :]]
        for j in range(GQA_HEADS // 2):
            qq = _stack_halves(q_ref[0, pl.ds(r0, GQA_TQ), j * LANES:(j + 1) * LANES])
            _scores(qq, keys, [mask, None], s_scr.at[slot, j], mx_scr.at[slot, j])

    def finish(n, slot):
        r0 = _row0(n, GQA_TQ)
        start = band_start(r0)
        vals = [v_ref[0, pl.ds(start, GQA_BAND), :], v_ref[0, SEQ:, :]]
        for j in range(GQA_HEADS // 2):
            sink = jnp.where(hi_rows, sink_ref[j + GQA_HEADS // 2], sink_ref[j]) * LOG2E
            acc, l = _softmax_pv(s_scr.at[slot, j], mx_scr.at[slot, j], vals, extra=sink)
            r = acc / l
            o_ref[0, pl.ds(r0, GQA_TQ), j * LANES:(j + 1) * LANES] = _merge_halves(r[:GQA_TQ], r[GQA_TQ:]).astype(BF16)

    _skewed_loop(SEQ // GQA_TQ, scores, finish)


def _gqa(sink, qc, kc, vc):
    bsz = qc.shape[0]
    return pl.pallas_call(
        _gqa_kernel,
        grid=(bsz,),
        in_specs=[pl.BlockSpec(memory_space=pltpu.SMEM),
                  pl.BlockSpec((1, SEQ, 512), lambda b: (b, 0, 0)),
                  pl.BlockSpec((1, N_TOK, LANES), lambda b: (b, 0, 0)),
                  pl.BlockSpec((1, N_TOK, LANES), lambda b: (b, 0, 0))],
        out_specs=pl.BlockSpec((1, SEQ, 512), lambda b: (b, 0, 0)),
        out_shape=jax.ShapeDtypeStruct((bsz, SEQ, 512), BF16),
        scratch_shapes=_att_scratch((GQA_HEADS // 2,), 2 * GQA_TQ, GQA_BAND + CTX_LEN),
        compiler_params=_params(1),
        name="windowed_gqa",
    )(sink, qc, kc, vc)


def _diff_kernel(lv_ref, q_ref, k_ref, v_ref, g_ref, o_ref, s_scr, mx_scr, *, lam_init):
    lv = lv_ref[...]
    lam = (jnp.exp(jnp.sum(lv[0:1] * lv[1:2], axis=-1, keepdims=True))
           - jnp.exp(jnp.sum(lv[2:3] * lv[3:4], axis=-1, keepdims=True)) + lam_init)
    hi_rows = lax.broadcasted_iota(jnp.int32, (2 * ATT_TQ, 1), 0) >= ATT_TQ
    coef = jnp.where(hi_rows, -lam, 1.0)
    g = g_ref[...] * (1.0 - lam_init)

    def scores(i, slot):
        qq = _stack_halves(q_ref[0, pl.ds(_row0(i, ATT_TQ), ATT_TQ), :])
        _scores(qq, [k_ref[0]], [None], s_scr.at[slot], mx_scr.at[slot])

    def finish(i, slot):
        acc, l = _softmax_pv(s_scr.at[slot], mx_scr.at[slot], [v_ref[0]])
        r = acc * (coef / l)
        o = r[:ATT_TQ] + r[ATT_TQ:]
        o_ref[0, pl.ds(_row0(i, ATT_TQ), ATT_TQ), :] = (o * _rms_rows(o) * g).astype(BF16)

    _skewed_loop(SEQ // ATT_TQ, scores, finish)


def _diff(lv, qd, kd, vd, g, lam_init):
    bsz = qd.shape[0]
    return pl.pallas_call(
        functools.partial(_diff_kernel, lam_init=lam_init),
        grid=(bsz, DIFF_HEADS),
        in_specs=[_const_spec(lv),
                  pl.BlockSpec((1, SEQ, LANES), lambda b, h: (b, 0, h)),
                  pl.BlockSpec((1, N_TOK, LANES), lambda b, h: (b, 0, h)),
                  pl.BlockSpec((1, N_TOK, LANES), lambda b, h: (b, 0, h)),
                  _const_spec(g)],
        out_specs=pl.BlockSpec((1, SEQ, LANES), lambda b, h: (b, 0, h)),
        out_shape=jax.ShapeDtypeStruct((bsz, SEQ, DIFF_HEADS * 2 * DIFF_HEAD_DIM), BF16),
        scratch_shapes=_att_scratch((), 2 * ATT_TQ, N_TOK),
        compiler_params=_params(2),
        name="diff_attention",
    )(lv, qd, kd, vd, g)


def _ffn_body(x, o1_ref, o2_ref, mod_ref, g2_ref, wo1_ref, wo2_ref, wg_ref, wu_ref, wd_ref, y_ref):
    mod = mod_ref[0, 0]
    attn = _dot(o1_ref[0], wo1_ref[...]) + _dot(o2_ref[0], wo2_ref[...])
    x1 = x + mod[2:3] * attn
    h = (x1 * _rms_rows(x1) * g2_ref[...] * (1.0 + mod[4:5]) + mod[3:4]).astype(BF16)
    gate = _dot(h, wg_ref[...])
    up = _dot(h, wu_ref[...])
    a = (gate * (1.0 / (1.0 + jnp.exp(-gate))) * up).astype(BF16)
    y_ref[0] = x1 + mod[5:6] * _dot(a, wd_ref[...])


def _ffn0_kernel(x_ref, c_ref, *rest):
    _ffn_body(_read_tokens(x_ref, c_ref), *rest)


def _ffn1_kernel(x_ref, *rest):
    _ffn_body(x_ref[0], *rest)


def _ffn(xs, o1, o2, mod, p, n_tiles):
    bsz = xs[0].shape[0]
    consts = [p['g2'], p['wo1'], p['wo2'], p['wg'], p['wu'], p['wd']]
    x_specs = [_lat_spec(), _ctx_spec()] if len(xs) == 2 else [_tok_spec(D_MODEL)]
    return pl.pallas_call(
        _ffn0_kernel if len(xs) == 2 else _ffn1_kernel,
        grid=(n_tiles, bsz),
        in_specs=(x_specs + [_tok_spec(512), _tok_spec(512), _mod_spec()] + [_const_spec(a) for a in consts]),
        out_specs=_tok_spec(D_MODEL),
        out_shape=jax.ShapeDtypeStruct((bsz, n_tiles * TOK_TILE, D_MODEL), F32),
        compiler_params=_params(2),
        name="outproj_ffn",
    )(*xs, o1, o2, mod, *consts)


def _partner_cols(w, dr):
    q = dr // 4
    shp = w.shape
    return jnp.flip(w.reshape(shp[:-1] + (shp[-1] // dr, 2, 2, q)), axis=-2).reshape(shp)


def _rope_tables(dr):
    h = dr // 2
    q = dr // 4
    freqs = ROPE_THETA ** (-jnp.arange(0, h, 2, dtype=F32) / h)
    t = jnp.arange(SEQ)
    pos = jnp.stack([t // GRID_W, t % GRID_W], axis=1).astype(F32)
    ang = (pos[:, :, None, None] * freqs[None, None, None, :])
    ang = jnp.broadcast_to(ang, (SEQ, 2, 2, q)).reshape(SEQ, dr)
    sign = np.tile(np.repeat(np.array([-1.0, 1.0], np.float32), q), 2)
    cos = jnp.concatenate([jnp.cos(ang), jnp.ones((CTX_LEN, dr), F32)], axis=0)
    sin = jnp.concatenate([jnp.sin(ang) * sign, jnp.zeros((CTX_LEN, dr), F32)], axis=0)
    return cos, sin


def _layer0_params(w_in, qa_g, w_uq, kva_g, w_ukv, qn_g, kn_g, na_qn_g, na_kn_g, norm1_g):
    cos, sin = _rope_tables(MLA_ROPE)
    o_kr = MLA_Q_RANK + MLA_KV_RANK
    kr_w = w_in[:, o_kr:o_kr + MLA_ROPE]
    lane_pad = ((0, 0), (MLA_NOPE, LANES - MLA_QK))
    w0 = jnp.concatenate([w_in[:, :o_kr], jnp.pad(kr_w, lane_pad), jnp.pad(_partner_cols(kr_w, MLA_ROPE), lane_pad),
                          w_in[:, o_kr + MLA_ROPE:]], axis=1).astype(BF16)
    r = w_uq.shape[0]
    w3 = w_uq.reshape(r, MLA_HEADS, MLA_QK)
    main = jnp.pad(w3, ((0, 0), (0, 0), (0, LANES - MLA_QK)))
    rot = jnp.pad(_partner_cols(w3[:, :, MLA_NOPE:], MLA_ROPE), ((0, 0), (0, 0), (MLA_NOPE, LANES - MLA_QK)))
    wuq = jnp.concatenate([main.reshape(r, -1), rot.reshape(r, -1)], axis=1).astype(BF16)

    def tables(g, c):
        z32 = jnp.zeros((N_TOK, LANES - MLA_QK), F32)
        a = jnp.concatenate([jnp.broadcast_to(g[None, :MLA_NOPE], (N_TOK, MLA_NOPE)), g[None, MLA_NOPE:] * cos, z32], axis=1)
        b = jnp.concatenate([jnp.zeros((N_TOK, MLA_NOPE), F32), _partner_cols(g[None, MLA_NOPE:], MLA_ROPE) * sin, z32], axis=1)
        return a * c, b * c

    aq, bq = tables(qn_g, MLA_QK ** -0.5 * LOG2E)
    ak, bk = tables(kn_g, 1.0)
    rk = w_ukv.shape[0]
    k3 = w_ukv.reshape(rk, MLA_HEADS, MLA_NOPE + MLA_V)
    wukv = jnp.concatenate([jnp.pad(k3[:, :, :MLA_NOPE], ((0, 0), (0, 0), (0, LANES - MLA_NOPE))).reshape(rk, -1),
                            k3[:, :, MLA_NOPE:].reshape(rk, -1)], axis=1).astype(BF16)
    return dict(
        g1=norm1_g.reshape(1, -1), w0=w0, qag=qa_g.reshape(1, -1), wuq=wuq, kvag=kva_g.reshape(1, -1),
        wukv=wukv, aq=aq, bq=bq, ak=ak, bk=bk,
        gqb=(jnp.tile(na_qn_g, 2) * (NA_HEAD_DIM ** -0.5 * LOG2E)).reshape(1, -1),
        gkb=jnp.tile(na_kn_g, 2).reshape(1, -1))


def _pair_heads(w, axis):
    g = GQA_HEADS // GQA_KV_HEADS
    shp = w.shape
    w4 = w.reshape(shp[:axis] + (GQA_KV_HEADS, g, GQA_HEAD_DIM) + shp[axis + 1:])
    return jnp.swapaxes(w4, axis, axis + 1).reshape(shp)


def _layer1_params(w_in, gqa_qn_g, gqa_kn_g, diff_qn_g, diff_kn_g, norm1_g):
    cos, sin = _rope_tables(64)
    main = jnp.concatenate([_pair_heads(w_in[:, :512], 1), w_in[:, 512:640], w_in[:, 768:1792]], axis=1)
    w1 = jnp.concatenate([main, w_in[:, 640:768], w_in[:, 1792:], _partner_cols(main, 64)], axis=1).astype(BF16)
    a_parts, b_parts = [], []
    for g, c in ((gqa_qn_g, GQA_HEAD_DIM ** -0.5 * LOG2E), (gqa_kn_g, 1.0),
                 (diff_qn_g, DIFF_HEAD_DIM ** -0.5 * LOG2E), (diff_kn_g, 1.0)):
        a_parts.append(jnp.tile(g[None, :] * cos * c, (1, 2)))
        b_parts.append(jnp.tile(_partner_cols(g[None, :], 64) * sin * c, (1, 2)))
    return dict(g1=norm1_g.reshape(1, -1), w1=w1,
                a=jnp.concatenate(a_parts, axis=1), b=jnp.concatenate(b_parts, axis=1))


def _na_table(rpb):
    us = np.array([0, 1, 2, NA_STEPS - 2, NA_STEPS - 1])
    n_var = len(us)
    qi = np.arange(NA_TQ)
    kk = np.arange(NA_WIN)
    r = NA_Q_ROWS * us[:, None, None] + (qi // GRID_W)[None, :, None]
    c = (qi % GRID_W)[None, :, None]
    w0 = np.minimum(np.clip(NA_Q_ROWS * us - NA_ROWS_MAX // 2, 0, GRID_ROWS - NA_ROWS_MAX), GRID_ROWS - NA_WIN_ROWS)
    key_r = w0[:, None, None] + (kk // GRID_W)[None, None, :]
    key_c = (kk % GRID_W)[None, None, :]
    r0 = np.clip(r - NA_ROWS_MAX // 2, 0, GRID_ROWS - NA_ROWS_MAX)
    c0 = np.clip(c - NA_COLS // 2, 0, GRID_W - NA_COLS)
    valid = (key_r >= r0) & (key_r < r0 + NA_ROWS_MAX) & (key_c >= c0) & (key_c < c0 + NA_COLS)
    valid = np.tile(valid, (1, 2, 1))[:, None]
    off_r = np.clip(w0[:, None, None] + np.arange(NA_WIN_ROWS)[None, None, :]
                    - (NA_Q_ROWS * us[:, None, None] + np.arange(NA_Q_ROWS)[None, :, None])
                    + (NA_ROWS_MAX - 1), 0, 2 * NA_ROWS_MAX - 2)
    n_rows, n_off = 2 * NA_ROWS_MAX - 1, 2 * NA_COLS - 1
    pick = (off_r.reshape(-1, 1) == np.arange(n_rows)[None, :]).astype(np.float32)
    cc = np.arange(GRID_W)
    spread = ((cc[None, None, :] - cc[None, :, None] + NA_COLS - 1) == np.arange(n_off)[:, None, None]).astype(np.float32)
    rows = jnp.einsum('nr,hrd->hnd', pick, rpb, precision=lax.Precision.HIGHEST)
    toep = jnp.einsum('hnd,dck->hnck', rows, spread, precision=lax.Precision.HIGHEST)
    toep = toep.reshape(NA_HEADS // 2, 2, n_var, NA_Q_ROWS, NA_WIN_ROWS, GRID_W, GRID_W)
    bias = jnp.transpose(toep, (2, 0, 1, 3, 5, 4, 6)).reshape(n_var, NA_HEADS // 2, 2 * NA_TQ, NA_WIN)
    return jnp.where(jnp.asarray(valid), bias * LOG2E, NEG_BIG)


def _ffn_params(norm2_g, w_out, w_gate, w_up, w_down, pair_gqa=False):
    wo1 = _pair_heads(w_out[:512], 0) if pair_gqa else w_out[:512]
    return dict(g2=norm2_g.reshape(1, -1), wo1=wo1.astype(BF16), wo2=w_out[512:].astype(BF16),
                wg=w_gate.astype(BF16), wu=w_up.astype(BF16), wd=w_down.astype(BF16))


def _mod_rows(mod_all, bsz):
    lat = mod_all[:bsz].reshape(bsz, 1, 6, D_MODEL)
    cx = jnp.broadcast_to(mod_all[bsz].reshape(1, 1, 6, D_MODEL), (bsz, 1, 6, D_MODEL))
    return jnp.concatenate([lat, cx], axis=1)


def kernel(x, c, ctx, c_ctx, l0_ada_w, l0_ada_b, l0_norm1_g, l0_norm2_g, l0_w_in, l0_mla_qa_g, l0_mla_w_uq, l0_mla_kva_g, l0_mla_w_ukv, l0_mla_qn_g, l0_mla_kn_g, l0_na_qn_g, l0_na_kn_g, l0_na_rpb, l0_w_out, l0_ffn_w_gate, l0_ffn_w_up, l0_ffn_w_down, l1_ada_w, l1_ada_b, l1_norm1_g, l1_norm2_g, l1_w_in, l1_gqa_qn_g, l1_gqa_kn_g, l1_gqa_sink, l1_diff_qn_g, l1_diff_kn_g, l1_diff_lq1, l1_diff_lk1, l1_diff_lq2, l1_diff_lk2, l1_diff_subln_g, l1_w_out, l1_ffn_w_gate, l1_ffn_w_up, l1_ffn_w_down):
    bsz = x.shape[0]
    assert x.shape[1:] == (SEQ, D_MODEL) and ctx.shape[1:] == (CTX_LEN, D_MODEL)
    rows = -(-(bsz + 1) // 8) * 8
    cond = jnp.concatenate([c, c_ctx[None, :], jnp.zeros((rows - bsz - 1, D_MODEL), F32)], axis=0)
    mod0 = _mod_rows(_ada(cond, l0_ada_w, l0_ada_b), bsz)
    mod1 = _mod_rows(_ada(cond, l1_ada_w, l1_ada_b), bsz)

    p0 = _layer0_params(l0_w_in, l0_mla_qa_g, l0_mla_w_uq, l0_mla_kva_g, l0_mla_w_ukv, l0_mla_qn_g,
                        l0_mla_kn_g, l0_na_qn_g, l0_na_kn_g, l0_norm1_g)
    qa, ka, va, qb, kb, vb = _proj0(x, ctx, mod0, p0)
    o_a = _mla(qa, ka, va)
    o_b = _na(qb, kb, vb, _na_table(l0_na_rpb))
    f0 = _ffn_params(l0_norm2_g, l0_w_out, l0_ffn_w_gate, l0_ffn_w_up, l0_ffn_w_down)
    xa = _ffn((x, ctx), o_a, o_b, mod0, f0, N_TILES)

    p1 = _layer1_params(l1_w_in, l1_gqa_qn_g, l1_gqa_kn_g, l1_diff_qn_g, l1_diff_kn_g, l1_norm1_g)
    qc, kc, vc, qd, kd, vd = _proj1(xa, mod1, p1)
    o_c = _gqa(l1_gqa_sink, qc, kc, vc)
    lv = jnp.zeros((8, LANES), F32).at[:4, :DIFF_HEAD_DIM].set(
        jnp.stack([l1_diff_lq1, l1_diff_lk1, l1_diff_lq2, l1_diff_lk2]))
    lam_init = 0.8 - 0.6 * math.exp(-0.3 * 1)
    o_d = _diff(lv, qd, kd, vd, l1_diff_subln_g.reshape(1, -1), lam_init)
    f1 = _ffn_params(l1_norm2_g, l1_w_out, l1_ffn_w_gate, l1_ffn_w_up, l1_ffn_w_down, pair_gqa=True)
    return _ffn((xa,), o_c, o_d, mod1, f1, N_LAT_TILES)
```

```python
import functools
import math

import numpy as np
import jax
import jax.numpy as jnp
from jax import lax
from jax.experimental import pallas as pl
from jax.experimental.pallas import tpu as pltpu

F32 = jnp.float32
BF16 = jnp.bfloat16

D_MODEL = 1024
SEQ = 2048
GRID_W = 64
CTX_LEN = 256
N_TOK = SEQ + CTX_LEN
ROPE_THETA = 10000.0
EPS = 1e-6

MLA_HEADS = 8
MLA_Q_RANK = 256
MLA_KV_RANK = 128
MLA_NOPE = 64
MLA_ROPE = 32
MLA_V = 64
MLA_QK = MLA_NOPE + MLA_ROPE

NA_HEADS = 8
NA_HEAD_DIM = 64
NA_ROWS_MAX = 8
NA_COLS = 16
NA_DIM = NA_HEADS * NA_HEAD_DIM

GQA_HEADS = 8
GQA_KV_HEADS = 2
GQA_HEAD_DIM = 64
GQA_WINDOW = 128

DIFF_HEADS = 4
DIFF_HEAD_DIM = 64

FFN_HIDDEN = 2816

LANES = 128
TOK_TILE = 256
N_TILES = N_TOK // TOK_TILE
N_LAT_TILES = SEQ // TOK_TILE
VMEM_LIMIT = 56 * 1024 * 1024
LOG2E = 1.4426950408889634
NEG_BIG = -1e30

ATT_TQ = 256
MLA_TQ = 256
NA_Q_ROWS = 2
NA_TQ = NA_Q_ROWS * GRID_W
NA_WIN_ROWS = 10
NA_WIN = NA_WIN_ROWS * GRID_W
NA_STEPS = SEQ // NA_TQ
GRID_ROWS = SEQ // GRID_W
GQA_TQ = 128
GQA_BAND = 3 * GQA_TQ


def _dot(a, b):
    return jnp.dot(a, b, preferred_element_type=F32)


def _dot_nt(a, b):
    return lax.dot_general(a, b, (((1,), (1,)), ((), ())), preferred_element_type=F32)


def _rms_rows(x):
    return lax.rsqrt(jnp.mean(x * x, axis=-1, keepdims=True) + EPS)


def _const_spec(a):
    nd = a.ndim
    return pl.BlockSpec(a.shape, lambda *_: (0,) * nd)


def _params(n_grid):
    return pltpu.CompilerParams(dimension_semantics=("arbitrary",) * n_grid,
                                vmem_limit_bytes=VMEM_LIMIT)


def _row0(i, size):
    return i * size if isinstance(i, int) else pl.multiple_of(i * size, size)


def _fold_lanes(xs, op):
    r = None
    for x in xs:
        for lo in range(0, x.shape[1], LANES):
            blk = x[:, lo:lo + LANES]
            r = blk if r is None else op(r, blk)
    return r


def _scores(q, keys, biases, s_ref, mx_ref, keys_t=False):
    parts = []
    lo = 0
    for k, bias in zip(keys, biases):
        sc = _dot(q, k) if keys_t else _dot_nt(q, k)
        if bias is not None:
            sc = sc + bias
        s_ref[:, lo:lo + sc.shape[1]] = sc
        lo += sc.shape[1]
        parts.append(sc)
    mx_ref[...] = _fold_lanes(parts, jnp.maximum)


def _softmax_pv(s_ref, mx_ref, vals, extra=None):
    m = jnp.max(mx_ref[...], axis=-1, keepdims=True)
    if extra is not None:
        m = jnp.maximum(m, extra)
    ps = []
    lo = 0
    for v in vals:
        ps.append(jnp.exp2(s_ref[:, lo:lo + v.shape[0]] - m))
        lo += v.shape[0]
    l = jnp.sum(_fold_lanes(ps, jnp.add), axis=-1, keepdims=True)
    if extra is not None:
        l = l + jnp.exp2(extra - m)
    acc = _dot(ps[0].astype(BF16), vals[0])
    for p, v in zip(ps[1:], vals[1:]):
        acc = acc + _dot(p.astype(BF16), v)
    return acc, l


def _skewed_loop(n, scores, finish, per_trip=8):
    assert n % per_trip == 0 and per_trip % 2 == 0
    scores(0, 0)

    def body(j, carry):
        for d in range(per_trip):
            i = per_trip * j + d
            scores(jnp.minimum(i + 1, n - 1), (d + 1) % 2)
            finish(i, d % 2)
        return carry

    lax.fori_loop(0, n // per_trip, body, 0)


def _att_scratch(lead, rows, n_keys):
    lead = (2,) + tuple(lead)
    return [pltpu.VMEM(lead + (rows, n_keys), F32), pltpu.VMEM(lead + (rows, LANES), F32)]


def _stack_halves(q):
    lane = lax.broadcasted_iota(jnp.int32, q.shape, 1)
    zero = jnp.zeros_like(q)
    return jnp.concatenate([jnp.where(lane < 64, q, zero), jnp.where(lane >= 64, q, zero)], axis=0)


def _merge_halves(lo, hi):
    lane = lax.broadcasted_iota(jnp.int32, lo.shape, 1)
    return jnp.where(lane < 64, lo, hi)


def _ada_kernel(c_ref, w_ref, b_ref, o_ref):
    c = c_ref[...]
    a = (c * (1.0 / (1.0 + jnp.exp(-c)))).astype(BF16)
    o_ref[...] = _dot(a, w_ref[...].astype(BF16)) + b_ref[...]


def _ada(cond, w, b):
    n = w.shape[1]
    tn = 1024
    return pl.pallas_call(
        _ada_kernel,
        grid=(n // tn,),
        in_specs=[pl.BlockSpec(cond.shape, lambda j: (0, 0)),
                  pl.BlockSpec((w.shape[0], tn), lambda j: (0, j)),
                  pl.BlockSpec((1, tn), lambda j: (0, j))],
        out_specs=pl.BlockSpec((cond.shape[0], tn), lambda j: (0, j)),
        out_shape=jax.ShapeDtypeStruct((cond.shape[0], n), F32),
        compiler_params=_params(1),
        name="ada_modulation",
    )(cond, w, b.reshape(1, n))


def _tok_spec(width):
    return pl.BlockSpec((1, TOK_TILE, width), lambda t, b: (b, t, 0))


def _lat_spec():
    return pl.BlockSpec((1, TOK_TILE, D_MODEL), lambda t, b: (b, jnp.minimum(t, N_LAT_TILES - 1), 0))


def _ctx_spec():
    return pl.BlockSpec((1, TOK_TILE, D_MODEL), lambda t, b: (b, 0, 0))


def _mod_spec():
    return pl.BlockSpec((1, 1, 6, D_MODEL), lambda t, b: (b, t // N_LAT_TILES, 0, 0))


def _table_spec(width):
    return pl.BlockSpec((TOK_TILE, width), lambda t, b: (t, 0))


def _read_tokens(x_ref, c_ref):
    return jnp.where(pl.program_id(0) < N_LAT_TILES, x_ref[0], c_ref[0])


W0_CQ, W0_CKV, W0_KR, W0_KRP, W0_QN, W0_KN, W0_VN, W0_END = 0, 256, 384, 512, 640, 1152, 1664, 2176


def _head_inv(z, n):
    return lax.rsqrt(jnp.sum(z * z, axis=-1, keepdims=True) * (1.0 / n) + EPS)


def _half_inv(z):
    z2 = z * z
    lane = lax.broadcasted_iota(jnp.int32, z.shape, 1)
    lo = jnp.sum(jnp.where(lane < 64, z2, 0.0), axis=-1, keepdims=True)
    hi = jnp.sum(z2, axis=-1, keepdims=True) - lo
    return jnp.where(lane < 64, lax.rsqrt(lo * (1.0 / 64.0) + EPS), lax.rsqrt(hi * (1.0 / 64.0) + EPS))


def _proj0_kernel(x_ref, c_ref, mod_ref, g1_ref, w0_ref, qag_ref, wuq_ref, kvag_ref, wukv_ref,
                  aq_ref, bq_ref, ak_ref, bk_ref, gqb_ref, gkb_ref,
                  qa_ref, ka_ref, va_ref, qb_ref, kb_ref, vb_ref):
    x = _read_tokens(x_ref, c_ref)
    mod = mod_ref[0, 0]
    h = x * _rms_rows(x) * g1_ref[...] * (1.0 + mod[1:2]) + mod[0:1]
    z = _dot(h.astype(BF16), w0_ref[...])

    cq = z[:, W0_CQ:W0_CKV]
    cqn = (cq * _rms_rows(cq) * qag_ref[...]).astype(BF16)
    zq = _dot(cqn, wuq_ref[...])
    aq, bq = aq_ref[...], bq_ref[...]
    for hd in range(MLA_HEADS):
        qm = zq[:, hd * LANES:(hd + 1) * LANES]
        qr = zq[:, 1024 + hd * LANES:1024 + (hd + 1) * LANES]
        qa_ref[0, :, hd * LANES:(hd + 1) * LANES] = ((qm * aq + qr * bq) * _head_inv(qm, MLA_QK)).astype(BF16)

    ckv = z[:, W0_CKV:W0_KR]
    ckvn = (ckv * _rms_rows(ckv) * kvag_ref[...]).astype(BF16)
    zkv = _dot(ckvn, wukv_ref[...])
    kr = z[:, W0_KR:W0_KRP]
    kr_rot = z[:, W0_KRP:W0_QN] * bk_ref[...]
    ak = ak_ref[...]
    for hd in range(MLA_HEADS):
        km = zkv[:, hd * LANES:(hd + 1) * LANES] + kr
        ka_ref[0, :, hd * LANES:(hd + 1) * LANES] = ((km * ak + kr_rot) * _head_inv(km, MLA_QK)).astype(BF16)
    va_ref[0] = zkv[:, 1024:].astype(BF16)

    for j in range(NA_DIM // LANES):
        cols = slice(j * LANES, (j + 1) * LANES)
        qn = z[:, W0_QN + j * LANES:W0_QN + (j + 1) * LANES]
        kn = z[:, W0_KN + j * LANES:W0_KN + (j + 1) * LANES]
        qb_ref[0, :, cols] = (qn * gqb_ref[...] * _half_inv(qn)).astype(BF16)
        kb_ref[0, :, cols] = (kn * gkb_ref[...] * _half_inv(kn)).astype(BF16)
    vb_ref[0] = z[:, W0_VN:W0_END].astype(BF16)


def _proj0(x, ctx, mod, p):
    bsz = x.shape[0]
    consts1 = [p['g1'], p['w0'], p['qag'], p['wuq'], p['kvag'], p['wukv']]
    tables = [p['aq'], p['bq'], p['ak'], p['bk']]
    consts2 = [p['gqb'], p['gkb']]
    widths = [1024, 1024, 512, 512, 512, 512]
    return pl.pallas_call(
        _proj0_kernel,
        grid=(N_TILES, bsz),
        in_specs=([_lat_spec(), _ctx_spec(), _mod_spec()] + [_const_spec(a) for a in consts1]
                  + [_table_spec(LANES) for _ in tables] + [_const_spec(a) for a in consts2]),
        out_specs=[_tok_spec(w) for w in widths],
        out_shape=[jax.ShapeDtypeStruct((bsz, N_TOK, w), BF16) for w in widths],
        compiler_params=_params(2),
        name="proj_layer0",
    )(x, ctx, mod, *consts1, *tables, *consts2)


N1_SEG = (4, 1, 4, 4)
N1_NORM = 1664
N1_MAIN = 2304


def _proj1_kernel(x_ref, mod_ref, g1_ref, w1_ref, a_ref, b_ref,
                  qc_ref, kc_ref, vc_ref, qd_ref, kd_ref, vd_ref):
    x = x_ref[0]
    mod = mod_ref[0, 0]
    h = x * _rms_rows(x) * g1_ref[...] * (1.0 + mod[1:2]) + mod[0:1]
    z = _dot(h.astype(BF16), w1_ref[...])
    outs = (qc_ref, kc_ref, qd_ref, kd_ref)
    blk = 0
    for seg, n_blk in enumerate(N1_SEG):
        a = a_ref[:, seg * LANES:(seg + 1) * LANES]
        b = b_ref[:, seg * LANES:(seg + 1) * LANES]
        for j in range(n_blk):
            zm = z[:, blk * LANES:(blk + 1) * LANES]
            rot = z[:, N1_MAIN + blk * LANES:N1_MAIN + (blk + 1) * LANES]
            outs[seg][0, :, j * LANES:(j + 1) * LANES] = ((zm * a + rot * b) * _half_inv(zm)).astype(BF16)
            blk += 1
    vc_ref[0] = z[:, 1664:1792].astype(BF16)
    vd_ref[0] = z[:, 1792:2304].astype(BF16)


def _proj1(xa, mod, p):
    bsz = xa.shape[0]
    consts = [p['g1'], p['w1']]
    widths = [512, 128, 128, 512, 512, 512]
    return pl.pallas_call(
        _proj1_kernel,
        grid=(N_TILES, bsz),
        in_specs=([_tok_spec(D_MODEL), _mod_spec()] + [_const_spec(a) for a in consts]
                  + [_table_spec(4 * LANES), _table_spec(4 * LANES)]),
        out_specs=[_tok_spec(w) for w in widths],
        out_shape=[jax.ShapeDtypeStruct((bsz, N_TOK, w), BF16) for w in widths],
        compiler_params=_params(2),
        name="proj_layer1",
    )(xa, mod, *consts, p['a'], p['b'])


def _mla_kernel(q_ref, k_ref, v_ref, o_ref, s_scr, mx_scr, kt_scr):
    for hh in range(2):
        kt_scr[hh] = k_ref[0, :, hh * LANES:(hh + 1) * LANES].T

    def scores(r0, slot, key_lo):
        for hh in range(2):
            cols = slice(hh * LANES, (hh + 1) * LANES)
            rows = slice(hh * MLA_TQ, (hh + 1) * MLA_TQ)
            _scores(q_ref[0, pl.ds(r0, MLA_TQ), cols], [kt_scr[hh, :, key_lo:]], [None],
                    s_scr.at[slot, rows], mx_scr.at[slot, rows], keys_t=True)

    def finish(r0, slot, key_lo):
        acc, l = _softmax_pv(s_scr.at[slot], mx_scr.at[slot], [v_ref[0, key_lo:, :]])
        r = acc / l
        o_ref[0, pl.ds(r0, MLA_TQ), :] = _merge_halves(r[:MLA_TQ], r[MLA_TQ:]).astype(BF16)

    _skewed_loop(SEQ // MLA_TQ,
                 lambda i, slot: scores(_row0(i, MLA_TQ), slot, 0),
                 lambda i, slot: finish(_row0(i, MLA_TQ), slot, 0))
    scores(SEQ, 0, SEQ)
    finish(SEQ, 0, SEQ)


def _mla(qa, ka, va):
    bsz = qa.shape[0]
    assert CTX_LEN == MLA_TQ
    return pl.pallas_call(
        _mla_kernel,
        grid=(bsz, MLA_HEADS // 2),
        in_specs=[pl.BlockSpec((1, N_TOK, 2 * LANES), lambda b, j: (b, 0, j)),
                  pl.BlockSpec((1, N_TOK, 2 * LANES), lambda b, j: (b, 0, j)),
                  pl.BlockSpec((1, N_TOK, LANES), lambda b, j: (b, 0, j))],
        out_specs=pl.BlockSpec((1, N_TOK, LANES), lambda b, j: (b, 0, j)),
        out_shape=jax.ShapeDtypeStruct((bsz, N_TOK, MLA_HEADS * MLA_V), BF16),
        scratch_shapes=_att_scratch((), 2 * MLA_TQ, N_TOK) + [pltpu.VMEM((2, LANES, N_TOK), BF16)],
        compiler_params=_params(2),
        name="mla_attention",
    )(qa, ka, va)


def _na_kernel(q_ref, k_ref, v_ref, t_ref, o_ref, s_scr, mx_scr):
    def window_start(u):
        w0 = jnp.minimum(jnp.clip(NA_Q_ROWS * u - NA_ROWS_MAX // 2, 0, GRID_ROWS - NA_ROWS_MAX),
                         GRID_ROWS - NA_WIN_ROWS)
        return pl.multiple_of(w0 * GRID_W, GRID_W)

    def scores(u, slot):
        ws = window_start(u)
        r0 = _row0(u, NA_TQ)
        var = jnp.minimum(u, 2) + jnp.maximum(u - (NA_STEPS - 3), 0)
        for j in range(NA_HEADS // 2):
            cols = slice(j * LANES, (j + 1) * LANES)
            qq = _stack_halves(q_ref[0, pl.ds(r0, NA_TQ), cols])
            _scores(qq, [k_ref[0, pl.ds(ws, NA_WIN), cols], k_ref[0, SEQ:, cols]], [t_ref[var, j], None],
                    s_scr.at[slot, j], mx_scr.at[slot, j])

    def finish(u, slot):
        ws = window_start(u)
        r0 = _row0(u, NA_TQ)
        for j in range(NA_HEADS // 2):
            cols = slice(j * LANES, (j + 1) * LANES)
            acc, l = _softmax_pv(s_scr.at[slot, j], mx_scr.at[slot, j],
                                 [v_ref[0, pl.ds(ws, NA_WIN), cols], v_ref[0, SEQ:, cols]])
            r = acc / l
            o_ref[0, pl.ds(r0, NA_TQ), cols] = _merge_halves(r[:NA_TQ], r[NA_TQ:]).astype(BF16)

    _skewed_loop(NA_STEPS, scores, finish)
    for r0 in range(SEQ, N_TOK, NA_TQ):
        for j in range(NA_HEADS // 2):
            cols = slice(j * LANES, (j + 1) * LANES)
            qq = _stack_halves(q_ref[0, r0:r0 + NA_TQ, cols])
            _scores(qq, [k_ref[0, SEQ:, cols]], [None], s_scr.at[0, j], mx_scr.at[0, j])
            acc, l = _softmax_pv(s_scr.at[0, j], mx_scr.at[0, j], [v_ref[0, SEQ:, cols]])
            r = acc / l
            o_ref[0, r0:r0 + NA_TQ, cols] = _merge_halves(r[:NA_TQ], r[NA_TQ:]).astype(BF16)


def _na(qb, kb, vb, table):
    bsz = qb.shape[0]
    tok = pl.BlockSpec((1, N_TOK, NA_DIM), lambda b: (b, 0, 0))
    n_pairs = NA_HEADS // 2
    return pl.pallas_call(
        _na_kernel,
        grid=(bsz,),
        in_specs=[tok, tok, tok, _const_spec(table)],
        out_specs=tok,
        out_shape=jax.ShapeDtypeStruct((bsz, N_TOK, NA_DIM), BF16),
        scratch_shapes=_att_scratch((n_pairs,), 2 * NA_TQ, NA_WIN + CTX_LEN),
        compiler_params=_params(1),
        name="neighbourhood_attention",
    )(qb, kb, vb, table)


def _gqa_kernel(sink_ref, q_ref, k_ref, v_ref, o_ref, s_scr, mx_scr):
    row = lax.broadcasted_iota(jnp.int32, (2 * GQA_TQ, GQA_BAND), 0)
    col = lax.broadcasted_iota(jnp.int32, (2 * GQA_TQ, GQA_BAND), 1)
    rel = jnp.where(row >= GQA_TQ, row - GQA_TQ, row) - col
    hi_rows = lax.broadcasted_iota(jnp.int32, (2 * GQA_TQ, 1), 0) >= GQA_TQ

    def band_start(r0):
        return pl.multiple_of(jnp.clip(r0 - GQA_TQ, 0, SEQ - GQA_BAND), GQA_TQ)

    def scores(n, slot):
        r0 = _row0(n, GQA_TQ)
        start = band_start(r0)
        mask = jnp.where(jnp.abs(rel + (r0 - start)) <= GQA_WINDOW, 0.0, NEG_BIG)
        keys = [k_ref[0, pl.ds(start, GQA_BAND), :], k_ref[0, SEQ:, :]]
        for j in range(GQA_HEADS // 2):
            qq = _stack_halves(q_ref[0, pl.ds(r0, GQA_TQ), j * LANES:(j + 1) * LANES])
            _scores(qq, keys, [mask, None], s_scr.at[slot, j], mx_scr.at[slot, j])

    def finish(n, slot):
        r0 = _row0(n, GQA_TQ)
        start = band_start(r0)
        vals = [v_ref[0, pl.ds(start, GQA_BAND), :], v_ref[0, SEQ:, :]]
        for j in range(GQA_HEADS // 2):
            sink = jnp.where(hi_rows, sink_ref[j + GQA_HEADS // 2], sink_ref[j]) * LOG2E
            acc, l = _softmax_pv(s_scr.at[slot, j], mx_scr.at[slot, j], vals, extra=sink)
            r = acc / l
            o_ref[0, pl.ds(r0, GQA_TQ), j * LANES:(j + 1) * LANES] = _merge_halves(r[:GQA_TQ], r[GQA_TQ:]).astype(BF16)

    _skewed_loop(SEQ // GQA_TQ, scores, finish)


def _gqa(sink, qc, kc, vc):
    bsz = qc.shape[0]
    return pl.pallas_call(
        _gqa_kernel,
        grid=(bsz,),
        in_specs=[pl.BlockSpec(memory_space=pltpu.SMEM),
                  pl.BlockSpec((1, SEQ, 512), lambda b: (b, 0, 0)),
                  pl.BlockSpec((1, N_TOK, LANES), lambda b: (b, 0, 0)),
                  pl.BlockSpec((1, N_TOK, LANES), lambda b: (b, 0, 0))],
        out_specs=pl.BlockSpec((1, SEQ, 512), lambda b: (b, 0, 0)),
        out_shape=jax.ShapeDtypeStruct((bsz, SEQ, 512), BF16),
        scratch_shapes=_att_scratch((GQA_HEADS // 2,), 2 * GQA_TQ, GQA_BAND + CTX_LEN),
        compiler_params=_params(1),
        name="windowed_gqa",
    )(sink, qc, kc, vc)


def _diff_kernel(lv_ref, q_ref, k_ref, v_ref, g_ref, o_ref, s_scr, mx_scr, kt_scr, *, lam_init):
    kt_scr[...] = k_ref[0].T
    lv = lv_ref[...]
    lam = (jnp.exp(jnp.sum(lv[0:1] * lv[1:2], axis=-1, keepdims=True))
           - jnp.exp(jnp.sum(lv[2:3] * lv[3:4], axis=-1, keepdims=True)) + lam_init)
    hi_rows = lax.broadcasted_iota(jnp.int32, (2 * ATT_TQ, 1), 0) >= ATT_TQ
    coef = jnp.where(hi_rows, -lam, 1.0)
    g = g_ref[...] * (1.0 - lam_init)

    def scores(i, slot):
        qq = _stack_halves(q_ref[0, pl.ds(_row0(i, ATT_TQ), ATT_TQ), :])
        _scores(qq, [kt_scr[...]], [None], s_scr.at[slot], mx_scr.at[slot], keys_t=True)

    def finish(i, slot):
        acc, l = _softmax_pv(s_scr.at[slot], mx_scr.at[slot], [v_ref[0]])
        r = acc * (coef / l)
        o = r[:ATT_TQ] + r[ATT_TQ:]
        o_ref[0, pl.ds(_row0(i, ATT_TQ), ATT_TQ), :] = (o * _rms_rows(o) * g).astype(BF16)

    _skewed_loop(SEQ // ATT_TQ, scores, finish)


def _diff(lv, qd, kd, vd, g, lam_init):
    bsz = qd.shape[0]
    return pl.pallas_call(
        functools.partial(_diff_kernel, lam_init=lam_init),
        grid=(bsz, DIFF_HEADS),
        in_specs=[_const_spec(lv),
                  pl.BlockSpec((1, SEQ, LANES), lambda b, h: (b, 0, h)),
                  pl.BlockSpec((1, N_TOK, LANES), lambda b, h: (b, 0, h)),
                  pl.BlockSpec((1, N_TOK, LANES), lambda b, h: (b, 0, h)),
                  _const_spec(g)],
        out_specs=pl.BlockSpec((1, SEQ, LANES), lambda b, h: (b, 0, h)),
        out_shape=jax.ShapeDtypeStruct((bsz, SEQ, DIFF_HEADS * 2 * DIFF_HEAD_DIM), BF16),
        scratch_shapes=_att_scratch((), 2 * ATT_TQ, N_TOK) + [pltpu.VMEM((LANES, N_TOK), BF16)],
        compiler_params=_params(2),
        name="diff_attention",
    )(lv, qd, kd, vd, g)


def _ffn_body(x, o1_ref, o2_ref, mod_ref, g2_ref, wo1_ref, wo2_ref, wg_ref, wu_ref, wd_ref, y_ref):
    mod = mod_ref[0, 0]
    attn = _dot(o1_ref[0], wo1_ref[...]) + _dot(o2_ref[0], wo2_ref[...])
    x1 = x + mod[2:3] * attn
    h = (x1 * _rms_rows(x1) * g2_ref[...] * (1.0 + mod[4:5]) + mod[3:4]).astype(BF16)
    gate = _dot(h, wg_ref[...])
    up = _dot(h, wu_ref[...])
    a = (gate * (1.0 / (1.0 + jnp.exp(-gate))) * up).astype(BF16)
    y_ref[0] = x1 + mod[5:6] * _dot(a, wd_ref[...])


def _ffn0_kernel(x_ref, c_ref, *rest):
    _ffn_body(_read_tokens(x_ref, c_ref), *rest)


def _ffn1_kernel(x_ref, *rest):
    _ffn_body(x_ref[0], *rest)


def _ffn(xs, o1, o2, mod, p, n_tiles):
    bsz = xs[0].shape[0]
    consts = [p['g2'], p['wo1'], p['wo2'], p['wg'], p['wu'], p['wd']]
    x_specs = [_lat_spec(), _ctx_spec()] if len(xs) == 2 else [_tok_spec(D_MODEL)]
    return pl.pallas_call(
        _ffn0_kernel if len(xs) == 2 else _ffn1_kernel,
        grid=(n_tiles, bsz),
        in_specs=(x_specs + [_tok_spec(512), _tok_spec(512), _mod_spec()] + [_const_spec(a) for a in consts]),
        out_specs=_tok_spec(D_MODEL),
        out_shape=jax.ShapeDtypeStruct((bsz, n_tiles * TOK_TILE, D_MODEL), F32),
        compiler_params=_params(2),
        name="outproj_ffn",
    )(*xs, o1, o2, mod, *consts)


def _partner_cols(w, dr):
    q = dr // 4
    shp = w.shape
    return jnp.flip(w.reshape(shp[:-1] + (shp[-1] // dr, 2, 2, q)), axis=-2).reshape(shp)


def _rope_tables(dr):
    h = dr // 2
    q = dr // 4
    freqs = ROPE_THETA ** (-jnp.arange(0, h, 2, dtype=F32) / h)
    t = jnp.arange(SEQ)
    pos = jnp.stack([t // GRID_W, t % GRID_W], axis=1).astype(F32)
    ang = (pos[:, :, None, None] * freqs[None, None, None, :])
    ang = jnp.broadcast_to(ang, (SEQ, 2, 2, q)).reshape(SEQ, dr)
    sign = np.tile(np.repeat(np.array([-1.0, 1.0], np.float32), q), 2)
    cos = jnp.concatenate([jnp.cos(ang), jnp.ones((CTX_LEN, dr), F32)], axis=0)
    sin = jnp.concatenate([jnp.sin(ang) * sign, jnp.zeros((CTX_LEN, dr), F32)], axis=0)
    return cos, sin


def _layer0_params(w_in, qa_g, w_uq, kva_g, w_ukv, qn_g, kn_g, na_qn_g, na_kn_g, norm1_g):
    cos, sin = _rope_tables(MLA_ROPE)
    o_kr = MLA_Q_RANK + MLA_KV_RANK
    kr_w = w_in[:, o_kr:o_kr + MLA_ROPE]
    lane_pad = ((0, 0), (MLA_NOPE, LANES - MLA_QK))
    w0 = jnp.concatenate([w_in[:, :o_kr], jnp.pad(kr_w, lane_pad), jnp.pad(_partner_cols(kr_w, MLA_ROPE), lane_pad),
                          w_in[:, o_kr + MLA_ROPE:]], axis=1).astype(BF16)
    r = w_uq.shape[0]
    w3 = w_uq.reshape(r, MLA_HEADS, MLA_QK)
    main = jnp.pad(w3, ((0, 0), (0, 0), (0, LANES - MLA_QK)))
    rot = jnp.pad(_partner_cols(w3[:, :, MLA_NOPE:], MLA_ROPE), ((0, 0), (0, 0), (MLA_NOPE, LANES - MLA_QK)))
    wuq = jnp.concatenate([main.reshape(r, -1), rot.reshape(r, -1)], axis=1).astype(BF16)

    def tables(g, c):
        z32 = jnp.zeros((N_TOK, LANES - MLA_QK), F32)
        a = jnp.concatenate([jnp.broadcast_to(g[None, :MLA_NOPE], (N_TOK, MLA_NOPE)), g[None, MLA_NOPE:] * cos, z32], axis=1)
        b = jnp.concatenate([jnp.zeros((N_TOK, MLA_NOPE), F32), _partner_cols(g[None, MLA_NOPE:], MLA_ROPE) * sin, z32], axis=1)
        return a * c, b * c

    aq, bq = tables(qn_g, MLA_QK ** -0.5 * LOG2E)
    ak, bk = tables(kn_g, 1.0)
    rk = w_ukv.shape[0]
    k3 = w_ukv.reshape(rk, MLA_HEADS, MLA_NOPE + MLA_V)
    wukv = jnp.concatenate([jnp.pad(k3[:, :, :MLA_NOPE], ((0, 0), (0, 0), (0, LANES - MLA_NOPE))).reshape(rk, -1),
                            k3[:, :, MLA_NOPE:].reshape(rk, -1)], axis=1).astype(BF16)
    return dict(
        g1=norm1_g.reshape(1, -1), w0=w0, qag=qa_g.reshape(1, -1), wuq=wuq, kvag=kva_g.reshape(1, -1),
        wukv=wukv, aq=aq, bq=bq, ak=ak, bk=bk,
        gqb=(jnp.tile(na_qn_g, 2) * (NA_HEAD_DIM ** -0.5 * LOG2E)).reshape(1, -1),
        gkb=jnp.tile(na_kn_g, 2).reshape(1, -1))


def _pair_heads(w, axis):
    g = GQA_HEADS // GQA_KV_HEADS
    shp = w.shape
    w4 = w.reshape(shp[:axis] + (GQA_KV_HEADS, g, GQA_HEAD_DIM) + shp[axis + 1:])
    return jnp.swapaxes(w4, axis, axis + 1).reshape(shp)


def _layer1_params(w_in, gqa_qn_g, gqa_kn_g, diff_qn_g, diff_kn_g, norm1_g):
    cos, sin = _rope_tables(64)
    main = jnp.concatenate([_pair_heads(w_in[:, :512], 1), w_in[:, 512:640], w_in[:, 768:1792]], axis=1)
    w1 = jnp.concatenate([main, w_in[:, 640:768], w_in[:, 1792:], _partner_cols(main, 64)], axis=1).astype(BF16)
    a_parts, b_parts = [], []
    for g, c in ((gqa_qn_g, GQA_HEAD_DIM ** -0.5 * LOG2E), (gqa_kn_g, 1.0),
                 (diff_qn_g, DIFF_HEAD_DIM ** -0.5 * LOG2E), (diff_kn_g, 1.0)):
        a_parts.append(jnp.tile(g[None, :] * cos * c, (1, 2)))
        b_parts.append(jnp.tile(_partner_cols(g[None, :], 64) * sin * c, (1, 2)))
    return dict(g1=norm1_g.reshape(1, -1), w1=w1,
                a=jnp.concatenate(a_parts, axis=1), b=jnp.concatenate(b_parts, axis=1))


def _na_table(rpb):
    us = np.array([0, 1, 2, NA_STEPS - 2, NA_STEPS - 1])
    n_var = len(us)
    qi = np.arange(NA_TQ)
    kk = np.arange(NA_WIN)
    r = NA_Q_ROWS * us[:, None, None] + (qi // GRID_W)[None, :, None]
    c = (qi % GRID_W)[None, :, None]
    w0 = np.minimum(np.clip(NA_Q_ROWS * us - NA_ROWS_MAX // 2, 0, GRID_ROWS - NA_ROWS_MAX), GRID_ROWS - NA_WIN_ROWS)
    key_r = w0[:, None, None] + (kk // GRID_W)[None, None, :]
    key_c = (kk % GRID_W)[None, None, :]
    r0 = np.clip(r - NA_ROWS_MAX // 2, 0, GRID_ROWS - NA_ROWS_MAX)
    c0 = np.clip(c - NA_COLS // 2, 0, GRID_W - NA_COLS)
    valid = (key_r >= r0) & (key_r < r0 + NA_ROWS_MAX) & (key_c >= c0) & (key_c < c0 + NA_COLS)
    valid = np.tile(valid, (1, 2, 1))[:, None]
    off_r = np.clip(w0[:, None, None] + np.arange(NA_WIN_ROWS)[None, None, :]
                    - (NA_Q_ROWS * us[:, None, None] + np.arange(NA_Q_ROWS)[None, :, None])
                    + (NA_ROWS_MAX - 1), 0, 2 * NA_ROWS_MAX - 2)
    n_rows, n_off = 2 * NA_ROWS_MAX - 1, 2 * NA_COLS - 1
    pick = (off_r.reshape(-1, 1) == np.arange(n_rows)[None, :]).astype(np.float32)
    cc = np.arange(GRID_W)
    spread = ((cc[None, None, :] - cc[None, :, None] + NA_COLS - 1) == np.arange(n_off)[:, None, None]).astype(np.float32)
    rows = jnp.einsum('nr,hrd->hnd', pick, rpb, precision=lax.Precision.HIGHEST)
    toep = jnp.einsum('hnd,dck->hnck', rows, spread, precision=lax.Precision.HIGHEST)
    toep = toep.reshape(NA_HEADS // 2, 2, n_var, NA_Q_ROWS, NA_WIN_ROWS, GRID_W, GRID_W)
    bias = jnp.transpose(toep, (2, 0, 1, 3, 5, 4, 6)).reshape(n_var, NA_HEADS // 2, 2 * NA_TQ, NA_WIN)
    return jnp.where(jnp.asarray(valid), bias * LOG2E, NEG_BIG)


def _ffn_params(norm2_g, w_out, w_gate, w_up, w_down, pair_gqa=False):
    wo1 = _pair_heads(w_out[:512], 0) if pair_gqa else w_out[:512]
    return dict(g2=norm2_g.reshape(1, -1), wo1=wo1.astype(BF16), wo2=w_out[512:].astype(BF16),
                wg=w_gate.astype(BF16), wu=w_up.astype(BF16), wd=w_down.astype(BF16))


def _mod_rows(mod_all, bsz):
    lat = mod_all[:bsz].reshape(bsz, 1, 6, D_MODEL)
    cx = jnp.broadcast_to(mod_all[bsz].reshape(1, 1, 6, D_MODEL), (bsz, 1, 6, D_MODEL))
    return jnp.concatenate([lat, cx], axis=1)


def kernel(x, c, ctx, c_ctx, l0_ada_w, l0_ada_b, l0_norm1_g, l0_norm2_g, l0_w_in, l0_mla_qa_g, l0_mla_w_uq, l0_mla_kva_g, l0_mla_w_ukv, l0_mla_qn_g, l0_mla_kn_g, l0_na_qn_g, l0_na_kn_g, l0_na_rpb, l0_w_out, l0_ffn_w_gate, l0_ffn_w_up, l0_ffn_w_down, l1_ada_w, l1_ada_b, l1_norm1_g, l1_norm2_g, l1_w_in, l1_gqa_qn_g, l1_gqa_kn_g, l1_gqa_sink, l1_diff_qn_g, l1_diff_kn_g, l1_diff_lq1, l1_diff_lk1, l1_diff_lq2, l1_diff_lk2, l1_diff_subln_g, l1_w_out, l1_ffn_w_gate, l1_ffn_w_up, l1_ffn_w_down):
    bsz = x.shape[0]
    assert x.shape[1:] == (SEQ, D_MODEL) and ctx.shape[1:] == (CTX_LEN, D_MODEL)
    rows = -(-(bsz + 1) // 8) * 8
    cond = jnp.concatenate([c, c_ctx[None, :], jnp.zeros((rows - bsz - 1, D_MODEL), F32)], axis=0)
    mod0 = _mod_rows(_ada(cond, l0_ada_w, l0_ada_b), bsz)
    mod1 = _mod_rows(_ada(cond, l1_ada_w, l1_ada_b), bsz)

    p0 = _layer0_params(l0_w_in, l0_mla_qa_g, l0_mla_w_uq, l0_mla_kva_g, l0_mla_w_ukv, l0_mla_qn_g,
                        l0_mla_kn_g, l0_na_qn_g, l0_na_kn_g, l0_norm1_g)
    qa, ka, va, qb, kb, vb = _proj0(x, ctx, mod0, p0)
    o_a = _mla(qa, ka, va)
    o_b = _na(qb, kb, vb, _na_table(l0_na_rpb))
    f0 = _ffn_params(l0_norm2_g, l0_w_out, l0_ffn_w_gate, l0_ffn_w_up, l0_ffn_w_down)
    xa = _ffn((x, ctx), o_a, o_b, mod0, f0, N_TILES)

    p1 = _layer1_params(l1_w_in, l1_gqa_qn_g, l1_gqa_kn_g, l1_diff_qn_g, l1_diff_kn_g, l1_norm1_g)
    qc, kc, vc, qd, kd, vd = _proj1(xa, mod1, p1)
    o_c = _gqa(l1_gqa_sink, qc, kc, vc)
    lv = jnp.zeros((8, LANES), F32).at[:4, :DIFF_HEAD_DIM].set(
        jnp.stack([l1_diff_lq1, l1_diff_lk1, l1_diff_lq2, l1_diff_lk2]))
    lam_init = 0.8 - 0.6 * math.exp(-0.3 * 1)
    o_d = _diff(lv, qd, kd, vd, l1_diff_subln_g.reshape(1, -1), lam_init)
    f1 = _ffn_params(l1_norm2_g, l1_w_out, l1_ffn_w_gate, l1_ffn_w_up, l1_ffn_w_down, pair_gqa=True)
    return _ffn((xa,), o_c, o_d, mod1, f1, N_LAT_TILES)
```

```python
import functools
import math

import numpy as np
import jax
import jax.numpy as jnp
from jax import lax
from jax.experimental import pallas as pl
from jax.experimental.pallas import tpu as pltpu

F32 = jnp.float32
BF16 = jnp.bfloat16

D_MODEL = 1024
SEQ = 2048
GRID_W = 64
CTX_LEN = 256
N_TOK = SEQ + CTX_LEN
ROPE_THETA = 10000.0
EPS = 1e-6

MLA_HEADS = 8
MLA_Q_RANK = 256
MLA_KV_RANK = 128
MLA_NOPE = 64
MLA_ROPE = 32
MLA_V = 64
MLA_QK = MLA_NOPE + MLA_ROPE

NA_HEADS = 8
NA_HEAD_DIM = 64
NA_ROWS_MAX = 8
NA_COLS = 16
NA_DIM = NA_HEADS * NA_HEAD_DIM

GQA_HEADS = 8
GQA_KV_HEADS = 2
GQA_HEAD_DIM = 64
GQA_WINDOW = 128

DIFF_HEADS = 4
DIFF_HEAD_DIM = 64

FFN_HIDDEN = 2816

LANES = 128
TOK_TILE = 256
N_TILES = N_TOK // TOK_TILE
N_LAT_TILES = SEQ // TOK_TILE
VMEM_LIMIT = 56 * 1024 * 1024
LOG2E = 1.4426950408889634
NEG_BIG = -1e30

ATT_TQ = 256
MLA_TQ = 256
NA_Q_ROWS = 2
NA_TQ = NA_Q_ROWS * GRID_W
NA_WIN_ROWS = 10
NA_WIN = NA_WIN_ROWS * GRID_W
NA_STEPS = SEQ // NA_TQ
GRID_ROWS = SEQ // GRID_W
GQA_TQ = 128
GQA_BAND = 3 * GQA_TQ


def _dot(a, b):
    return jnp.dot(a, b, preferred_element_type=F32)


def _dot_nt(a, b):
    return lax.dot_general(a, b, (((1,), (1,)), ((), ())), preferred_element_type=F32)


def _rms_rows(x):
    return lax.rsqrt(jnp.mean(x * x, axis=-1, keepdims=True) + EPS)


def _const_spec(a):
    nd = a.ndim
    return pl.BlockSpec(a.shape, lambda *_: (0,) * nd)


def _params(n_grid):
    return pltpu.CompilerParams(dimension_semantics=("arbitrary",) * n_grid,
                                vmem_limit_bytes=VMEM_LIMIT)


def _fold_lanes(xs, op):
    r = None
    for x in xs:
        for lo in range(0, x.shape[1], LANES):
            blk = x[:, lo:lo + LANES]
            r = blk if r is None else op(r, blk)
    return r


def _scores(q, keys, biases, s_ref, mx_ref, keys_t=False):
    parts = []
    lo = 0
    for k, bias in zip(keys, biases):
        sc = _dot(q, k) if keys_t else _dot_nt(q, k)
        if bias is not None:
            sc = sc + bias
        s_ref[:, lo:lo + sc.shape[1]] = sc
        lo += sc.shape[1]
        parts.append(sc)
    mx_ref[...] = _fold_lanes(parts, jnp.maximum)


def _softmax_pv(s_ref, mx_ref, vals, extra=None):
    m = jnp.max(mx_ref[...], axis=-1, keepdims=True)
    if extra is not None:
        m = jnp.maximum(m, extra)
    ps = []
    lo = 0
    for v in vals:
        ps.append(jnp.exp2(s_ref[:, lo:lo + v.shape[0]] - m))
        lo += v.shape[0]
    l = jnp.sum(_fold_lanes(ps, jnp.add), axis=-1, keepdims=True)
    if extra is not None:
        l = l + jnp.exp2(extra - m)
    acc = _dot(ps[0].astype(BF16), vals[0])
    for p, v in zip(ps[1:], vals[1:]):
        acc = acc + _dot(p.astype(BF16), v)
    return acc, l


def _run_skewed(units):
    units[0][0](0)
    for i, (_, finish) in enumerate(units):
        if i + 1 < len(units):
            units[i + 1][0]((i + 1) % 2)
        finish(i % 2)


def _row0(i, size):
    return i * size if isinstance(i, int) else pl.multiple_of(i * size, size)


def _skewed_loop(n, scores, finish, per_trip=8):
    assert n % per_trip == 0 and per_trip % 2 == 0
    scores(0, 0)

    def body(j, carry):
        for d in range(per_trip):
            i = per_trip * j + d
            scores(jnp.minimum(i + 1, n - 1), (d + 1) % 2)
            finish(i, d % 2)
        return carry

    lax.fori_loop(0, n // per_trip, body, 0)


def _att_scratch(lead, rows, n_keys):
    lead = (2,) + tuple(lead)
    return [pltpu.VMEM(lead + (rows, n_keys), F32), pltpu.VMEM(lead + (rows, LANES), F32)]


def _stack_halves(q):
    lane = lax.broadcasted_iota(jnp.int32, q.shape, 1)
    zero = jnp.zeros_like(q)
    return jnp.concatenate([jnp.where(lane < 64, q, zero), jnp.where(lane >= 64, q, zero)], axis=0)


def _merge_halves(lo, hi):
    lane = lax.broadcasted_iota(jnp.int32, lo.shape, 1)
    return jnp.where(lane < 64, lo, hi)


def _ada_kernel(c_ref, w_ref, b_ref, o_ref):
    c = c_ref[...]
    a = (c * (1.0 / (1.0 + jnp.exp(-c)))).astype(BF16)
    o_ref[...] = _dot(a, w_ref[...].astype(BF16)) + b_ref[...]


def _ada(cond, w, b):
    n = w.shape[1]
    tn = 1024
    return pl.pallas_call(
        _ada_kernel,
        grid=(n // tn,),
        in_specs=[pl.BlockSpec(cond.shape, lambda j: (0, 0)),
                  pl.BlockSpec((w.shape[0], tn), lambda j: (0, j)),
                  pl.BlockSpec((1, tn), lambda j: (0, j))],
        out_specs=pl.BlockSpec((cond.shape[0], tn), lambda j: (0, j)),
        out_shape=jax.ShapeDtypeStruct((cond.shape[0], n), F32),
        compiler_params=_params(1),
        name="ada_modulation",
    )(cond, w, b.reshape(1, n))


def _tok_spec(width):
    return pl.BlockSpec((1, TOK_TILE, width), lambda t, b: (b, t, 0))


def _lat_spec():
    return pl.BlockSpec((1, TOK_TILE, D_MODEL), lambda t, b: (b, jnp.minimum(t, N_LAT_TILES - 1), 0))


def _ctx_spec():
    return pl.BlockSpec((1, TOK_TILE, D_MODEL), lambda t, b: (b, 0, 0))


def _mod_spec():
    return pl.BlockSpec((1, 1, 6, D_MODEL), lambda t, b: (b, t // N_LAT_TILES, 0, 0))


def _table_spec(width):
    return pl.BlockSpec((TOK_TILE, width), lambda t, b: (t, 0))


def _read_tokens(x_ref, c_ref):
    return jnp.where(pl.program_id(0) < N_LAT_TILES, x_ref[0], c_ref[0])


W0_CQ, W0_CKV, W0_KR, W0_KRP, W0_QN, W0_KN, W0_VN, W0_END = 0, 256, 384, 512, 640, 1152, 1664, 2176


def _head_inv(z, n):
    return lax.rsqrt(jnp.sum(z * z, axis=-1, keepdims=True) * (1.0 / n) + EPS)


def _half_inv(z):
    z2 = z * z
    lane = lax.broadcasted_iota(jnp.int32, z.shape, 1)
    lo = jnp.sum(jnp.where(lane < 64, z2, 0.0), axis=-1, keepdims=True)
    hi = jnp.sum(z2, axis=-1, keepdims=True) - lo
    return jnp.where(lane < 64, lax.rsqrt(lo * (1.0 / 64.0) + EPS), lax.rsqrt(hi * (1.0 / 64.0) + EPS))


def _proj0_kernel(x_ref, c_ref, mod_ref, g1_ref, w0_ref, qag_ref, wuq_ref, kvag_ref, wukv_ref,
                  aq_ref, bq_ref, ak_ref, bk_ref, gqb_ref, gkb_ref,
                  qa_ref, ka_ref, va_ref, qb_ref, kb_ref, vb_ref):
    x = _read_tokens(x_ref, c_ref)
    mod = mod_ref[0, 0]
    h = x * _rms_rows(x) * g1_ref[...] * (1.0 + mod[1:2]) + mod[0:1]
    z = _dot(h.astype(BF16), w0_ref[...])

    cq = z[:, W0_CQ:W0_CKV]
    cqn = (cq * _rms_rows(cq) * qag_ref[...]).astype(BF16)
    zq = _dot(cqn, wuq_ref[...])
    aq, bq = aq_ref[...], bq_ref[...]
    for hd in range(MLA_HEADS):
        qm = zq[:, hd * LANES:(hd + 1) * LANES]
        qr = zq[:, 1024 + hd * LANES:1024 + (hd + 1) * LANES]
        qa_ref[0, :, hd * LANES:(hd + 1) * LANES] = ((qm * aq + qr * bq) * _head_inv(qm, MLA_QK)).astype(BF16)

    ckv = z[:, W0_CKV:W0_KR]
    ckvn = (ckv * _rms_rows(ckv) * kvag_ref[...]).astype(BF16)
    zkv = _dot(ckvn, wukv_ref[...])
    kr = z[:, W0_KR:W0_KRP]
    kr_rot = z[:, W0_KRP:W0_QN] * bk_ref[...]
    ak = ak_ref[...]
    for hd in range(MLA_HEADS):
        km = zkv[:, hd * LANES:(hd + 1) * LANES] + kr
        ka_ref[0, :, hd * LANES:(hd + 1) * LANES] = ((km * ak + kr_rot) * _head_inv(km, MLA_QK)).astype(BF16)
    va_ref[0] = zkv[:, 1024:].astype(BF16)

    for j in range(NA_DIM // LANES):
        cols = slice(j * LANES, (j + 1) * LANES)
        qn = z[:, W0_QN + j * LANES:W0_QN + (j + 1) * LANES]
        kn = z[:, W0_KN + j * LANES:W0_KN + (j + 1) * LANES]
        qb_ref[0, :, cols] = (qn * gqb_ref[...] * _half_inv(qn)).astype(BF16)
        kb_ref[0, :, cols] = (kn * gkb_ref[...] * _half_inv(kn)).astype(BF16)
    vb_ref[0] = z[:, W0_VN:W0_END].astype(BF16)


def _proj0(x, ctx, mod, p):
    bsz = x.shape[0]
    consts1 = [p['g1'], p['w0'], p['qag'], p['wuq'], p['kvag'], p['wukv']]
    tables = [p['aq'], p['bq'], p['ak'], p['bk']]
    consts2 = [p['gqb'], p['gkb']]
    widths = [1024, 1024, 512, 512, 512, 512]
    return pl.pallas_call(
        _proj0_kernel,
        grid=(N_TILES, bsz),
        in_specs=([_lat_spec(), _ctx_spec(), _mod_spec()] + [_const_spec(a) for a in consts1]
                  + [_table_spec(LANES) for _ in tables] + [_const_spec(a) for a in consts2]),
        out_specs=[_tok_spec(w) for w in widths],
        out_shape=[jax.ShapeDtypeStruct((bsz, N_TOK, w), BF16) for w in widths],
        compiler_params=_params(2),
        name="proj_layer0",
    )(x, ctx, mod, *consts1, *tables, *consts2)


N1_SEG = (4, 1, 4, 4)
N1_NORM = 1664
N1_MAIN = 2304


def _proj1_kernel(x_ref, mod_ref, g1_ref, w1_ref, a_ref, b_ref,
                  qc_ref, kc_ref, vc_ref, qd_ref, kd_ref, vd_ref):
    x = x_ref[0]
    mod = mod_ref[0, 0]
    h = x * _rms_rows(x) * g1_ref[...] * (1.0 + mod[1:2]) + mod[0:1]
    z = _dot(h.astype(BF16), w1_ref[...])
    outs = (qc_ref, kc_ref, qd_ref, kd_ref)
    blk = 0
    for seg, n_blk in enumerate(N1_SEG):
        a = a_ref[:, seg * LANES:(seg + 1) * LANES]
        b = b_ref[:, seg * LANES:(seg + 1) * LANES]
        for j in range(n_blk):
            zm = z[:, blk * LANES:(blk + 1) * LANES]
            rot = z[:, N1_MAIN + blk * LANES:N1_MAIN + (blk + 1) * LANES]
            outs[seg][0, :, j * LANES:(j + 1) * LANES] = ((zm * a + rot * b) * _half_inv(zm)).astype(BF16)
            blk += 1
    vc_ref[0] = z[:, 1664:1792].astype(BF16)
    vd_ref[0] = z[:, 1792:2304].astype(BF16)


def _proj1(xa, mod, p):
    bsz = xa.shape[0]
    consts = [p['g1'], p['w1']]
    widths = [512, 128, 128, 512, 512, 512]
    return pl.pallas_call(
        _proj1_kernel,
        grid=(N_TILES, bsz),
        in_specs=([_tok_spec(D_MODEL), _mod_spec()] + [_const_spec(a) for a in consts]
                  + [_table_spec(4 * LANES), _table_spec(4 * LANES)]),
        out_specs=[_tok_spec(w) for w in widths],
        out_shape=[jax.ShapeDtypeStruct((bsz, N_TOK, w), BF16) for w in widths],
        compiler_params=_params(2),
        name="proj_layer1",
    )(xa, mod, *consts, p['a'], p['b'])


def _mla_kernel(q_ref, k_ref, v_ref, o_ref, s_scr, mx_scr, kt_scr):
    for hh in range(2):
        kt_scr[hh] = k_ref[0, :, hh * LANES:(hh + 1) * LANES].T

    def scores(r0, slot, key_lo):
        for hh in range(2):
            cols = slice(hh * LANES, (hh + 1) * LANES)
            rows = slice(hh * MLA_TQ, (hh + 1) * MLA_TQ)
            _scores(q_ref[0, r0:r0 + MLA_TQ, cols], [kt_scr[hh, :, key_lo:]], [None],
                    s_scr.at[slot, rows], mx_scr.at[slot, rows], keys_t=True)

    def finish(r0, slot, key_lo):
        acc, l = _softmax_pv(s_scr.at[slot], mx_scr.at[slot], [v_ref[0, key_lo:, :]])
        r = acc / l
        o_ref[0, r0:r0 + MLA_TQ, :] = _merge_halves(r[:MLA_TQ], r[MLA_TQ:]).astype(BF16)

    units = [(functools.partial(scores, r0, key_lo=0), functools.partial(finish, r0, key_lo=0))
             for r0 in range(0, SEQ, MLA_TQ)]
    units.append((functools.partial(scores, SEQ, key_lo=SEQ),
                  functools.partial(finish, SEQ, key_lo=SEQ)))
    _run_skewed(units)


def _mla(qa, ka, va):
    bsz = qa.shape[0]
    assert CTX_LEN == MLA_TQ
    return pl.pallas_call(
        _mla_kernel,
        grid=(bsz, MLA_HEADS // 2),
        in_specs=[pl.BlockSpec((1, N_TOK, 2 * LANES), lambda b, j: (b, 0, j)),
                  pl.BlockSpec((1, N_TOK, 2 * LANES), lambda b, j: (b, 0, j)),
                  pl.BlockSpec((1, N_TOK, LANES), lambda b, j: (b, 0, j))],
        out_specs=pl.BlockSpec((1, N_TOK, LANES), lambda b, j: (b, 0, j)),
        out_shape=jax.ShapeDtypeStruct((bsz, N_TOK, MLA_HEADS * MLA_V), BF16),
        scratch_shapes=_att_scratch((), 2 * MLA_TQ, N_TOK) + [pltpu.VMEM((2, LANES, N_TOK), BF16)],
        compiler_params=_params(2),
        name="mla_attention",
    )(qa, ka, va)


def _na_kernel(q_ref, k_ref, v_ref, t_ref, o_ref, s_scr, mx_scr):
    n_pairs = NA_HEADS // 2

    def window_start(u):
        w0 = jnp.minimum(jnp.clip(NA_Q_ROWS * u - NA_ROWS_MAX // 2, 0, GRID_ROWS - NA_ROWS_MAX),
                         GRID_ROWS - NA_WIN_ROWS)
        return pl.multiple_of(w0 * GRID_W, GRID_W)

    def scores(u, slot):
        ws = window_start(u)
        r0 = _row0(u, NA_TQ)
        var = jnp.minimum(u, 2) + jnp.maximum(u - (NA_STEPS - 3), 0)
        for j in range(n_pairs):
            cols = slice(j * LANES, (j + 1) * LANES)
            qq = _stack_halves(q_ref[0, pl.ds(r0, NA_TQ), cols])
            _scores(qq, [k_ref[0, pl.ds(ws, NA_WIN), cols], k_ref[0, SEQ:, cols]], [t_ref[var, j], None],
                    s_scr.at[slot, j], mx_scr.at[slot, j])

    def finish(u, slot):
        ws = window_start(u)
        r0 = _row0(u, NA_TQ)
        for j in range(n_pairs):
            cols = slice(j * LANES, (j + 1) * LANES)
            acc, l = _softmax_pv(s_scr.at[slot, j], mx_scr.at[slot, j],
                                 [v_ref[0, pl.ds(ws, NA_WIN), cols], v_ref[0, SEQ:, cols]])
            r = acc / l
            o_ref[0, pl.ds(r0, NA_TQ), cols] = _merge_halves(r[:NA_TQ], r[NA_TQ:]).astype(BF16)

    _skewed_loop(NA_STEPS, scores, finish)

    def ctx_scores(r0, slot):
        for j in range(n_pairs):
            cols = slice(j * LANES, (j + 1) * LANES)
            qq = _stack_halves(q_ref[0, r0:r0 + NA_TQ, cols])
            _scores(qq, [k_ref[0, SEQ:, cols]], [None], s_scr.at[slot, j], mx_scr.at[slot, j])

    def ctx_finish(r0, slot):
        for j in range(n_pairs):
            cols = slice(j * LANES, (j + 1) * LANES)
            acc, l = _softmax_pv(s_scr.at[slot, j], mx_scr.at[slot, j], [v_ref[0, SEQ:, cols]])
            r = acc / l
            o_ref[0, r0:r0 + NA_TQ, cols] = _merge_halves(r[:NA_TQ], r[NA_TQ:]).astype(BF16)

    _run_skewed([(functools.partial(ctx_scores, r0), functools.partial(ctx_finish, r0))
                 for r0 in range(SEQ, N_TOK, NA_TQ)])


def _na(qb, kb, vb, table):
    bsz = qb.shape[0]
    tok = pl.BlockSpec((1, N_TOK, NA_DIM), lambda b: (b, 0, 0))
    n_pairs = NA_HEADS // 2
    return pl.pallas_call(
        _na_kernel,
        grid=(bsz,),
        in_specs=[tok, tok, tok, _const_spec(table)],
        out_specs=tok,
        out_shape=jax.ShapeDtypeStruct((bsz, N_TOK, NA_DIM), BF16),
        scratch_shapes=_att_scratch((n_pairs,), 2 * NA_TQ, NA_WIN + CTX_LEN),
        compiler_params=_params(1),
        name="neighbourhood_attention",
    )(qb, kb, vb, table)


def _gqa_kernel(sink_ref, q_ref, k_ref, v_ref, o_ref, s_scr, mx_scr):
    row = lax.broadcasted_iota(jnp.int32, (2 * GQA_TQ, GQA_BAND), 0)
    col = lax.broadcasted_iota(jnp.int32, (2 * GQA_TQ, GQA_BAND), 1)
    rel = jnp.where(row >= GQA_TQ, row - GQA_TQ, row) - col
    hi_rows = lax.broadcasted_iota(jnp.int32, (2 * GQA_TQ, 1), 0) >= GQA_TQ

    def band_start(r0):
        return pl.multiple_of(jnp.clip(r0 - GQA_TQ, 0, SEQ - GQA_BAND), GQA_TQ)

    def scores(n, slot):
        r0 = _row0(n, GQA_TQ)
        start = band_start(r0)
        mask = jnp.where(jnp.abs(rel + (r0 - start)) <= GQA_WINDOW, 0.0, NEG_BIG)
        keys = [k_ref[0, pl.ds(start, GQA_BAND), :], k_ref[0, SEQ:, :]]
        for j in range(GQA_HEADS // 2):
            qq = _stack_halves(q_ref[0, pl.ds(r0, GQA_TQ), j * LANES:(j + 1) * LANES])
            _scores(qq, keys, [mask, None], s_scr.at[slot, j], mx_scr.at[slot, j])

    def finish(n, slot):
        r0 = _row0(n, GQA_TQ)
        start = band_start(r0)
        vals = [v_ref[0, pl.ds(start, GQA_BAND), :], v_ref[0, SEQ:, :]]
        for j in range(GQA_HEADS // 2):
            sink = jnp.where(hi_rows, sink_ref[j + GQA_HEADS // 2], sink_ref[j]) * LOG2E
            acc, l = _softmax_pv(s_scr.at[slot, j], mx_scr.at[slot, j], vals, extra=sink)
            r = acc / l
            o_ref[0, pl.ds(r0, GQA_TQ), j * LANES:(j + 1) * LANES] = _merge_halves(r[:GQA_TQ], r[GQA_TQ:]).astype(BF16)

    _skewed_loop(SEQ // GQA_TQ, scores, finish)


def _gqa(sink, qc, kc, vc):
    bsz = qc.shape[0]
    return pl.pallas_call(
        _gqa_kernel,
        grid=(bsz,),
        in_specs=[pl.BlockSpec(memory_space=pltpu.SMEM),
                  pl.BlockSpec((1, SEQ, 512), lambda b: (b, 0, 0)),
                  pl.BlockSpec((1, N_TOK, LANES), lambda b: (b, 0, 0)),
                  pl.BlockSpec((1, N_TOK, LANES), lambda b: (b, 0, 0))],
        out_specs=pl.BlockSpec((1, SEQ, 512), lambda b: (b, 0, 0)),
        out_shape=jax.ShapeDtypeStruct((bsz, SEQ, 512), BF16),
        scratch_shapes=_att_scratch((GQA_HEADS // 2,), 2 * GQA_TQ, GQA_BAND + CTX_LEN),
        compiler_params=_params(1),
        name="windowed_gqa",
    )(sink, qc, kc, vc)


def _diff_kernel(lv_ref, q_ref, k_ref, v_ref, g_ref, o_ref, s_scr, mx_scr, kt_scr, *, lam_init):
    kt_scr[...] = k_ref[0].T
    lv = lv_ref[...]
    lam = (jnp.exp(jnp.sum(lv[0:1] * lv[1:2], axis=-1, keepdims=True))
           - jnp.exp(jnp.sum(lv[2:3] * lv[3:4], axis=-1, keepdims=True)) + lam_init)
    hi_rows = lax.broadcasted_iota(jnp.int32, (2 * ATT_TQ, 1), 0) >= ATT_TQ
    coef = jnp.where(hi_rows, -lam, 1.0)
    g = g_ref[...] * (1.0 - lam_init)

    def scores(i, slot):
        qq = _stack_halves(q_ref[0, i * ATT_TQ:(i + 1) * ATT_TQ, :])
        _scores(qq, [kt_scr[...]], [None], s_scr.at[slot], mx_scr.at[slot], keys_t=True)

    def finish(i, slot):
        acc, l = _softmax_pv(s_scr.at[slot], mx_scr.at[slot], [v_ref[0]])
        r = acc * (coef / l)
        o = r[:ATT_TQ] + r[ATT_TQ:]
        o_ref[0, i * ATT_TQ:(i + 1) * ATT_TQ, :] = (o * _rms_rows(o) * g).astype(BF16)

    _run_skewed([(functools.partial(scores, i), functools.partial(finish, i)) for i in range(SEQ // ATT_TQ)])


def _diff(lv, qd, kd, vd, g, lam_init):
    bsz = qd.shape[0]
    return pl.pallas_call(
        functools.partial(_diff_kernel, lam_init=lam_init),
        grid=(bsz, DIFF_HEADS),
        in_specs=[_const_spec(lv),
                  pl.BlockSpec((1, SEQ, LANES), lambda b, h: (b, 0, h)),
                  pl.BlockSpec((1, N_TOK, LANES), lambda b, h: (b, 0, h)),
                  pl.BlockSpec((1, N_TOK, LANES), lambda b, h: (b, 0, h)),
                  _const_spec(g)],
        out_specs=pl.BlockSpec((1, SEQ, LANES), lambda b, h: (b, 0, h)),
        out_shape=jax.ShapeDtypeStruct((bsz, SEQ, DIFF_HEADS * 2 * DIFF_HEAD_DIM), BF16),
        scratch_shapes=_att_scratch((), 2 * ATT_TQ, N_TOK) + [pltpu.VMEM((LANES, N_TOK), BF16)],
        compiler_params=_params(2),
        name="diff_attention",
    )(lv, qd, kd, vd, g)


def _ffn_body(x, o1_ref, o2_ref, mod_ref, g2_ref, wo1_ref, wo2_ref, wg_ref, wu_ref, wd_ref, y_ref):
    mod = mod_ref[0, 0]
    attn = _dot(o1_ref[0], wo1_ref[...]) + _dot(o2_ref[0], wo2_ref[...])
    x1 = x + mod[2:3] * attn
    h = (x1 * _rms_rows(x1) * g2_ref[...] * (1.0 + mod[4:5]) + mod[3:4]).astype(BF16)
    gate = _dot(h, wg_ref[...])
    up = _dot(h, wu_ref[...])
    a = (gate * (1.0 / (1.0 + jnp.exp(-gate))) * up).astype(BF16)
    y_ref[0] = x1 + mod[5:6] * _dot(a, wd_ref[...])


def _ffn0_kernel(x_ref, c_ref, *rest):
    _ffn_body(_read_tokens(x_ref, c_ref), *rest)


def _ffn1_kernel(x_ref, *rest):
    _ffn_body(x_ref[0], *rest)


def _ffn(xs, o1, o2, mod, p, n_tiles):
    bsz = xs[0].shape[0]
    consts = [p['g2'], p['wo1'], p['wo2'], p['wg'], p['wu'], p['wd']]
    x_specs = [_lat_spec(), _ctx_spec()] if len(xs) == 2 else [_tok_spec(D_MODEL)]
    return pl.pallas_call(
        _ffn0_kernel if len(xs) == 2 else _ffn1_kernel,
        grid=(n_tiles, bsz),
        in_specs=(x_specs + [_tok_spec(512), _tok_spec(512), _mod_spec()] + [_const_spec(a) for a in consts]),
        out_specs=_tok_spec(D_MODEL),
        out_shape=jax.ShapeDtypeStruct((bsz, n_tiles * TOK_TILE, D_MODEL), F32),
        compiler_params=_params(2),
        name="outproj_ffn",
    )(*xs, o1, o2, mod, *consts)


def _partner_cols(w, dr):
    q = dr // 4
    shp = w.shape
    return jnp.flip(w.reshape(shp[:-1] + (shp[-1] // dr, 2, 2, q)), axis=-2).reshape(shp)


def _rope_tables(dr):
    h = dr // 2
    q = dr // 4
    freqs = ROPE_THETA ** (-jnp.arange(0, h, 2, dtype=F32) / h)
    t = jnp.arange(SEQ)
    pos = jnp.stack([t // GRID_W, t % GRID_W], axis=1).astype(F32)
    ang = (pos[:, :, None, None] * freqs[None, None, None, :])
    ang = jnp.broadcast_to(ang, (SEQ, 2, 2, q)).reshape(SEQ, dr)
    sign = np.tile(np.repeat(np.array([-1.0, 1.0], np.float32), q), 2)
    cos = jnp.concatenate([jnp.cos(ang), jnp.ones((CTX_LEN, dr), F32)], axis=0)
    sin = jnp.concatenate([jnp.sin(ang) * sign, jnp.zeros((CTX_LEN, dr), F32)], axis=0)
    return cos, sin


def _layer0_params(w_in, qa_g, w_uq, kva_g, w_ukv, qn_g, kn_g, na_qn_g, na_kn_g, norm1_g):
    cos, sin = _rope_tables(MLA_ROPE)
    o_kr = MLA_Q_RANK + MLA_KV_RANK
    kr_w = w_in[:, o_kr:o_kr + MLA_ROPE]
    lane_pad = ((0, 0), (MLA_NOPE, LANES - MLA_QK))
    w0 = jnp.concatenate([w_in[:, :o_kr], jnp.pad(kr_w, lane_pad), jnp.pad(_partner_cols(kr_w, MLA_ROPE), lane_pad),
                          w_in[:, o_kr + MLA_ROPE:]], axis=1).astype(BF16)
    r = w_uq.shape[0]
    w3 = w_uq.reshape(r, MLA_HEADS, MLA_QK)
    main = jnp.pad(w3, ((0, 0), (0, 0), (0, LANES - MLA_QK)))
    rot = jnp.pad(_partner_cols(w3[:, :, MLA_NOPE:], MLA_ROPE), ((0, 0), (0, 0), (MLA_NOPE, LANES - MLA_QK)))
    wuq = jnp.concatenate([main.reshape(r, -1), rot.reshape(r, -1)], axis=1).astype(BF16)

    def tables(g, c):
        z32 = jnp.zeros((N_TOK, LANES - MLA_QK), F32)
        a = jnp.concatenate([jnp.broadcast_to(g[None, :MLA_NOPE], (N_TOK, MLA_NOPE)), g[None, MLA_NOPE:] * cos, z32], axis=1)
        b = jnp.concatenate([jnp.zeros((N_TOK, MLA_NOPE), F32), _partner_cols(g[None, MLA_NOPE:], MLA_ROPE) * sin, z32], axis=1)
        return a * c, b * c

    aq, bq = tables(qn_g, MLA_QK ** -0.5 * LOG2E)
    ak, bk = tables(kn_g, 1.0)
    rk = w_ukv.shape[0]
    k3 = w_ukv.reshape(rk, MLA_HEADS, MLA_NOPE + MLA_V)
    wukv = jnp.concatenate([jnp.pad(k3[:, :, :MLA_NOPE], ((0, 0), (0, 0), (0, LANES - MLA_NOPE))).reshape(rk, -1),
                            k3[:, :, MLA_NOPE:].reshape(rk, -1)], axis=1).astype(BF16)
    return dict(
        g1=norm1_g.reshape(1, -1), w0=w0, qag=qa_g.reshape(1, -1), wuq=wuq, kvag=kva_g.reshape(1, -1),
        wukv=wukv, aq=aq, bq=bq, ak=ak, bk=bk,
        gqb=(jnp.tile(na_qn_g, 2) * (NA_HEAD_DIM ** -0.5 * LOG2E)).reshape(1, -1),
        gkb=jnp.tile(na_kn_g, 2).reshape(1, -1))


def _pair_heads(w, axis):
    g = GQA_HEADS // GQA_KV_HEADS
    shp = w.shape
    w4 = w.reshape(shp[:axis] + (GQA_KV_HEADS, g, GQA_HEAD_DIM) + shp[axis + 1:])
    return jnp.swapaxes(w4, axis, axis + 1).reshape(shp)


def _layer1_params(w_in, gqa_qn_g, gqa_kn_g, diff_qn_g, diff_kn_g, norm1_g):
    cos, sin = _rope_tables(64)
    main = jnp.concatenate([_pair_heads(w_in[:, :512], 1), w_in[:, 512:640], w_in[:, 768:1792]], axis=1)
    w1 = jnp.concatenate([main, w_in[:, 640:768], w_in[:, 1792:], _partner_cols(main, 64)], axis=1).astype(BF16)
    a_parts, b_parts = [], []
    for g, c in ((gqa_qn_g, GQA_HEAD_DIM ** -0.5 * LOG2E), (gqa_kn_g, 1.0),
                 (diff_qn_g, DIFF_HEAD_DIM ** -0.5 * LOG2E), (diff_kn_g, 1.0)):
        a_parts.append(jnp.tile(g[None, :] * cos * c, (1, 2)))
        b_parts.append(jnp.tile(_partner_cols(g[None, :], 64) * sin * c, (1, 2)))
    return dict(g1=norm1_g.reshape(1, -1), w1=w1,
                a=jnp.concatenate(a_parts, axis=1), b=jnp.concatenate(b_parts, axis=1))


def _na_table_kernel(w_ref, o_ref):
    row = lax.broadcasted_iota(jnp.int32, (GRID_W, LANES), 0)
    lane = lax.broadcasted_iota(jnp.int32, (GRID_W, LANES), 1)
    kc = jnp.where(lane < GRID_W, lane, lane - GRID_W)
    c0 = jnp.clip(row - NA_COLS // 2, 0, GRID_W - NA_COLS)
    band = (kc >= c0) & (kc < c0 + NA_COLS)
    shift = LANES - (GRID_W - 1)
    for hf in range(2):
        for a in range(NA_Q_ROWS):
            for jp in range(NA_WIN_ROWS // 2):
                n0 = a * NA_WIN_ROWS + 2 * jp
                lo = jnp.broadcast_to(w_ref[0, hf, 0, n0:n0 + 1, :], (GRID_W, LANES))
                hi = jnp.broadcast_to(w_ref[0, hf, 0, n0 + 1:n0 + 2, :], (GRID_W, LANES))
                lo = pltpu.roll(lo, shift, 1, stride=1, stride_axis=0)
                hi = pltpu.roll(hi, (shift + GRID_W) % LANES, 1, stride=1, stride_axis=0)
                r0 = (hf * NA_Q_ROWS + a) * GRID_W
                o_ref[0, 0, r0:r0 + GRID_W, jp * LANES:(jp + 1) * LANES] = jnp.where(
                    band, jnp.where(lane < GRID_W, lo, hi), NEG_BIG)


def _na_table(rpb):
    us = np.array([0, 1, 2, NA_STEPS - 2, NA_STEPS - 1])
    n_var = len(us)
    w0 = np.minimum(np.clip(NA_Q_ROWS * us - NA_ROWS_MAX // 2, 0, GRID_ROWS - NA_ROWS_MAX), GRID_ROWS - NA_WIN_ROWS)
    r = NA_Q_ROWS * us[:, None, None] + np.arange(NA_Q_ROWS)[None, :, None]
    key_r = w0[:, None, None] + np.arange(NA_WIN_ROWS)[None, None, :]
    r0 = np.clip(r - NA_ROWS_MAX // 2, 0, GRID_ROWS - NA_ROWS_MAX)
    row_valid = (key_r >= r0) & (key_r < r0 + NA_ROWS_MAX)
    off_r = np.clip(key_r - r + (NA_ROWS_MAX - 1), 0, 2 * NA_ROWS_MAX - 2)
    pick = (off_r.reshape(-1, 1) == np.arange(2 * NA_ROWS_MAX - 1)[None, :]).astype(np.float32)
    rows = jnp.einsum('nr,hrd->hnd', pick, rpb, precision=lax.Precision.HIGHEST) * LOG2E
    pad = GRID_W - NA_COLS
    w = jnp.pad(rows, ((0, 0), (0, 0), (pad, LANES - pad - (2 * NA_COLS - 1))))
    w = jnp.where(jnp.asarray(row_valid.reshape(1, -1, 1)), w, NEG_BIG)
    w = w.reshape(NA_HEADS // 2, 2, n_var, NA_Q_ROWS * NA_WIN_ROWS, LANES)
    return pl.pallas_call(
        _na_table_kernel,
        grid=(n_var, NA_HEADS // 2),
        in_specs=[pl.BlockSpec((1, 2, 1, NA_Q_ROWS * NA_WIN_ROWS, LANES), lambda v, p: (p, 0, v, 0, 0))],
        out_specs=pl.BlockSpec((1, 1, 2 * NA_TQ, NA_WIN), lambda v, p: (v, p, 0, 0)),
        out_shape=jax.ShapeDtypeStruct((n_var, NA_HEADS // 2, 2 * NA_TQ, NA_WIN), F32),
        compiler_params=_params(2),
        name="na_bias_table",
    )(w)


def _ffn_params(norm2_g, w_out, w_gate, w_up, w_down, pair_gqa=False):
    wo1 = _pair_heads(w_out[:512], 0) if pair_gqa else w_out[:512]
    return dict(g2=norm2_g.reshape(1, -1), wo1=wo1.astype(BF16), wo2=w_out[512:].astype(BF16),
                wg=w_gate.astype(BF16), wu=w_up.astype(BF16), wd=w_down.astype(BF16))


def _mod_rows(mod_all, bsz):
    lat = mod_all[:bsz].reshape(bsz, 1, 6, D_MODEL)
    cx = jnp.broadcast_to(mod_all[bsz].reshape(1, 1, 6, D_MODEL), (bsz, 1, 6, D_MODEL))
    return jnp.concatenate([lat, cx], axis=1)


def kernel(x, c, ctx, c_ctx, l0_ada_w, l0_ada_b, l0_norm1_g, l0_norm2_g, l0_w_in, l0_mla_qa_g, l0_mla_w_uq, l0_mla_kva_g, l0_mla_w_ukv, l0_mla_qn_g, l0_mla_kn_g, l0_na_qn_g, l0_na_kn_g, l0_na_rpb, l0_w_out, l0_ffn_w_gate, l0_ffn_w_up, l0_ffn_w_down, l1_ada_w, l1_ada_b, l1_norm1_g, l1_norm2_g, l1_w_in, l1_gqa_qn_g, l1_gqa_kn_g, l1_gqa_sink, l1_diff_qn_g, l1_diff_kn_g, l1_diff_lq1, l1_diff_lk1, l1_diff_lq2, l1_diff_lk2, l1_diff_subln_g, l1_w_out, l1_ffn_w_gate, l1_ffn_w_up, l1_ffn_w_down):
    bsz = x.shape[0]
    assert x.shape[1:] == (SEQ, D_MODEL) and ctx.shape[1:] == (CTX_LEN, D_MODEL)
    rows = -(-(bsz + 1) // 8) * 8
    cond = jnp.concatenate([c, c_ctx[None, :], jnp.zeros((rows - bsz - 1, D_MODEL), F32)], axis=0)
    mod0 = _mod_rows(_ada(cond, l0_ada_w, l0_ada_b), bsz)
    mod1 = _mod_rows(_ada(cond, l1_ada_w, l1_ada_b), bsz)

    p0 = _layer0_params(l0_w_in, l0_mla_qa_g, l0_mla_w_uq, l0_mla_kva_g, l0_mla_w_ukv, l0_mla_qn_g,
                        l0_mla_kn_g, l0_na_qn_g, l0_na_kn_g, l0_norm1_g)
    qa, ka, va, qb, kb, vb = _proj0(x, ctx, mod0, p0)
    o_a = _mla(qa, ka, va)
    o_b = _na(qb, kb, vb, _na_table(l0_na_rpb))
    f0 = _ffn_params(l0_norm2_g, l0_w_out, l0_ffn_w_gate, l0_ffn_w_up, l0_ffn_w_down)
    xa = _ffn((x, ctx), o_a, o_b, mod0, f0, N_TILES)

    p1 = _layer1_params(l1_w_in, l1_gqa_qn_g, l1_gqa_kn_g, l1_diff_qn_g, l1_diff_kn_g, l1_norm1_g)
    qc, kc, vc, qd, kd, vd = _proj1(xa, mod1, p1)
    o_c = _gqa(l1_gqa_sink, qc, kc, vc)
    lv = jnp.zeros((8, LANES), F32).at[:4, :DIFF_HEAD_DIM].set(
        jnp.stack([l1_diff_lq1, l1_diff_lk1, l1_diff_lq2, l1_diff_lk2]))
    lam_init = 0.8 - 0.6 * math.exp(-0.3 * 1)
    o_d = _diff(lv, qd, kd, vd, l1_diff_subln_g.reshape(1, -1), lam_init)
    f1 = _ffn_params(l1_norm2_g, l1_w_out, l1_ffn_w_gate, l1_ffn_w_up, l1_ffn_w_down, pair_gqa=True)
    return _ffn((xa,), o_c, o_d, mod1, f1, N_LAT_TILES)
```

```python
import functools
import math

import numpy as np
import jax
import jax.numpy as jnp
from jax import lax
from jax.experimental import pallas as pl
from jax.experimental.pallas import tpu as pltpu

F32 = jnp.float32
BF16 = jnp.bfloat16

D_MODEL = 1024
SEQ = 2048
GRID_W = 64
CTX_LEN = 256
N_TOK = SEQ + CTX_LEN
ROPE_THETA = 10000.0
EPS = 1e-6

MLA_HEADS = 8
MLA_Q_RANK = 256
MLA_KV_RANK = 128
MLA_NOPE = 64
MLA_ROPE = 32
MLA_V = 64
MLA_QK = MLA_NOPE + MLA_ROPE

NA_HEADS = 8
NA_HEAD_DIM = 64
NA_ROWS_MAX = 8
NA_COLS = 16
NA_DIM = NA_HEADS * NA_HEAD_DIM

GQA_HEADS = 8
GQA_KV_HEADS = 2
GQA_HEAD_DIM = 64
GQA_WINDOW = 128

DIFF_HEADS = 4
DIFF_HEAD_DIM = 64

FFN_HIDDEN = 2816

LANES = 128
TOK_TILE = 256
N_TILES = N_TOK // TOK_TILE
N_LAT_TILES = SEQ // TOK_TILE
VMEM_LIMIT = 56 * 1024 * 1024
LOG2E = 1.4426950408889634
NEG_BIG = -1e30

ATT_TQ = 256
MLA_TQ = 256
NA_Q_ROWS = 2
NA_TQ = NA_Q_ROWS * GRID_W
NA_WIN_ROWS = 10
NA_WIN = NA_WIN_ROWS * GRID_W
NA_STEPS = SEQ // NA_TQ
NA_VARIANT_UNITS = (0, 1, 2, NA_STEPS - 2, NA_STEPS - 1)
GRID_ROWS = SEQ // GRID_W
GQA_TQ = 128
GQA_BAND = GQA_TQ + 2 * GQA_WINDOW


def _dot(a, b):
    return jnp.dot(a, b, preferred_element_type=F32)


def _dot_nt(a, b):
    return lax.dot_general(a, b, (((1,), (1,)), ((), ())), preferred_element_type=F32)


def _rms_rows(x):
    return lax.rsqrt(jnp.mean(x * x, axis=-1, keepdims=True) + EPS)


def _const_spec(a):
    nd = a.ndim
    return pl.BlockSpec(a.shape, lambda *_: (0,) * nd)


def _params(n_grid):
    return pltpu.CompilerParams(dimension_semantics=("arbitrary",) * n_grid,
                                vmem_limit_bytes=VMEM_LIMIT)


def _fold_lanes(xs, op):
    r = None
    for x in xs:
        for lo in range(0, x.shape[1], LANES):
            blk = x[:, lo:lo + LANES]
            r = blk if r is None else op(r, blk)
    return r


def _scores(q, keys, biases, s_ref, mx_ref, keys_t=False):
    parts = []
    lo = 0
    for k, bias in zip(keys, biases):
        sc = _dot(q, k) if keys_t else _dot_nt(q, k)
        if bias is not None:
            sc = sc + bias
        s_ref[:, lo:lo + sc.shape[1]] = sc
        lo += sc.shape[1]
        parts.append(sc)
    mx_ref[...] = _fold_lanes(parts, jnp.maximum)


def _softmax_pv(s_ref, mx_ref, vals, extra=None):
    m = jnp.max(mx_ref[...], axis=-1, keepdims=True)
    if extra is not None:
        m = jnp.maximum(m, extra)
    ps = []
    lo = 0
    for v in vals:
        ps.append(jnp.exp2(s_ref[:, lo:lo + v.shape[0]] - m))
        lo += v.shape[0]
    l = jnp.sum(_fold_lanes(ps, jnp.add), axis=-1, keepdims=True)
    if extra is not None:
        l = l + jnp.exp2(extra - m)
    acc = _dot(ps[0].astype(BF16), vals[0])
    for p, v in zip(ps[1:], vals[1:]):
        acc = acc + _dot(p.astype(BF16), v)
    return acc, l


def _run_skewed(units):
    units[0][0](0)
    for i, (_, finish) in enumerate(units):
        if i + 1 < len(units):
            units[i + 1][0]((i + 1) % 2)
        finish(i % 2)


def _row0(i, size):
    return i * size if isinstance(i, int) else pl.multiple_of(i * size, size)


def _skewed_loop(n, scores, finish, per_trip=8):
    assert n % per_trip == 0 and per_trip % 2 == 0
    scores(0, 0)

    def body(j, carry):
        for d in range(per_trip):
            i = per_trip * j + d
            scores(jnp.minimum(i + 1, n - 1), (d + 1) % 2)
            finish(i, d % 2)
        return carry

    lax.fori_loop(0, n // per_trip, body, 0)


def _att_scratch(lead, rows, n_keys):
    lead = (2,) + tuple(lead)
    return [pltpu.VMEM(lead + (rows, n_keys), F32), pltpu.VMEM(lead + (rows, LANES), F32)]


def _stack_halves(q):
    lane = lax.broadcasted_iota(jnp.int32, q.shape, 1)
    zero = jnp.zeros_like(q)
    return jnp.concatenate([jnp.where(lane < 64, q, zero), jnp.where(lane >= 64, q, zero)], axis=0)


def _merge_halves(lo, hi):
    lane = lax.broadcasted_iota(jnp.int32, lo.shape, 1)
    return jnp.where(lane < 64, lo, hi)


def _ada_kernel(c_ref, w_ref, b_ref, o_ref):
    c = c_ref[...]
    a = (c * (1.0 / (1.0 + jnp.exp(-c)))).astype(BF16)
    o_ref[...] = _dot(a, w_ref[...].astype(BF16)) + b_ref[...]


def _ada(cond, w, b):
    n = w.shape[1]
    tn = 1024
    return pl.pallas_call(
        _ada_kernel,
        grid=(n // tn,),
        in_specs=[pl.BlockSpec(cond.shape, lambda j: (0, 0)),
                  pl.BlockSpec((w.shape[0], tn), lambda j: (0, j)),
                  pl.BlockSpec((1, tn), lambda j: (0, j))],
        out_specs=pl.BlockSpec((cond.shape[0], tn), lambda j: (0, j)),
        out_shape=jax.ShapeDtypeStruct((cond.shape[0], n), F32),
        compiler_params=_params(1),
        name="ada_modulation",
    )(cond, w, b.reshape(1, n))


def _batch_tile(bsz):
    return 2 if bsz % 2 == 0 else 1


def _tok_spec(bt, width):
    return pl.BlockSpec((bt, TOK_TILE, width), lambda t, b: (b, t, 0))


def _lat_spec(bt):
    return pl.BlockSpec((bt, TOK_TILE, D_MODEL), lambda t, b: (b, jnp.minimum(t, N_LAT_TILES - 1), 0))


def _ctx_spec(bt):
    return pl.BlockSpec((bt, TOK_TILE, D_MODEL), lambda t, b: (b, 0, 0))


def _mod_spec(bt):
    return pl.BlockSpec((bt, 1, 6, D_MODEL), lambda t, b: (b, t // N_LAT_TILES, 0, 0))


def _table_spec(width):
    return pl.BlockSpec((TOK_TILE, width), lambda t, b: (t, 0))


def _read_tokens(x_ref, c_ref):
    return jnp.where(pl.program_id(0) < N_LAT_TILES, x_ref[...], c_ref[...])


def _mod_row(mod_ref, i):
    return mod_ref[:, 0, i:i + 1, :]


def _rows_dot(a, w):
    bt, rows, k = a.shape
    return _dot(a.reshape(bt * rows, k), w).reshape(bt, rows, w.shape[1])


W0_CQ, W0_CKV, W0_KR, W0_KRP, W0_QN, W0_KN, W0_VN, W0_END = 0, 256, 384, 512, 640, 1152, 1664, 2176


def _head_inv(z, n):
    return lax.rsqrt(jnp.sum(z * z, axis=-1, keepdims=True) * (1.0 / n) + EPS)


def _half_inv(z):
    z2 = z * z
    lane = lax.broadcasted_iota(jnp.int32, z.shape, z.ndim - 1)
    lo = jnp.sum(jnp.where(lane < 64, z2, 0.0), axis=-1, keepdims=True)
    hi = jnp.sum(jnp.where(lane < 64, 0.0, z2), axis=-1, keepdims=True)
    return jnp.where(lane < 64, lax.rsqrt(lo * (1.0 / 64.0) + EPS), lax.rsqrt(hi * (1.0 / 64.0) + EPS))


def _proj0_kernel(x_ref, c_ref, mod_ref, g1_ref, w0_ref, qag_ref, wuq_ref, kvag_ref, wukv_ref,
                  aq_ref, bq_ref, ak_ref, bk_ref, gqb_ref, gkb_ref,
                  qa_ref, ka_ref, va_ref, qb_ref, kb_ref, vb_ref):
    x = _read_tokens(x_ref, c_ref)
    h = x * _rms_rows(x) * g1_ref[...] * (1.0 + _mod_row(mod_ref, 1)) + _mod_row(mod_ref, 0)
    z = _rows_dot(h.astype(BF16), w0_ref[...])

    cq = z[:, :, W0_CQ:W0_CKV]
    cqn = (cq * _rms_rows(cq) * qag_ref[...]).astype(BF16)
    zq = _rows_dot(cqn, wuq_ref[...])
    aq, bq = aq_ref[...], bq_ref[...]
    for hd in range(MLA_HEADS):
        qm = zq[:, :, hd * LANES:(hd + 1) * LANES]
        qr = zq[:, :, 1024 + hd * LANES:1024 + (hd + 1) * LANES]
        qa_ref[:, :, hd * LANES:(hd + 1) * LANES] = ((qm * aq + qr * bq) * _head_inv(qm, MLA_QK)).astype(BF16)

    ckv = z[:, :, W0_CKV:W0_KR]
    ckvn = (ckv * _rms_rows(ckv) * kvag_ref[...]).astype(BF16)
    zkv = _rows_dot(ckvn, wukv_ref[...])
    kr = z[:, :, W0_KR:W0_KRP]
    kr_rot = z[:, :, W0_KRP:W0_QN] * bk_ref[...]
    ak = ak_ref[...]
    for hd in range(MLA_HEADS):
        km = zkv[:, :, hd * LANES:(hd + 1) * LANES] + kr
        ka_ref[:, :, hd * LANES:(hd + 1) * LANES] = ((km * ak + kr_rot) * _head_inv(km, MLA_QK)).astype(BF16)
    va_ref[...] = zkv[:, :, 1024:].astype(BF16)

    for j in range(NA_DIM // LANES):
        cols = slice(j * LANES, (j + 1) * LANES)
        qn = z[:, :, W0_QN + j * LANES:W0_QN + (j + 1) * LANES]
        kn = z[:, :, W0_KN + j * LANES:W0_KN + (j + 1) * LANES]
        qb_ref[:, :, cols] = (qn * gqb_ref[...] * _half_inv(qn)).astype(BF16)
        kb_ref[:, :, cols] = (kn * gkb_ref[...] * _half_inv(kn)).astype(BF16)
    vb_ref[...] = z[:, :, W0_VN:W0_END].astype(BF16)


def _proj0(x, ctx, mod, p):
    bsz = x.shape[0]
    bt = _batch_tile(bsz)
    consts1 = [p['g1'], p['w0'], p['qag'], p['wuq'], p['kvag'], p['wukv']]
    tables = [p['aq'], p['bq'], p['ak'], p['bk']]
    consts2 = [p['gqb'], p['gkb']]
    widths = [1024, 1024, 512, 512, 512, 512]
    return pl.pallas_call(
        _proj0_kernel,
        grid=(N_TILES, bsz // bt),
        in_specs=([_lat_spec(bt), _ctx_spec(bt), _mod_spec(bt)] + [_const_spec(a) for a in consts1]
                  + [_table_spec(LANES) for _ in tables] + [_const_spec(a) for a in consts2]),
        out_specs=[_tok_spec(bt, w) for w in widths],
        out_shape=[jax.ShapeDtypeStruct((bsz, N_TOK, w), BF16) for w in widths],
        compiler_params=_params(2),
        name="proj_layer0",
    )(x, ctx, mod, *consts1, *tables, *consts2)


N1_SEG = (4, 1, 4, 4)
N1_NORM = 1664
N1_MAIN = 2304


def _proj1_kernel(x_ref, mod_ref, g1_ref, w1_ref, a_ref, b_ref,
                  qc_ref, kc_ref, vc_ref, qd_ref, kd_ref, vd_ref):
    x = x_ref[...]
    h = x * _rms_rows(x) * g1_ref[...] * (1.0 + _mod_row(mod_ref, 1)) + _mod_row(mod_ref, 0)
    z = _rows_dot(h.astype(BF16), w1_ref[...])
    outs = (qc_ref, kc_ref, qd_ref, kd_ref)
    blk = 0
    for seg, n_blk in enumerate(N1_SEG):
        a = a_ref[:, seg * LANES:(seg + 1) * LANES]
        b = b_ref[:, seg * LANES:(seg + 1) * LANES]
        for j in range(n_blk):
            zm = z[:, :, blk * LANES:(blk + 1) * LANES]
            rot = z[:, :, N1_MAIN + blk * LANES:N1_MAIN + (blk + 1) * LANES]
            outs[seg][:, :, j * LANES:(j + 1) * LANES] = ((zm * a + rot * b) * _half_inv(zm)).astype(BF16)
            blk += 1
    vc_ref[...] = z[:, :, 1664:1792].astype(BF16)
    vd_ref[...] = z[:, :, 1792:2304].astype(BF16)


def _proj1(xa, mod, p):
    bsz = xa.shape[0]
    bt = _batch_tile(bsz)
    consts = [p['g1'], p['w1']]
    widths = [512, 128, 128, 512, 512, 512]
    return pl.pallas_call(
        _proj1_kernel,
        grid=(N_TILES, bsz // bt),
        in_specs=([_tok_spec(bt, D_MODEL), _mod_spec(bt)] + [_const_spec(a) for a in consts]
                  + [_table_spec(4 * LANES), _table_spec(4 * LANES)]),
        out_specs=[_tok_spec(bt, w) for w in widths],
        out_shape=[jax.ShapeDtypeStruct((bsz, N_TOK, w), BF16) for w in widths],
        compiler_params=_params(2),
        name="proj_layer1",
    )(xa, mod, *consts, p['a'], p['b'])


def _mla_kernel(q_ref, k_ref, v_ref, o_ref, s_scr, mx_scr, kt_scr):
    for hh in range(2):
        kt_scr[hh] = k_ref[0, :, hh * LANES:(hh + 1) * LANES].T

    def scores(r0, slot, key_lo):
        for hh in range(2):
            cols = slice(hh * LANES, (hh + 1) * LANES)
            rows = slice(hh * MLA_TQ, (hh + 1) * MLA_TQ)
            _scores(q_ref[0, r0:r0 + MLA_TQ, cols], [kt_scr[hh, :, key_lo:]], [None],
                    s_scr.at[slot, rows], mx_scr.at[slot, rows], keys_t=True)

    def finish(r0, slot, key_lo):
        acc, l = _softmax_pv(s_scr.at[slot], mx_scr.at[slot], [v_ref[0, key_lo:, :]])
        r = acc / l
        o_ref[0, r0:r0 + MLA_TQ, :] = _merge_halves(r[:MLA_TQ], r[MLA_TQ:]).astype(BF16)

    units = [(functools.partial(scores, r0, key_lo=0), functools.partial(finish, r0, key_lo=0))
             for r0 in range(0, SEQ, MLA_TQ)]
    units.append((functools.partial(scores, SEQ, key_lo=SEQ),
                  functools.partial(finish, SEQ, key_lo=SEQ)))
    _run_skewed(units)


def _mla(qa, ka, va):
    bsz = qa.shape[0]
    assert CTX_LEN == MLA_TQ
    return pl.pallas_call(
        _mla_kernel,
        grid=(bsz, MLA_HEADS // 2),
        in_specs=[pl.BlockSpec((1, N_TOK, 2 * LANES), lambda b, j: (b, 0, j)),
                  pl.BlockSpec((1, N_TOK, 2 * LANES), lambda b, j: (b, 0, j)),
                  pl.BlockSpec((1, N_TOK, LANES), lambda b, j: (b, 0, j))],
        out_specs=pl.BlockSpec((1, N_TOK, LANES), lambda b, j: (b, 0, j)),
        out_shape=jax.ShapeDtypeStruct((bsz, N_TOK, MLA_HEADS * MLA_V), BF16),
        scratch_shapes=_att_scratch((), 2 * MLA_TQ, N_TOK) + [pltpu.VMEM((2, LANES, N_TOK), BF16)],
        compiler_params=_params(2),
        name="mla_attention",
    )(qa, ka, va)


def _na_kernel(q_ref, k_ref, v_ref, t_ref, o_ref, s_scr, mx_scr):
    n_pairs = NA_HEADS // 2

    def window_start(u):
        w0 = jnp.minimum(jnp.clip(NA_Q_ROWS * u - NA_ROWS_MAX // 2, 0, GRID_ROWS - NA_ROWS_MAX),
                         GRID_ROWS - NA_WIN_ROWS)
        return pl.multiple_of(w0 * GRID_W, GRID_W)

    def scores(u, slot):
        ws = window_start(u)
        r0 = _row0(u, NA_TQ)
        var = jnp.minimum(u, 2) + jnp.maximum(u - (NA_STEPS - 3), 0)
        for j in range(n_pairs):
            cols = slice(j * LANES, (j + 1) * LANES)
            qq = _stack_halves(q_ref[0, pl.ds(r0, NA_TQ), cols])
            _scores(qq, [k_ref[0, pl.ds(ws, NA_WIN), cols], k_ref[0, SEQ:, cols]], [t_ref[var, j], None],
                    s_scr.at[slot, j], mx_scr.at[slot, j])

    def finish(u, slot):
        ws = window_start(u)
        r0 = _row0(u, NA_TQ)
        for j in range(n_pairs):
            cols = slice(j * LANES, (j + 1) * LANES)
            acc, l = _softmax_pv(s_scr.at[slot, j], mx_scr.at[slot, j],
                                 [v_ref[0, pl.ds(ws, NA_WIN), cols], v_ref[0, SEQ:, cols]])
            r = acc / l
            o_ref[0, pl.ds(r0, NA_TQ), cols] = _merge_halves(r[:NA_TQ], r[NA_TQ:]).astype(BF16)

    _skewed_loop(NA_STEPS, scores, finish)

    def ctx_scores(r0, slot):
        for j in range(n_pairs):
            cols = slice(j * LANES, (j + 1) * LANES)
            qq = _stack_halves(q_ref[0, r0:r0 + NA_TQ, cols])
            _scores(qq, [k_ref[0, SEQ:, cols]], [None], s_scr.at[slot, j], mx_scr.at[slot, j])

    def ctx_finish(r0, slot):
        for j in range(n_pairs):
            cols = slice(j * LANES, (j + 1) * LANES)
            acc, l = _softmax_pv(s_scr.at[slot, j], mx_scr.at[slot, j], [v_ref[0, SEQ:, cols]])
            r = acc / l
            o_ref[0, r0:r0 + NA_TQ, cols] = _merge_halves(r[:NA_TQ], r[NA_TQ:]).astype(BF16)

    _run_skewed([(functools.partial(ctx_scores, r0), functools.partial(ctx_finish, r0))
                 for r0 in range(SEQ, N_TOK, NA_TQ)])


def _na(qb, kb, vb, table):
    bsz = qb.shape[0]
    tok = pl.BlockSpec((1, N_TOK, NA_DIM), lambda b: (b, 0, 0))
    n_pairs = NA_HEADS // 2
    return pl.pallas_call(
        _na_kernel,
        grid=(bsz,),
        in_specs=[tok, tok, tok, _const_spec(table)],
        out_specs=tok,
        out_shape=jax.ShapeDtypeStruct((bsz, N_TOK, NA_DIM), BF16),
        scratch_shapes=_att_scratch((n_pairs,), 2 * NA_TQ, NA_WIN + CTX_LEN),
        compiler_params=_params(1),
        name="neighbourhood_attention",
    )(qb, kb, vb, table)


def _gqa_kernel(sink_ref, q_ref, k_ref, v_ref, o_ref, s_scr, mx_scr):
    row = lax.broadcasted_iota(jnp.int32, (2 * GQA_TQ, GQA_BAND), 0)
    col = lax.broadcasted_iota(jnp.int32, (2 * GQA_TQ, GQA_BAND), 1)
    rel = jnp.where(row >= GQA_TQ, row - GQA_TQ, row) - col
    hi_rows = lax.broadcasted_iota(jnp.int32, (2 * GQA_TQ, 1), 0) >= GQA_TQ

    def band_start(r0):
        return min(max(r0 - GQA_WINDOW, 0), SEQ - GQA_BAND)

    def scores(n, slot):
        r0 = n * GQA_TQ
        start = band_start(r0)
        mask = jnp.where(jnp.abs(rel + (r0 - start)) <= GQA_WINDOW, 0.0, NEG_BIG)
        keys = [k_ref[0, start:start + GQA_BAND, :], k_ref[0, SEQ:, :]]
        for j in range(GQA_HEADS // 2):
            qq = _stack_halves(q_ref[0, r0:r0 + GQA_TQ, j * LANES:(j + 1) * LANES])
            _scores(qq, keys, [mask, None], s_scr.at[slot, j], mx_scr.at[slot, j])

    def finish(n, slot):
        r0 = n * GQA_TQ
        start = band_start(r0)
        vals = [v_ref[0, start:start + GQA_BAND, :], v_ref[0, SEQ:, :]]
        for j in range(GQA_HEADS // 2):
            sink = jnp.where(hi_rows, sink_ref[j + GQA_HEADS // 2], sink_ref[j]) * LOG2E
            acc, l = _softmax_pv(s_scr.at[slot, j], mx_scr.at[slot, j], vals, extra=sink)
            r = acc / l
            o_ref[0, r0:r0 + GQA_TQ, j * LANES:(j + 1) * LANES] = _merge_halves(r[:GQA_TQ], r[GQA_TQ:]).astype(BF16)

    _run_skewed([(functools.partial(scores, n), functools.partial(finish, n)) for n in range(SEQ // GQA_TQ)])


def _gqa(sink, qc, kc, vc):
    bsz = qc.shape[0]
    return pl.pallas_call(
        _gqa_kernel,
        grid=(bsz,),
        in_specs=[pl.BlockSpec(memory_space=pltpu.SMEM),
                  pl.BlockSpec((1, SEQ, 512), lambda b: (b, 0, 0)),
                  pl.BlockSpec((1, N_TOK, LANES), lambda b: (b, 0, 0)),
                  pl.BlockSpec((1, N_TOK, LANES), lambda b: (b, 0, 0))],
        out_specs=pl.BlockSpec((1, SEQ, 512), lambda b: (b, 0, 0)),
        out_shape=jax.ShapeDtypeStruct((bsz, SEQ, 512), BF16),
        scratch_shapes=_att_scratch((GQA_HEADS // 2,), 2 * GQA_TQ, GQA_BAND + CTX_LEN),
        compiler_params=_params(1),
        name="windowed_gqa",
    )(sink, qc, kc, vc)


def _diff_kernel(lv_ref, q_ref, k_ref, v_ref, g_ref, o_ref, s_scr, mx_scr, kt_scr, *, lam_init):
    kt_scr[...] = k_ref[0].T
    lv = lv_ref[...]
    lam = (jnp.exp(jnp.sum(lv[0:1] * lv[1:2], axis=-1, keepdims=True))
           - jnp.exp(jnp.sum(lv[2:3] * lv[3:4], axis=-1, keepdims=True)) + lam_init)
    hi_rows = lax.broadcasted_iota(jnp.int32, (2 * ATT_TQ, 1), 0) >= ATT_TQ
    coef = jnp.where(hi_rows, -lam, 1.0)
    g = g_ref[...] * (1.0 - lam_init)

    def scores(i, slot):
        qq = _stack_halves(q_ref[0, i * ATT_TQ:(i + 1) * ATT_TQ, :])
        _scores(qq, [kt_scr[...]], [None], s_scr.at[slot], mx_scr.at[slot], keys_t=True)

    def finish(i, slot):
        acc, l = _softmax_pv(s_scr.at[slot], mx_scr.at[slot], [v_ref[0]])
        r = acc * (coef / l)
        o = r[:ATT_TQ] + r[ATT_TQ:]
        o_ref[0, i * ATT_TQ:(i + 1) * ATT_TQ, :] = (o * _rms_rows(o) * g).astype(BF16)

    _run_skewed([(functools.partial(scores, i), functools.partial(finish, i)) for i in range(SEQ // ATT_TQ)])


def _diff(lv, qd, kd, vd, g, lam_init):
    bsz = qd.shape[0]
    return pl.pallas_call(
        functools.partial(_diff_kernel, lam_init=lam_init),
        grid=(bsz, DIFF_HEADS),
        in_specs=[_const_spec(lv),
                  pl.BlockSpec((1, SEQ, LANES), lambda b, h: (b, 0, h)),
                  pl.BlockSpec((1, N_TOK, LANES), lambda b, h: (b, 0, h)),
                  pl.BlockSpec((1, N_TOK, LANES), lambda b, h: (b, 0, h)),
                  _const_spec(g)],
        out_specs=pl.BlockSpec((1, SEQ, LANES), lambda b, h: (b, 0, h)),
        out_shape=jax.ShapeDtypeStruct((bsz, SEQ, DIFF_HEADS * 2 * DIFF_HEAD_DIM), BF16),
        scratch_shapes=_att_scratch((), 2 * ATT_TQ, N_TOK) + [pltpu.VMEM((LANES, N_TOK), BF16)],
        compiler_params=_params(2),
        name="diff_attention",
    )(lv, qd, kd, vd, g)


def _ffn_body(x, o1_ref, o2_ref, mod_ref, g2_ref, wo1_ref, wo2_ref, wg_ref, wu_ref, wd_ref, y_ref):
    attn = _rows_dot(o1_ref[...], wo1_ref[...]) + _rows_dot(o2_ref[...], wo2_ref[...])
    x1 = x + _mod_row(mod_ref, 2) * attn
    h = (x1 * _rms_rows(x1) * g2_ref[...] * (1.0 + _mod_row(mod_ref, 4)) + _mod_row(mod_ref, 3)).astype(BF16)
    gate = _rows_dot(h, wg_ref[...])
    up = _rows_dot(h, wu_ref[...])
    a = (gate * (1.0 / (1.0 + jnp.exp(-gate))) * up).astype(BF16)
    y_ref[...] = x1 + _mod_row(mod_ref, 5) * _rows_dot(a, wd_ref[...])


def _ffn0_kernel(x_ref, c_ref, *rest):
    _ffn_body(_read_tokens(x_ref, c_ref), *rest)


def _ffn1_kernel(x_ref, *rest):
    _ffn_body(x_ref[...], *rest)


def _ffn(xs, o1, o2, mod, p, n_tiles):
    bsz = xs[0].shape[0]
    bt = _batch_tile(bsz)
    consts = [p['g2'], p['wo1'], p['wo2'], p['wg'], p['wu'], p['wd']]
    x_specs = [_lat_spec(bt), _ctx_spec(bt)] if len(xs) == 2 else [_tok_spec(bt, D_MODEL)]
    return pl.pallas_call(
        _ffn0_kernel if len(xs) == 2 else _ffn1_kernel,
        grid=(n_tiles, bsz // bt),
        in_specs=(x_specs + [_tok_spec(bt, 512), _tok_spec(bt, 512), _mod_spec(bt)] + [_const_spec(a) for a in consts]),
        out_specs=_tok_spec(bt, D_MODEL),
        out_shape=jax.ShapeDtypeStruct((bsz, n_tiles * TOK_TILE, D_MODEL), F32),
        compiler_params=_params(2),
        name="outproj_ffn",
    )(*xs, o1, o2, mod, *consts)


def _partner_cols(w, dr):
    q = dr // 4
    shp = w.shape
    return jnp.flip(w.reshape(shp[:-1] + (shp[-1] // dr, 2, 2, q)), axis=-2).reshape(shp)


def _rope_tables(dr):
    h = dr // 2
    q = dr // 4
    freqs = ROPE_THETA ** (-jnp.arange(0, h, 2, dtype=F32) / h)
    t = jnp.arange(SEQ)
    pos = jnp.stack([t // GRID_W, t % GRID_W], axis=1).astype(F32)
    ang = (pos[:, :, None, None] * freqs[None, None, None, :])
    ang = jnp.broadcast_to(ang, (SEQ, 2, 2, q)).reshape(SEQ, dr)
    sign = np.tile(np.repeat(np.array([-1.0, 1.0], np.float32), q), 2)
    cos = jnp.concatenate([jnp.cos(ang), jnp.ones((CTX_LEN, dr), F32)], axis=0)
    sin = jnp.concatenate([jnp.sin(ang) * sign, jnp.zeros((CTX_LEN, dr), F32)], axis=0)
    return cos, sin


def _layer0_params(w_in, qa_g, w_uq, kva_g, w_ukv, qn_g, kn_g, na_qn_g, na_kn_g, norm1_g):
    cos, sin = _rope_tables(MLA_ROPE)
    o_kr = MLA_Q_RANK + MLA_KV_RANK
    kr_w = w_in[:, o_kr:o_kr + MLA_ROPE]
    lane_pad = ((0, 0), (MLA_NOPE, LANES - MLA_QK))
    w0 = jnp.concatenate([w_in[:, :o_kr], jnp.pad(kr_w, lane_pad), jnp.pad(_partner_cols(kr_w, MLA_ROPE), lane_pad),
                          w_in[:, o_kr + MLA_ROPE:]], axis=1).astype(BF16)
    r = w_uq.shape[0]
    w3 = w_uq.reshape(r, MLA_HEADS, MLA_QK)
    main = jnp.pad(w3, ((0, 0), (0, 0), (0, LANES - MLA_QK)))
    rot = jnp.pad(_partner_cols(w3[:, :, MLA_NOPE:], MLA_ROPE), ((0, 0), (0, 0), (MLA_NOPE, LANES - MLA_QK)))
    wuq = jnp.concatenate([main.reshape(r, -1), rot.reshape(r, -1)], axis=1).astype(BF16)

    def tables(g, c):
        z32 = jnp.zeros((N_TOK, LANES - MLA_QK), F32)
        a = jnp.concatenate([jnp.broadcast_to(g[None, :MLA_NOPE], (N_TOK, MLA_NOPE)), g[None, MLA_NOPE:] * cos, z32], axis=1)
        b = jnp.concatenate([jnp.zeros((N_TOK, MLA_NOPE), F32), _partner_cols(g[None, MLA_NOPE:], MLA_ROPE) * sin, z32], axis=1)
        return a * c, b * c

    aq, bq = tables(qn_g, MLA_QK ** -0.5 * LOG2E)
    ak, bk = tables(kn_g, 1.0)
    rk = w_ukv.shape[0]
    k3 = w_ukv.reshape(rk, MLA_HEADS, MLA_NOPE + MLA_V)
    wukv = jnp.concatenate([jnp.pad(k3[:, :, :MLA_NOPE], ((0, 0), (0, 0), (0, LANES - MLA_NOPE))).reshape(rk, -1),
                            k3[:, :, MLA_NOPE:].reshape(rk, -1)], axis=1).astype(BF16)
    return dict(
        g1=norm1_g.reshape(1, -1), w0=w0, qag=qa_g.reshape(1, -1), wuq=wuq, kvag=kva_g.reshape(1, -1),
        wukv=wukv, aq=aq, bq=bq, ak=ak, bk=bk,
        gqb=(jnp.tile(na_qn_g, 2) * (NA_HEAD_DIM ** -0.5 * LOG2E)).reshape(1, -1),
        gkb=jnp.tile(na_kn_g, 2).reshape(1, -1))


def _pair_heads(w, axis):
    g = GQA_HEADS // GQA_KV_HEADS
    shp = w.shape
    w4 = w.reshape(shp[:axis] + (GQA_KV_HEADS, g, GQA_HEAD_DIM) + shp[axis + 1:])
    return jnp.swapaxes(w4, axis, axis + 1).reshape(shp)


def _layer1_params(w_in, gqa_qn_g, gqa_kn_g, diff_qn_g, diff_kn_g, norm1_g):
    cos, sin = _rope_tables(64)
    main = jnp.concatenate([_pair_heads(w_in[:, :512], 1), w_in[:, 512:640], w_in[:, 768:1792]], axis=1)
    w1 = jnp.concatenate([main, w_in[:, 640:768], w_in[:, 1792:], _partner_cols(main, 64)], axis=1).astype(BF16)
    a_parts, b_parts = [], []
    for g, c in ((gqa_qn_g, GQA_HEAD_DIM ** -0.5 * LOG2E), (gqa_kn_g, 1.0),
                 (diff_qn_g, DIFF_HEAD_DIM ** -0.5 * LOG2E), (diff_kn_g, 1.0)):
        a_parts.append(jnp.tile(g[None, :] * cos * c, (1, 2)))
        b_parts.append(jnp.tile(_partner_cols(g[None, :], 64) * sin * c, (1, 2)))
    return dict(g1=norm1_g.reshape(1, -1), w1=w1,
                a=jnp.concatenate(a_parts, axis=1), b=jnp.concatenate(b_parts, axis=1))


def _na_table_kernel(w_ref, o_ref):
    row = lax.broadcasted_iota(jnp.int32, (GRID_W, LANES), 0)
    lane = lax.broadcasted_iota(jnp.int32, (GRID_W, LANES), 1)
    kc = jnp.where(lane < GRID_W, lane, lane - GRID_W)
    c0 = jnp.clip(row - NA_COLS // 2, 0, GRID_W - NA_COLS)
    band = (kc >= c0) & (kc < c0 + NA_COLS)
    shift = LANES - (GRID_W - 1)
    for hf in range(2):
        for a in range(NA_Q_ROWS):
            for jp in range(NA_WIN_ROWS // 2):
                n0 = a * NA_WIN_ROWS + 2 * jp
                lo = jnp.broadcast_to(w_ref[0, hf, 0, n0:n0 + 1, :], (GRID_W, LANES))
                hi = jnp.broadcast_to(w_ref[0, hf, 0, n0 + 1:n0 + 2, :], (GRID_W, LANES))
                lo = pltpu.roll(lo, shift, 1, stride=1, stride_axis=0)
                hi = pltpu.roll(hi, (shift + GRID_W) % LANES, 1, stride=1, stride_axis=0)
                r0 = (hf * NA_Q_ROWS + a) * GRID_W
                o_ref[0, 0, r0:r0 + GRID_W, jp * LANES:(jp + 1) * LANES] = jnp.where(
                    band, jnp.where(lane < GRID_W, lo, hi), NEG_BIG)


def _na_table(rpb):
    us = np.array(NA_VARIANT_UNITS)
    n_var = len(us)
    w0 = np.minimum(np.clip(NA_Q_ROWS * us - NA_ROWS_MAX // 2, 0, GRID_ROWS - NA_ROWS_MAX), GRID_ROWS - NA_WIN_ROWS)
    r = NA_Q_ROWS * us[:, None, None] + np.arange(NA_Q_ROWS)[None, :, None]
    key_r = w0[:, None, None] + np.arange(NA_WIN_ROWS)[None, None, :]
    r0 = np.clip(r - NA_ROWS_MAX // 2, 0, GRID_ROWS - NA_ROWS_MAX)
    row_valid = (key_r >= r0) & (key_r < r0 + NA_ROWS_MAX)
    off_r = np.clip(key_r - r + (NA_ROWS_MAX - 1), 0, 2 * NA_ROWS_MAX - 2)
    pick = (off_r.reshape(-1, 1) == np.arange(2 * NA_ROWS_MAX - 1)[None, :]).astype(np.float32)
    rows = jnp.einsum('nr,hrd->hnd', pick, rpb, precision=lax.Precision.HIGHEST) * LOG2E
    pad = GRID_W - NA_COLS
    w = jnp.pad(rows, ((0, 0), (0, 0), (pad, LANES - pad - (2 * NA_COLS - 1))))
    w = jnp.where(jnp.asarray(row_valid.reshape(1, -1, 1)), w, NEG_BIG)
    w = w.reshape(NA_HEADS // 2, 2, n_var, NA_Q_ROWS * NA_WIN_ROWS, LANES)
    return pl.pallas_call(
        _na_table_kernel,
        grid=(n_var, NA_HEADS // 2),
        in_specs=[pl.BlockSpec((1, 2, 1, NA_Q_ROWS * NA_WIN_ROWS, LANES), lambda v, p: (p, 0, v, 0, 0))],
        out_specs=pl.BlockSpec((1, 1, 2 * NA_TQ, NA_WIN), lambda v, p: (v, p, 0, 0)),
        out_shape=jax.ShapeDtypeStruct((n_var, NA_HEADS // 2, 2 * NA_TQ, NA_WIN), F32),
        compiler_params=_params(2),
        name="na_bias_table",
    )(w)


def _ffn_params(norm2_g, w_out, w_gate, w_up, w_down, pair_gqa=False):
    wo1 = _pair_heads(w_out[:512], 0) if pair_gqa else w_out[:512]
    return dict(g2=norm2_g.reshape(1, -1), wo1=wo1.astype(BF16), wo2=w_out[512:].astype(BF16),
                wg=w_gate.astype(BF16), wu=w_up.astype(BF16), wd=w_down.astype(BF16))


def _mod_rows(mod_all, bsz):
    lat = mod_all[:bsz].reshape(bsz, 1, 6, D_MODEL)
    cx = jnp.broadcast_to(mod_all[bsz].reshape(1, 1, 6, D_MODEL), (bsz, 1, 6, D_MODEL))
    return jnp.concatenate([lat, cx], axis=1)


def kernel(x, c, ctx, c_ctx, l0_ada_w, l0_ada_b, l0_norm1_g, l0_norm2_g, l0_w_in, l0_mla_qa_g, l0_mla_w_uq, l0_mla_kva_g, l0_mla_w_ukv, l0_mla_qn_g, l0_mla_kn_g, l0_na_qn_g, l0_na_kn_g, l0_na_rpb, l0_w_out, l0_ffn_w_gate, l0_ffn_w_up, l0_ffn_w_down, l1_ada_w, l1_ada_b, l1_norm1_g, l1_norm2_g, l1_w_in, l1_gqa_qn_g, l1_gqa_kn_g, l1_gqa_sink, l1_diff_qn_g, l1_diff_kn_g, l1_diff_lq1, l1_diff_lk1, l1_diff_lq2, l1_diff_lk2, l1_diff_subln_g, l1_w_out, l1_ffn_w_gate, l1_ffn_w_up, l1_ffn_w_down):
    bsz = x.shape[0]
    assert x.shape[1:] == (SEQ, D_MODEL) and ctx.shape[1:] == (CTX_LEN, D_MODEL)
    rows = -(-(bsz + 1) // 8) * 8
    cond = jnp.concatenate([c, c_ctx[None, :], jnp.zeros((rows - bsz - 1, D_MODEL), F32)], axis=0)
    mod0 = _mod_rows(_ada(cond, l0_ada_w, l0_ada_b), bsz)
    mod1 = _mod_rows(_ada(cond, l1_ada_w, l1_ada_b), bsz)

    p0 = _layer0_params(l0_w_in, l0_mla_qa_g, l0_mla_w_uq, l0_mla_kva_g, l0_mla_w_ukv, l0_mla_qn_g,
                        l0_mla_kn_g, l0_na_qn_g, l0_na_kn_g, l0_norm1_g)
    qa, ka, va, qb, kb, vb = _proj0(x, ctx, mod0, p0)
    o_a = _mla(qa, ka, va)
    o_b = _na(qb, kb, vb, _na_table(l0_na_rpb))
    f0 = _ffn_params(l0_norm2_g, l0_w_out, l0_ffn_w_gate, l0_ffn_w_up, l0_ffn_w_down)
    xa = _ffn((x, ctx), o_a, o_b, mod0, f0, N_TILES)

    p1 = _layer1_params(l1_w_in, l1_gqa_qn_g, l1_gqa_kn_g, l1_diff_qn_g, l1_diff_kn_g, l1_norm1_g)
    qc, kc, vc, qd, kd, vd = _proj1(xa, mod1, p1)
    o_c = _gqa(l1_gqa_sink, qc, kc, vc)
    lv = jnp.zeros((8, LANES), F32).at[:4, :DIFF_HEAD_DIM].set(
        jnp.stack([l1_diff_lq1, l1_diff_lk1, l1_diff_lq2, l1_diff_lk2]))
    lam_init = 0.8 - 0.6 * math.exp(-0.3 * 1)
    o_d = _diff(lv, qd, kd, vd, l1_diff_subln_g.reshape(1, -1), lam_init)
    f1 = _ffn_params(l1_norm2_g, l1_w_out, l1_ffn_w_gate, l1_ffn_w_up, l1_ffn_w_down, pair_gqa=True)
    return _ffn((xa,), o_c, o_d, mod1, f1, N_LAT_TILES)
```

```python
import functools
import math

import numpy as np
import jax
import jax.numpy as jnp
from jax import lax
from jax.experimental import pallas as pl
from jax.experimental.pallas import tpu as pltpu

F32 = jnp.float32
BF16 = jnp.bfloat16

D_MODEL = 1024
SEQ = 2048
GRID_W = 64
CTX_LEN = 256
N_TOK = SEQ + CTX_LEN
ROPE_THETA = 10000.0
EPS = 1e-6

MLA_HEADS = 8
MLA_Q_RANK = 256
MLA_KV_RANK = 128
MLA_NOPE = 64
MLA_ROPE = 32
MLA_V = 64
MLA_QK = MLA_NOPE + MLA_ROPE

NA_HEADS = 8
NA_HEAD_DIM = 64
NA_ROWS_MAX = 8
NA_COLS = 16
NA_DIM = NA_HEADS * NA_HEAD_DIM

GQA_HEADS = 8
GQA_KV_HEADS = 2
GQA_HEAD_DIM = 64
GQA_WINDOW = 128

DIFF_HEADS = 4
DIFF_HEAD_DIM = 64

FFN_HIDDEN = 2816

LANES = 128
TOK_TILE = 256
N_TILES = N_TOK // TOK_TILE
N_LAT_TILES = SEQ // TOK_TILE
VMEM_LIMIT = 56 * 1024 * 1024
LOG2E = 1.4426950408889634
NEG_BIG = -1e30

ATT_TQ = 256
MLA_TQ = 256
NA_Q_ROWS = 2
NA_TQ = NA_Q_ROWS * GRID_W
NA_WIN_ROWS = 10
NA_WIN = NA_WIN_ROWS * GRID_W
NA_STEPS = SEQ // NA_TQ
NA_VARIANT_UNITS = (0, 1, 2, NA_STEPS - 2, NA_STEPS - 1)
GRID_ROWS = SEQ // GRID_W
GQA_TQ = 128
GQA_BAND = GQA_TQ + 2 * GQA_WINDOW


def _dot(a, b):
    return jnp.dot(a, b, preferred_element_type=F32)


def _dot_nt(a, b):
    return lax.dot_general(a, b, (((1,), (1,)), ((), ())), preferred_element_type=F32)


def _rms_rows(x):
    return lax.rsqrt(jnp.mean(x * x, axis=-1, keepdims=True) + EPS)


def _const_spec(a):
    nd = a.ndim
    return pl.BlockSpec(a.shape, lambda *_: (0,) * nd)


def _params(n_grid):
    return pltpu.CompilerParams(dimension_semantics=("arbitrary",) * n_grid,
                                vmem_limit_bytes=VMEM_LIMIT)


def _fold_lanes(xs, op):
    r = None
    for x in xs:
        for lo in range(0, x.shape[1], LANES):
            blk = x[:, lo:lo + LANES]
            r = blk if r is None else op(r, blk)
    return r


def _scores(q, keys, biases, s_ref, mx_ref, keys_t=False):
    parts = []
    lo = 0
    for k, bias in zip(keys, biases):
        sc = _dot(q, k) if keys_t else _dot_nt(q, k)
        if bias is not None:
            sc = sc + bias
        s_ref[:, lo:lo + sc.shape[1]] = sc
        lo += sc.shape[1]
        parts.append(sc)
    mx_ref[...] = _fold_lanes(parts, jnp.maximum)


def _softmax_pv(s_ref, mx_ref, vals, extra=None):
    m = jnp.max(mx_ref[...], axis=-1, keepdims=True)
    if extra is not None:
        m = jnp.maximum(m, extra)
    ps = []
    lo = 0
    for v in vals:
        ps.append(jnp.exp2(s_ref[:, lo:lo + v.shape[0]] - m))
        lo += v.shape[0]
    l = jnp.sum(_fold_lanes(ps, jnp.add), axis=-1, keepdims=True)
    if extra is not None:
        l = l + jnp.exp2(extra - m)
    acc = _dot(ps[0].astype(BF16), vals[0])
    for p, v in zip(ps[1:], vals[1:]):
        acc = acc + _dot(p.astype(BF16), v)
    return acc, l


def _run_skewed(units):
    units[0][0](0)
    for i, (_, finish) in enumerate(units):
        if i + 1 < len(units):
            units[i + 1][0]((i + 1) % 2)
        finish(i % 2)


def _row0(i, size):
    return i * size if isinstance(i, int) else pl.multiple_of(i * size, size)


def _skewed_loop(n, scores, finish, per_trip=8):
    assert n % per_trip == 0 and per_trip % 2 == 0
    scores(0, 0)

    def body(j, carry):
        for d in range(per_trip):
            i = per_trip * j + d
            scores(jnp.minimum(i + 1, n - 1), (d + 1) % 2)
            finish(i, d % 2)
        return carry

    lax.fori_loop(0, n // per_trip, body, 0)


def _att_scratch(lead, rows, n_keys):
    lead = (2,) + tuple(lead)
    return [pltpu.VMEM(lead + (rows, n_keys), F32), pltpu.VMEM(lead + (rows, LANES), F32)]


def _stack_halves(q):
    lane = lax.broadcasted_iota(jnp.int32, q.shape, 1)
    zero = jnp.zeros_like(q)
    return jnp.concatenate([jnp.where(lane < 64, q, zero), jnp.where(lane >= 64, q, zero)], axis=0)


def _merge_halves(lo, hi):
    lane = lax.broadcasted_iota(jnp.int32, lo.shape, 1)
    return jnp.where(lane < 64, lo, hi)


def _ada_kernel(c_ref, w_ref, b_ref, o_ref):
    c = c_ref[...]
    a = (c * (1.0 / (1.0 + jnp.exp(-c)))).astype(BF16)
    o_ref[...] = _dot(a, w_ref[...].astype(BF16)) + b_ref[...]


def _ada(cond, w, b):
    n = w.shape[1]
    tn = 1024
    return pl.pallas_call(
        _ada_kernel,
        grid=(n // tn,),
        in_specs=[pl.BlockSpec(cond.shape, lambda j: (0, 0)),
                  pl.BlockSpec((w.shape[0], tn), lambda j: (0, j)),
                  pl.BlockSpec((1, tn), lambda j: (0, j))],
        out_specs=pl.BlockSpec((cond.shape[0], tn), lambda j: (0, j)),
        out_shape=jax.ShapeDtypeStruct((cond.shape[0], n), F32),
        compiler_params=_params(1),
        name="ada_modulation",
    )(cond, w, b.reshape(1, n))


def _batch_tile(bsz):
    return 2 if bsz % 2 == 0 else 1


def _tok_spec(bt, width):
    return pl.BlockSpec((bt, TOK_TILE, width), lambda t, b: (b, t, 0))


def _lat_spec(bt):
    return pl.BlockSpec((bt, TOK_TILE, D_MODEL), lambda t, b: (b, jnp.minimum(t, N_LAT_TILES - 1), 0))


def _ctx_spec(bt):
    return pl.BlockSpec((bt, TOK_TILE, D_MODEL), lambda t, b: (b, 0, 0))


def _mod_spec(bt):
    return pl.BlockSpec((bt, 1, 6, D_MODEL), lambda t, b: (b, t // N_LAT_TILES, 0, 0))


def _table_spec(width):
    return pl.BlockSpec((TOK_TILE, width), lambda t, b: (t, 0))


def _read_tokens(x_ref, c_ref):
    return jnp.where(pl.program_id(0) < N_LAT_TILES, x_ref[...], c_ref[...])


def _mod_row(mod_ref, i):
    return mod_ref[:, 0, i:i + 1, :]


def _rows_dot(a, w):
    bt, rows, k = a.shape
    return _dot(a.reshape(bt * rows, k), w).reshape(bt, rows, w.shape[1])


W0_CQ, W0_CKV, W0_KR, W0_KRP, W0_QN, W0_KN, W0_VN, W0_END = 0, 256, 384, 512, 640, 1152, 1664, 2176


def _head_inv(z, n):
    return lax.rsqrt(jnp.sum(z * z, axis=-1, keepdims=True) * (1.0 / n) + EPS)


def _half_inv(z):
    z2 = z * z
    lane = lax.broadcasted_iota(jnp.int32, z.shape, z.ndim - 1)
    lo = jnp.sum(jnp.where(lane < 64, z2, 0.0), axis=-1, keepdims=True)
    hi = jnp.sum(jnp.where(lane < 64, 0.0, z2), axis=-1, keepdims=True)
    return jnp.where(lane < 64, lax.rsqrt(lo * (1.0 / 64.0) + EPS), lax.rsqrt(hi * (1.0 / 64.0) + EPS))


def _proj0_kernel(x_ref, c_ref, mod_ref, g1_ref, w0_ref, qag_ref, wuq_ref, kvag_ref, wukv_ref,
                  aq_ref, bq_ref, ak_ref, bk_ref, gqb_ref, gkb_ref,
                  qa_ref, ka_ref, va_ref, qb_ref, kb_ref, vb_ref):
    x = _read_tokens(x_ref, c_ref)
    h = x * _rms_rows(x) * g1_ref[...] * (1.0 + _mod_row(mod_ref, 1)) + _mod_row(mod_ref, 0)
    z = _rows_dot(h.astype(BF16), w0_ref[...])

    cq = z[:, :, W0_CQ:W0_CKV]
    cqn = (cq * _rms_rows(cq) * qag_ref[...]).astype(BF16)
    zq = _rows_dot(cqn, wuq_ref[...])
    aq, bq = aq_ref[...], bq_ref[...]
    for hd in range(MLA_HEADS):
        qm = zq[:, :, hd * LANES:(hd + 1) * LANES]
        qr = zq[:, :, 1024 + hd * LANES:1024 + (hd + 1) * LANES]
        qa_ref[:, :, hd * LANES:(hd + 1) * LANES] = ((qm * aq + qr * bq) * _head_inv(qm, MLA_QK)).astype(BF16)

    ckv = z[:, :, W0_CKV:W0_KR]
    ckvn = (ckv * _rms_rows(ckv) * kvag_ref[...]).astype(BF16)
    zkv = _rows_dot(ckvn, wukv_ref[...])
    kr = z[:, :, W0_KR:W0_KRP]
    kr_rot = z[:, :, W0_KRP:W0_QN] * bk_ref[...]
    ak = ak_ref[...]
    for hd in range(MLA_HEADS):
        km = zkv[:, :, hd * LANES:(hd + 1) * LANES] + kr
        ka_ref[:, :, hd * LANES:(hd + 1) * LANES] = ((km * ak + kr_rot) * _head_inv(km, MLA_QK)).astype(BF16)
    va_ref[...] = zkv[:, :, 1024:].astype(BF16)

    for j in range(NA_DIM // LANES):
        cols = slice(j * LANES, (j + 1) * LANES)
        qn = z[:, :, W0_QN + j * LANES:W0_QN + (j + 1) * LANES]
        kn = z[:, :, W0_KN + j * LANES:W0_KN + (j + 1) * LANES]
        qb_ref[:, :, cols] = (qn * gqb_ref[...] * _half_inv(qn)).astype(BF16)
        kb_ref[:, :, cols] = (kn * gkb_ref[...] * _half_inv(kn)).astype(BF16)
    vb_ref[...] = z[:, :, W0_VN:W0_END].astype(BF16)


def _proj0(x, ctx, mod, p):
    bsz = x.shape[0]
    bt = _batch_tile(bsz)
    consts1 = [p['g1'], p['w0'], p['qag'], p['wuq'], p['kvag'], p['wukv']]
    tables = [p['aq'], p['bq'], p['ak'], p['bk']]
    consts2 = [p['gqb'], p['gkb']]
    widths = [1024, 1024, 512, 512, 512, 512]
    return pl.pallas_call(
        _proj0_kernel,
        grid=(N_TILES, bsz // bt),
        in_specs=([_lat_spec(bt), _ctx_spec(bt), _mod_spec(bt)] + [_const_spec(a) for a in consts1]
                  + [_table_spec(LANES) for _ in tables] + [_const_spec(a) for a in consts2]),
        out_specs=[_tok_spec(bt, w) for w in widths],
        out_shape=[jax.ShapeDtypeStruct((bsz, N_TOK, w), BF16) for w in widths],
        compiler_params=_params(2),
        name="proj_layer0",
    )(x, ctx, mod, *consts1, *tables, *consts2)


N1_SEG = (4, 1, 4, 4)
N1_NORM = 1664
N1_MAIN = 2304


def _proj1_kernel(x_ref, mod_ref, g1_ref, w1_ref, a_ref, b_ref,
                  qc_ref, kc_ref, vc_ref, qd_ref, kd_ref, vd_ref):
    x = x_ref[...]
    h = x * _rms_rows(x) * g1_ref[...] * (1.0 + _mod_row(mod_ref, 1)) + _mod_row(mod_ref, 0)
    z = _rows_dot(h.astype(BF16), w1_ref[...])
    outs = (qc_ref, kc_ref, qd_ref, kd_ref)
    blk = 0
    for seg, n_blk in enumerate(N1_SEG):
        a = a_ref[:, seg * LANES:(seg + 1) * LANES]
        b = b_ref[:, seg * LANES:(seg + 1) * LANES]
        for j in range(n_blk):
            zm = z[:, :, blk * LANES:(blk + 1) * LANES]
            rot = z[:, :, N1_MAIN + blk * LANES:N1_MAIN + (blk + 1) * LANES]
            outs[seg][:, :, j * LANES:(j + 1) * LANES] = ((zm * a + rot * b) * _half_inv(zm)).astype(BF16)
            blk += 1
    vc_ref[...] = z[:, :, 1664:1792].astype(BF16)
    vd_ref[...] = z[:, :, 1792:2304].astype(BF16)


def _proj1(xa, mod, p):
    bsz = xa.shape[0]
    bt = _batch_tile(bsz)
    consts = [p['g1'], p['w1']]
    widths = [512, 128, 128, 512, 512, 512]
    return pl.pallas_call(
        _proj1_kernel,
        grid=(N_TILES, bsz // bt),
        in_specs=([_tok_spec(bt, D_MODEL), _mod_spec(bt)] + [_const_spec(a) for a in consts]
                  + [_table_spec(4 * LANES), _table_spec(4 * LANES)]),
        out_specs=[_tok_spec(bt, w) for w in widths],
        out_shape=[jax.ShapeDtypeStruct((bsz, N_TOK, w), BF16) for w in widths],
        compiler_params=_params(2),
        name="proj_layer1",
    )(xa, mod, *consts, p['a'], p['b'])


def _mla_kernel(q_ref, k_ref, v_ref, o_ref, s_scr, mx_scr, kt_scr):
    for hh in range(2):
        kt_scr[hh] = k_ref[0, :, hh * LANES:(hh + 1) * LANES].T

    def scores(r0, slot, key_lo):
        for hh in range(2):
            cols = slice(hh * LANES, (hh + 1) * LANES)
            rows = slice(hh * MLA_TQ, (hh + 1) * MLA_TQ)
            _scores(q_ref[0, r0:r0 + MLA_TQ, cols], [kt_scr[hh, :, key_lo:]], [None],
                    s_scr.at[slot, rows], mx_scr.at[slot, rows], keys_t=True)

    def finish(r0, slot, key_lo):
        acc, l = _softmax_pv(s_scr.at[slot], mx_scr.at[slot], [v_ref[0, key_lo:, :]])
        r = acc / l
        o_ref[0, r0:r0 + MLA_TQ, :] = _merge_halves(r[:MLA_TQ], r[MLA_TQ:]).astype(BF16)

    units = [(functools.partial(scores, r0, key_lo=0), functools.partial(finish, r0, key_lo=0))
             for r0 in range(0, SEQ, MLA_TQ)]
    units.append((functools.partial(scores, SEQ, key_lo=SEQ),
                  functools.partial(finish, SEQ, key_lo=SEQ)))
    _run_skewed(units)


def _mla(qa, ka, va):
    bsz = qa.shape[0]
    assert CTX_LEN == MLA_TQ
    return pl.pallas_call(
        _mla_kernel,
        grid=(bsz, MLA_HEADS // 2),
        in_specs=[pl.BlockSpec((1, N_TOK, 2 * LANES), lambda b, j: (b, 0, j)),
                  pl.BlockSpec((1, N_TOK, 2 * LANES), lambda b, j: (b, 0, j)),
                  pl.BlockSpec((1, N_TOK, LANES), lambda b, j: (b, 0, j))],
        out_specs=pl.BlockSpec((1, N_TOK, LANES), lambda b, j: (b, 0, j)),
        out_shape=jax.ShapeDtypeStruct((bsz, N_TOK, MLA_HEADS * MLA_V), BF16),
        scratch_shapes=_att_scratch((), 2 * MLA_TQ, N_TOK) + [pltpu.VMEM((2, LANES, N_TOK), BF16)],
        compiler_params=_params(2),
        name="mla_attention",
    )(qa, ka, va)


def _na_kernel(q_ref, k_ref, v_ref, t_ref, o_ref, s_scr, mx_scr):
    n_pairs = NA_HEADS // 2

    def window_start(u):
        w0 = jnp.minimum(jnp.clip(NA_Q_ROWS * u - NA_ROWS_MAX // 2, 0, GRID_ROWS - NA_ROWS_MAX),
                         GRID_ROWS - NA_WIN_ROWS)
        return pl.multiple_of(w0 * GRID_W, GRID_W)

    def scores(u, slot):
        ws = window_start(u)
        r0 = _row0(u, NA_TQ)
        var = jnp.minimum(u, 2) + jnp.maximum(u - (NA_STEPS - 3), 0)
        for j in range(n_pairs):
            cols = slice(j * LANES, (j + 1) * LANES)
            qq = _stack_halves(q_ref[0, pl.ds(r0, NA_TQ), cols])
            _scores(qq, [k_ref[0, pl.ds(ws, NA_WIN), cols], k_ref[0, SEQ:, cols]], [t_ref[var, j], None],
                    s_scr.at[slot, j], mx_scr.at[slot, j])

    def finish(u, slot):
        ws = window_start(u)
        r0 = _row0(u, NA_TQ)
        for j in range(n_pairs):
            cols = slice(j * LANES, (j + 1) * LANES)
            acc, l = _softmax_pv(s_scr.at[slot, j], mx_scr.at[slot, j],
                                 [v_ref[0, pl.ds(ws, NA_WIN), cols], v_ref[0, SEQ:, cols]])
            r = acc / l
            o_ref[0, pl.ds(r0, NA_TQ), cols] = _merge_halves(r[:NA_TQ], r[NA_TQ:]).astype(BF16)

    _skewed_loop(NA_STEPS, scores, finish)

    def ctx_scores(r0, slot):
        for j in range(n_pairs):
            cols = slice(j * LANES, (j + 1) * LANES)
            qq = _stack_halves(q_ref[0, r0:r0 + NA_TQ, cols])
            _scores(qq, [k_ref[0, SEQ:, cols]], [None], s_scr.at[slot, j], mx_scr.at[slot, j])

    def ctx_finish(r0, slot):
        for j in range(n_pairs):
            cols = slice(j * LANES, (j + 1) * LANES)
            acc, l = _softmax_pv(s_scr.at[slot, j], mx_scr.at[slot, j], [v_ref[0, SEQ:, cols]])
            r = acc / l
            o_ref[0, r0:r0 + NA_TQ, cols] = _merge_halves(r[:NA_TQ], r[NA_TQ:]).astype(BF16)

    _run_skewed([(functools.partial(ctx_scores, r0), functools.partial(ctx_finish, r0))
                 for r0 in range(SEQ, N_TOK, NA_TQ)])


def _na(qb, kb, vb, table):
    bsz = qb.shape[0]
    tok = pl.BlockSpec((1, N_TOK, NA_DIM), lambda b: (b, 0, 0))
    n_pairs = NA_HEADS // 2
    return pl.pallas_call(
        _na_kernel,
        grid=(bsz,),
        in_specs=[tok, tok, tok, _const_spec(table)],
        out_specs=tok,
        out_shape=jax.ShapeDtypeStruct((bsz, N_TOK, NA_DIM), BF16),
        scratch_shapes=_att_scratch((n_pairs,), 2 * NA_TQ, NA_WIN + CTX_LEN),
        compiler_params=_params(1),
        name="neighbourhood_attention",
    )(qb, kb, vb, table)


def _gqa_kernel(sink_ref, q_ref, k_ref, v_ref, o_ref, s_scr, mx_scr):
    row = lax.broadcasted_iota(jnp.int32, (2 * GQA_TQ, GQA_BAND), 0)
    col = lax.broadcasted_iota(jnp.int32, (2 * GQA_TQ, GQA_BAND), 1)
    rel = jnp.where(row >= GQA_TQ, row - GQA_TQ, row) - col
    hi_rows = lax.broadcasted_iota(jnp.int32, (2 * GQA_TQ, 1), 0) >= GQA_TQ

    def band_start(r0):
        return min(max(r0 - GQA_WINDOW, 0), SEQ - GQA_BAND)

    def scores(n, slot):
        r0 = n * GQA_TQ
        start = band_start(r0)
        mask = jnp.where(jnp.abs(rel + (r0 - start)) <= GQA_WINDOW, 0.0, NEG_BIG)
        keys = [k_ref[0, start:start + GQA_BAND, :], k_ref[0, SEQ:, :]]
        for j in range(GQA_HEADS // 2):
            qq = _stack_halves(q_ref[0, r0:r0 + GQA_TQ, j * LANES:(j + 1) * LANES])
            _scores(qq, keys, [mask, None], s_scr.at[slot, j], mx_scr.at[slot, j])

    def finish(n, slot):
        r0 = n * GQA_TQ
        start = band_start(r0)
        vals = [v_ref[0, start:start + GQA_BAND, :], v_ref[0, SEQ:, :]]
        for j in range(GQA_HEADS // 2):
            sink = jnp.where(hi_rows, sink_ref[j + GQA_HEADS // 2], sink_ref[j]) * LOG2E
            acc, l = _softmax_pv(s_scr.at[slot, j], mx_scr.at[slot, j], vals, extra=sink)
            r = acc / l
            o_ref[0, r0:r0 + GQA_TQ, j * LANES:(j + 1) * LANES] = _merge_halves(r[:GQA_TQ], r[GQA_TQ:]).astype(BF16)

    _run_skewed([(functools.partial(scores, n), functools.partial(finish, n)) for n in range(SEQ // GQA_TQ)])


def _gqa(sink, qc, kc, vc):
    bsz = qc.shape[0]
    return pl.pallas_call(
        _gqa_kernel,
        grid=(bsz,),
        in_specs=[pl.BlockSpec(memory_space=pltpu.SMEM),
                  pl.BlockSpec((1, SEQ, 512), lambda b: (b, 0, 0)),
                  pl.BlockSpec((1, N_TOK, LANES), lambda b: (b, 0, 0)),
                  pl.BlockSpec((1, N_TOK, LANES), lambda b: (b, 0, 0))],
        out_specs=pl.BlockSpec((1, SEQ, 512), lambda b: (b, 0, 0)),
        out_shape=jax.ShapeDtypeStruct((bsz, SEQ, 512), BF16),
        scratch_shapes=_att_scratch((GQA_HEADS // 2,), 2 * GQA_TQ, GQA_BAND + CTX_LEN),
        compiler_params=_params(1),
        name="windowed_gqa",
    )(sink, qc, kc, vc)


def _diff_kernel(lv_ref, q_ref, k_ref, v_ref, g_ref, o_ref, s_scr, mx_scr, vt_scr, *, lam_init):
    vt_scr[...] = v_ref[0].T
    lv = lv_ref[...]
    lam = (jnp.exp(jnp.sum(lv[0:1] * lv[1:2], axis=-1, keepdims=True))
           - jnp.exp(jnp.sum(lv[2:3] * lv[3:4], axis=-1, keepdims=True)) + lam_init)
    hi_cols = lax.broadcasted_iota(jnp.int32, (1, 2 * ATT_TQ), 1) >= ATT_TQ
    coef = jnp.where(hi_cols, -lam, 1.0)
    g = g_ref[...] * (1.0 - lam_init)

    def scores(i, slot):
        qq = _stack_halves(q_ref[0, i * ATT_TQ:(i + 1) * ATT_TQ, :])
        st = _dot_nt(k_ref[0], qq)
        s_scr[slot] = st
        mx_scr[slot] = jnp.max(st.reshape(N_TOK // 8, 8, 2 * ATT_TQ), axis=0)

    def finish(i, slot):
        m = jnp.max(mx_scr[slot], axis=0, keepdims=True)
        p = jnp.exp2(s_scr[slot] - m)
        l = jnp.sum(jnp.sum(p.reshape(N_TOK // 8, 8, 2 * ATT_TQ), axis=0), axis=0, keepdims=True)
        ot = _dot(vt_scr[...], p.astype(BF16)) * (coef / l)
        o = (ot[:, :ATT_TQ] + ot[:, ATT_TQ:]).T
        o_ref[0, i * ATT_TQ:(i + 1) * ATT_TQ, :] = (o * _rms_rows(o) * g).astype(BF16)

    _run_skewed([(functools.partial(scores, i), functools.partial(finish, i)) for i in range(SEQ // ATT_TQ)])


def _diff(lv, qd, kd, vd, g, lam_init):
    bsz = qd.shape[0]
    return pl.pallas_call(
        functools.partial(_diff_kernel, lam_init=lam_init),
        grid=(bsz, DIFF_HEADS),
        in_specs=[_const_spec(lv),
                  pl.BlockSpec((1, SEQ, LANES), lambda b, h: (b, 0, h)),
                  pl.BlockSpec((1, N_TOK, LANES), lambda b, h: (b, 0, h)),
                  pl.BlockSpec((1, N_TOK, LANES), lambda b, h: (b, 0, h)),
                  _const_spec(g)],
        out_specs=pl.BlockSpec((1, SEQ, LANES), lambda b, h: (b, 0, h)),
        out_shape=jax.ShapeDtypeStruct((bsz, SEQ, DIFF_HEADS * 2 * DIFF_HEAD_DIM), BF16),
        scratch_shapes=[pltpu.VMEM((2, N_TOK, 2 * ATT_TQ), F32), pltpu.VMEM((2, 8, 2 * ATT_TQ), F32),
                        pltpu.VMEM((LANES, N_TOK), BF16)],
        compiler_params=_params(2),
        name="diff_attention",
    )(lv, qd, kd, vd, g)


def _ffn_body(x, o1_ref, o2_ref, mod_ref, g2_ref, wo1_ref, wo2_ref, wg_ref, wu_ref, wd_ref, y_ref):
    attn = _rows_dot(o1_ref[...], wo1_ref[...]) + _rows_dot(o2_ref[...], wo2_ref[...])
    x1 = x + _mod_row(mod_ref, 2) * attn
    h = (x1 * _rms_rows(x1) * g2_ref[...] * (1.0 + _mod_row(mod_ref, 4)) + _mod_row(mod_ref, 3)).astype(BF16)
    gate = _rows_dot(h, wg_ref[...])
    up = _rows_dot(h, wu_ref[...])
    a = (gate * (1.0 / (1.0 + jnp.exp(-gate))) * up).astype(BF16)
    y_ref[...] = x1 + _mod_row(mod_ref, 5) * _rows_dot(a, wd_ref[...])


def _ffn0_kernel(x_ref, c_ref, *rest):
    _ffn_body(_read_tokens(x_ref, c_ref), *rest)


def _ffn1_kernel(x_ref, *rest):
    _ffn_body(x_ref[...], *rest)


def _ffn(xs, o1, o2, mod, p, n_tiles):
    bsz = xs[0].shape[0]
    bt = _batch_tile(bsz)
    consts = [p['g2'], p['wo1'], p['wo2'], p['wg'], p['wu'], p['wd']]
    x_specs = [_lat_spec(bt), _ctx_spec(bt)] if len(xs) == 2 else [_tok_spec(bt, D_MODEL)]
    return pl.pallas_call(
        _ffn0_kernel if len(xs) == 2 else _ffn1_kernel,
        grid=(n_tiles, bsz // bt),
        in_specs=(x_specs + [_tok_spec(bt, 512), _tok_spec(bt, 512), _mod_spec(bt)] + [_const_spec(a) for a in consts]),
        out_specs=_tok_spec(bt, D_MODEL),
        out_shape=jax.ShapeDtypeStruct((bsz, n_tiles * TOK_TILE, D_MODEL), F32),
        compiler_params=_params(2),
        name="outproj_ffn",
    )(*xs, o1, o2, mod, *consts)


def _partner_cols(w, dr):
    q = dr // 4
    shp = w.shape
    return jnp.flip(w.reshape(shp[:-1] + (shp[-1] // dr, 2, 2, q)), axis=-2).reshape(shp)


def _rope_tables(dr):
    h = dr // 2
    q = dr // 4
    freqs = ROPE_THETA ** (-jnp.arange(0, h, 2, dtype=F32) / h)
    t = jnp.arange(SEQ)
    pos = jnp.stack([t // GRID_W, t % GRID_W], axis=1).astype(F32)
    ang = (pos[:, :, None, None] * freqs[None, None, None, :])
    ang = jnp.broadcast_to(ang, (SEQ, 2, 2, q)).reshape(SEQ, dr)
    sign = np.tile(np.repeat(np.array([-1.0, 1.0], np.float32), q), 2)
    cos = jnp.concatenate([jnp.cos(ang), jnp.ones((CTX_LEN, dr), F32)], axis=0)
    sin = jnp.concatenate([jnp.sin(ang) * sign, jnp.zeros((CTX_LEN, dr), F32)], axis=0)
    return cos, sin


def _layer0_params(w_in, qa_g, w_uq, kva_g, w_ukv, qn_g, kn_g, na_qn_g, na_kn_g, norm1_g):
    cos, sin = _rope_tables(MLA_ROPE)
    o_kr = MLA_Q_RANK + MLA_KV_RANK
    kr_w = w_in[:, o_kr:o_kr + MLA_ROPE]
    lane_pad = ((0, 0), (MLA_NOPE, LANES - MLA_QK))
    w0 = jnp.concatenate([w_in[:, :o_kr], jnp.pad(kr_w, lane_pad), jnp.pad(_partner_cols(kr_w, MLA_ROPE), lane_pad),
                          w_in[:, o_kr + MLA_ROPE:]], axis=1).astype(BF16)
    r = w_uq.shape[0]
    w3 = w_uq.reshape(r, MLA_HEADS, MLA_QK)
    main = jnp.pad(w3, ((0, 0), (0, 0), (0, LANES - MLA_QK)))
    rot = jnp.pad(_partner_cols(w3[:, :, MLA_NOPE:], MLA_ROPE), ((0, 0), (0, 0), (MLA_NOPE, LANES - MLA_QK)))
    wuq = jnp.concatenate([main.reshape(r, -1), rot.reshape(r, -1)], axis=1).astype(BF16)

    def tables(g, c):
        z32 = jnp.zeros((N_TOK, LANES - MLA_QK), F32)
        a = jnp.concatenate([jnp.broadcast_to(g[None, :MLA_NOPE], (N_TOK, MLA_NOPE)), g[None, MLA_NOPE:] * cos, z32], axis=1)
        b = jnp.concatenate([jnp.zeros((N_TOK, MLA_NOPE), F32), _partner_cols(g[None, MLA_NOPE:], MLA_ROPE) * sin, z32], axis=1)
        return a * c, b * c

    aq, bq = tables(qn_g, MLA_QK ** -0.5 * LOG2E)
    ak, bk = tables(kn_g, 1.0)
    rk = w_ukv.shape[0]
    k3 = w_ukv.reshape(rk, MLA_HEADS, MLA_NOPE + MLA_V)
    wukv = jnp.concatenate([jnp.pad(k3[:, :, :MLA_NOPE], ((0, 0), (0, 0), (0, LANES - MLA_NOPE))).reshape(rk, -1),
                            k3[:, :, MLA_NOPE:].reshape(rk, -1)], axis=1).astype(BF16)
    return dict(
        g1=norm1_g.reshape(1, -1), w0=w0, qag=qa_g.reshape(1, -1), wuq=wuq, kvag=kva_g.reshape(1, -1),
        wukv=wukv, aq=aq, bq=bq, ak=ak, bk=bk,
        gqb=(jnp.tile(na_qn_g, 2) * (NA_HEAD_DIM ** -0.5 * LOG2E)).reshape(1, -1),
        gkb=jnp.tile(na_kn_g, 2).reshape(1, -1))


def _pair_heads(w, axis):
    g = GQA_HEADS // GQA_KV_HEADS
    shp = w.shape
    w4 = w.reshape(shp[:axis] + (GQA_KV_HEADS, g, GQA_HEAD_DIM) + shp[axis + 1:])
    return jnp.swapaxes(w4, axis, axis + 1).reshape(shp)


def _layer1_params(w_in, gqa_qn_g, gqa_kn_g, diff_qn_g, diff_kn_g, norm1_g):
    cos, sin = _rope_tables(64)
    main = jnp.concatenate([_pair_heads(w_in[:, :512], 1), w_in[:, 512:640], w_in[:, 768:1792]], axis=1)
    w1 = jnp.concatenate([main, w_in[:, 640:768], w_in[:, 1792:], _partner_cols(main, 64)], axis=1).astype(BF16)
    a_parts, b_parts = [], []
    for g, c in ((gqa_qn_g, GQA_HEAD_DIM ** -0.5 * LOG2E), (gqa_kn_g, 1.0),
                 (diff_qn_g, DIFF_HEAD_DIM ** -0.5 * LOG2E), (diff_kn_g, 1.0)):
        a_parts.append(jnp.tile(g[None, :] * cos * c, (1, 2)))
        b_parts.append(jnp.tile(_partner_cols(g[None, :], 64) * sin * c, (1, 2)))
    return dict(g1=norm1_g.reshape(1, -1), w1=w1,
                a=jnp.concatenate(a_parts, axis=1), b=jnp.concatenate(b_parts, axis=1))


def _na_table_kernel(w_ref, o_ref):
    row = lax.broadcasted_iota(jnp.int32, (GRID_W, LANES), 0)
    lane = lax.broadcasted_iota(jnp.int32, (GRID_W, LANES), 1)
    kc = jnp.where(lane < GRID_W, lane, lane - GRID_W)
    c0 = jnp.clip(row - NA_COLS // 2, 0, GRID_W - NA_COLS)
    band = (kc >= c0) & (kc < c0 + NA_COLS)
    shift = LANES - (GRID_W - 1)
    for hf in range(2):
        for a in range(NA_Q_ROWS):
            for jp in range(NA_WIN_ROWS // 2):
                n0 = a * NA_WIN_ROWS + 2 * jp
                lo = jnp.broadcast_to(w_ref[0, hf, 0, n0:n0 + 1, :], (GRID_W, LANES))
                hi = jnp.broadcast_to(w_ref[0, hf, 0, n0 + 1:n0 + 2, :], (GRID_W, LANES))
                lo = pltpu.roll(lo, shift, 1, stride=1, stride_axis=0)
                hi = pltpu.roll(hi, (shift + GRID_W) % LANES, 1, stride=1, stride_axis=0)
                r0 = (hf * NA_Q_ROWS + a) * GRID_W
                o_ref[0, 0, r0:r0 + GRID_W, jp * LANES:(jp + 1) * LANES] = jnp.where(
                    band, jnp.where(lane < GRID_W, lo, hi), NEG_BIG)


def _na_table(rpb):
    us = np.array(NA_VARIANT_UNITS)
    n_var = len(us)
    w0 = np.minimum(np.clip(NA_Q_ROWS * us - NA_ROWS_MAX // 2, 0, GRID_ROWS - NA_ROWS_MAX), GRID_ROWS - NA_WIN_ROWS)
    r = NA_Q_ROWS * us[:, None, None] + np.arange(NA_Q_ROWS)[None, :, None]
    key_r = w0[:, None, None] + np.arange(NA_WIN_ROWS)[None, None, :]
    r0 = np.clip(r - NA_ROWS_MAX // 2, 0, GRID_ROWS - NA_ROWS_MAX)
    row_valid = (key_r >= r0) & (key_r < r0 + NA_ROWS_MAX)
    off_r = np.clip(key_r - r + (NA_ROWS_MAX - 1), 0, 2 * NA_ROWS_MAX - 2)
    pick = (off_r.reshape(-1, 1) == np.arange(2 * NA_ROWS_MAX - 1)[None, :]).astype(np.float32)
    rows = jnp.einsum('nr,hrd->hnd', pick, rpb, precision=lax.Precision.HIGHEST) * LOG2E
    pad = GRID_W - NA_COLS
    w = jnp.pad(rows, ((0, 0), (0, 0), (pad, LANES - pad - (2 * NA_COLS - 1))))
    w = jnp.where(jnp.asarray(row_valid.reshape(1, -1, 1)), w, NEG_BIG)
    w = w.reshape(NA_HEADS // 2, 2, n_var, NA_Q_ROWS * NA_WIN_ROWS, LANES)
    return pl.pallas_call(
        _na_table_kernel,
        grid=(n_var, NA_HEADS // 2),
        in_specs=[pl.BlockSpec((1, 2, 1, NA_Q_ROWS * NA_WIN_ROWS, LANES), lambda v, p: (p, 0, v, 0, 0))],
        out_specs=pl.BlockSpec((1, 1, 2 * NA_TQ, NA_WIN), lambda v, p: (v, p, 0, 0)),
        out_shape=jax.ShapeDtypeStruct((n_var, NA_HEADS // 2, 2 * NA_TQ, NA_WIN), F32),
        compiler_params=_params(2),
        name="na_bias_table",
    )(w)


def _ffn_params(norm2_g, w_out, w_gate, w_up, w_down, pair_gqa=False):
    wo1 = _pair_heads(w_out[:512], 0) if pair_gqa else w_out[:512]
    return dict(g2=norm2_g.reshape(1, -1), wo1=wo1.astype(BF16), wo2=w_out[512:].astype(BF16),
                wg=w_gate.astype(BF16), wu=w_up.astype(BF16), wd=w_down.astype(BF16))


def _mod_rows(mod_all, bsz):
    lat = mod_all[:bsz].reshape(bsz, 1, 6, D_MODEL)
    cx = jnp.broadcast_to(mod_all[bsz].reshape(1, 1, 6, D_MODEL), (bsz, 1, 6, D_MODEL))
    return jnp.concatenate([lat, cx], axis=1)


def kernel(x, c, ctx, c_ctx, l0_ada_w, l0_ada_b, l0_norm1_g, l0_norm2_g, l0_w_in, l0_mla_qa_g, l0_mla_w_uq, l0_mla_kva_g, l0_mla_w_ukv, l0_mla_qn_g, l0_mla_kn_g, l0_na_qn_g, l0_na_kn_g, l0_na_rpb, l0_w_out, l0_ffn_w_gate, l0_ffn_w_up, l0_ffn_w_down, l1_ada_w, l1_ada_b, l1_norm1_g, l1_norm2_g, l1_w_in, l1_gqa_qn_g, l1_gqa_kn_g, l1_gqa_sink, l1_diff_qn_g, l1_diff_kn_g, l1_diff_lq1, l1_diff_lk1, l1_diff_lq2, l1_diff_lk2, l1_diff_subln_g, l1_w_out, l1_ffn_w_gate, l1_ffn_w_up, l1_ffn_w_down):
    bsz = x.shape[0]
    assert x.shape[1:] == (SEQ, D_MODEL) and ctx.shape[1:] == (CTX_LEN, D_MODEL)
    rows = -(-(bsz + 1) // 8) * 8
    cond = jnp.concatenate([c, c_ctx[None, :], jnp.zeros((rows - bsz - 1, D_MODEL), F32)], axis=0)
    mod0 = _mod_rows(_ada(cond, l0_ada_w, l0_ada_b), bsz)
    mod1 = _mod_rows(_ada(cond, l1_ada_w, l1_ada_b), bsz)

    p0 = _layer0_params(l0_w_in, l0_mla_qa_g, l0_mla_w_uq, l0_mla_kva_g, l0_mla_w_ukv, l0_mla_qn_g,
                        l0_mla_kn_g, l0_na_qn_g, l0_na_kn_g, l0_norm1_g)
    qa, ka, va, qb, kb, vb = _proj0(x, ctx, mod0, p0)
    o_a = _mla(qa, ka, va)
    o_b = _na(qb, kb, vb, _na_table(l0_na_rpb))
    f0 = _ffn_params(l0_norm2_g, l0_w_out, l0_ffn_w_gate, l0_ffn_w_up, l0_ffn_w_down)
    xa = _ffn((x, ctx), o_a, o_b, mod0, f0, N_TILES)

    p1 = _layer1_params(l1_w_in, l1_gqa_qn_g, l1_gqa_kn_g, l1_diff_qn_g, l1_diff_kn_g, l1_norm1_g)
    qc, kc, vc, qd, kd, vd = _proj1(xa, mod1, p1)
    o_c = _gqa(l1_gqa_sink, qc, kc, vc)
    lv = jnp.zeros((8, LANES), F32).at[:4, :DIFF_HEAD_DIM].set(
        jnp.stack([l1_diff_lq1, l1_diff_lk1, l1_diff_lq2, l1_diff_lk2]))
    lam_init = 0.8 - 0.6 * math.exp(-0.3 * 1)
    o_d = _diff(lv, qd, kd, vd, l1_diff_subln_g.reshape(1, -1), lam_init)
    f1 = _ffn_params(l1_norm2_g, l1_w_out, l1_ffn_w_gate, l1_ffn_w_up, l1_ffn_w_down, pair_gqa=True)
    return _ffn((xa,), o_c, o_d, mod1, f1, N_LAT_TILES)
```

```python
import functools
import math

import numpy as np
import jax
import jax.numpy as jnp
from jax import lax
from jax.experimental import pallas as pl
from jax.experimental.pallas import tpu as pltpu

F32 = jnp.float32
BF16 = jnp.bfloat16

D_MODEL = 1024
SEQ = 2048
GRID_W = 64
CTX_LEN = 256
N_TOK = SEQ + CTX_LEN
ROPE_THETA = 10000.0
EPS = 1e-6

MLA_HEADS = 8
MLA_Q_RANK = 256
MLA_KV_RANK = 128
MLA_NOPE = 64
MLA_ROPE = 32
MLA_V = 64
MLA_QK = MLA_NOPE + MLA_ROPE

NA_HEADS = 8
NA_HEAD_DIM = 64
NA_ROWS_MAX = 8
NA_COLS = 16
NA_DIM = NA_HEADS * NA_HEAD_DIM

GQA_HEADS = 8
GQA_KV_HEADS = 2
GQA_HEAD_DIM = 64
GQA_WINDOW = 128

DIFF_HEADS = 4
DIFF_HEAD_DIM = 64

MLA_VDIM = MLA_HEADS * MLA_V
GQA_Q = GQA_HEADS * GQA_HEAD_DIM
GQA_KV = GQA_KV_HEADS * GQA_HEAD_DIM
DIFF_DIM = DIFF_HEADS * 2 * DIFF_HEAD_DIM

LANES = 128
MLA_SLOTS = MLA_HEADS * LANES
TOK_TILE = 256
N_TILES = N_TOK // TOK_TILE
N_LAT_TILES = SEQ // TOK_TILE
VMEM_LIMIT = 56 * 1024 * 1024
LOG2E = 1.4426950408889634
NEG_BIG = -1e30

ATT_TQ = 256
MLA_TQ = 256
NA_Q_ROWS = 2
NA_TQ = NA_Q_ROWS * GRID_W
NA_WIN_ROWS = 10
NA_WIN = NA_WIN_ROWS * GRID_W
NA_STEPS = SEQ // NA_TQ
NA_VARIANT_UNITS = (0, 1, 2, NA_STEPS - 2, NA_STEPS - 1)
GRID_ROWS = SEQ // GRID_W
GQA_TQ = 128
GQA_BAND = GQA_TQ + 2 * GQA_WINDOW


def _dot(a, b):
    return jnp.dot(a, b, preferred_element_type=F32)


def _dot_nt(a, b):
    return lax.dot_general(a, b, (((1,), (1,)), ((), ())), preferred_element_type=F32)


def _rms_rows(x):
    return lax.rsqrt(jnp.mean(x * x, axis=-1, keepdims=True) + EPS)


def _const_spec(a):
    nd = a.ndim
    return pl.BlockSpec(a.shape, lambda *_: (0,) * nd)


def _params(n_grid):
    return pltpu.CompilerParams(dimension_semantics=("arbitrary",) * n_grid,
                                vmem_limit_bytes=VMEM_LIMIT)


def _fold_lanes(xs, op):
    r = None
    for x in xs:
        for lo in range(0, x.shape[1], LANES):
            blk = x[:, lo:lo + LANES]
            r = blk if r is None else op(r, blk)
    return r


def _scores(q, keys, biases, s_ref, mx_ref, keys_t=False):
    parts = []
    lo = 0
    for k, bias in zip(keys, biases):
        sc = _dot(q, k) if keys_t else _dot_nt(q, k)
        if bias is not None:
            sc = sc + bias
        s_ref[:, lo:lo + sc.shape[1]] = sc
        lo += sc.shape[1]
        parts.append(sc)
    mx_ref[...] = _fold_lanes(parts, jnp.maximum)


def _softmax_pv(s_ref, mx_ref, vals, extra=None):
    m = jnp.max(mx_ref[...], axis=-1, keepdims=True)
    if extra is not None:
        m = jnp.maximum(m, extra)
    ps = []
    lo = 0
    for v in vals:
        ps.append(jnp.exp2(s_ref[:, lo:lo + v.shape[0]] - m))
        lo += v.shape[0]
    l = jnp.sum(_fold_lanes(ps, jnp.add), axis=-1, keepdims=True)
    if extra is not None:
        l = l + jnp.exp2(extra - m)
    acc = _dot(ps[0].astype(BF16), vals[0])
    for p, v in zip(ps[1:], vals[1:]):
        acc = acc + _dot(p.astype(BF16), v)
    return acc, l


def _run_skewed(units):
    units[0][0](0)
    for i, (_, finish) in enumerate(units):
        if i + 1 < len(units):
            units[i + 1][0]((i + 1) % 2)
        finish(i % 2)


def _row0(i, size):
    return i * size if isinstance(i, int) else pl.multiple_of(i * size, size)


def _skewed_loop(n, scores, finish, per_trip=8):
    assert n % per_trip == 0 and per_trip % 2 == 0
    scores(0, 0)

    def body(j, carry):
        for d in range(per_trip):
            i = per_trip * j + d
            scores(jnp.minimum(i + 1, n - 1), (d + 1) % 2)
            finish(i, d % 2)
        return carry

    lax.fori_loop(0, n // per_trip, body, 0)


def _att_scratch(lead, rows, n_keys):
    lead = (2,) + tuple(lead)
    return [pltpu.VMEM(lead + (rows, n_keys), F32), pltpu.VMEM(lead + (rows, LANES), F32)]


def _stack_halves(q):
    lane = lax.broadcasted_iota(jnp.int32, q.shape, 1)
    zero = jnp.zeros_like(q)
    return jnp.concatenate([jnp.where(lane < 64, q, zero), jnp.where(lane >= 64, q, zero)], axis=0)


def _merge_halves(lo, hi):
    lane = lax.broadcasted_iota(jnp.int32, lo.shape, 1)
    return jnp.where(lane < 64, lo, hi)


def _ada_kernel(c_ref, w_ref, b_ref, o_ref):
    c = c_ref[...]
    a = (c * (1.0 / (1.0 + jnp.exp(-c)))).astype(BF16)
    o_ref[...] = _dot(a, w_ref[...].astype(BF16)) + b_ref[...]


def _ada(cond, w, b):
    n = w.shape[1]
    tn = D_MODEL
    return pl.pallas_call(
        _ada_kernel,
        grid=(n // tn,),
        in_specs=[pl.BlockSpec(cond.shape, lambda j: (0, 0)),
                  pl.BlockSpec((w.shape[0], tn), lambda j: (0, j)),
                  pl.BlockSpec((1, tn), lambda j: (0, j))],
        out_specs=pl.BlockSpec((cond.shape[0], tn), lambda j: (0, j)),
        out_shape=jax.ShapeDtypeStruct((cond.shape[0], n), F32),
        compiler_params=_params(1),
        name="ada_modulation",
    )(cond, w, b.reshape(1, n))


def _batch_tile(bsz, most=2):
    return max(bt for bt in (1, 2, 4) if bt <= most and bsz % bt == 0)


def _tok_spec(bt, width):
    return pl.BlockSpec((bt, TOK_TILE, width), lambda t, b: (b, t, 0))


def _lat_spec(bt):
    return pl.BlockSpec((bt, TOK_TILE, D_MODEL), lambda t, b: (b, jnp.minimum(t, N_LAT_TILES - 1), 0))


def _ctx_spec(bt):
    return pl.BlockSpec((bt, TOK_TILE, D_MODEL), lambda t, b: (b, 0, 0))


def _mod_spec(bt):
    return pl.BlockSpec((bt, 1, 6, D_MODEL), lambda t, b: (b, t // N_LAT_TILES, 0, 0))


def _table_spec(width):
    return pl.BlockSpec((TOK_TILE, width), lambda t, b: (t, 0))


def _read_tokens(x_ref, c_ref):
    return jnp.where(pl.program_id(0) < N_LAT_TILES, x_ref[...], c_ref[...])


def _mod_row(mod_ref, i):
    return mod_ref[:, 0, i:i + 1, :]


def _rows_dot(a, w):
    bt, rows, k = a.shape
    return _dot(a.reshape(bt * rows, k), w).reshape(bt, rows, w.shape[1])


W0_CQ = 0
W0_CKV = W0_CQ + MLA_Q_RANK
W0_KR = W0_CKV + MLA_KV_RANK
W0_KRP = W0_KR + LANES
W0_QN = W0_KRP + LANES
W0_KN = W0_QN + NA_DIM
W0_VN = W0_KN + NA_DIM
W0_END = W0_VN + NA_DIM


def _head_inv(z, n):
    return lax.rsqrt(jnp.sum(z * z, axis=-1, keepdims=True) * (1.0 / n) + EPS)


def _half_inv(z):
    z2 = z * z
    lane = lax.broadcasted_iota(jnp.int32, z.shape, z.ndim - 1)
    lo = jnp.sum(jnp.where(lane < 64, z2, 0.0), axis=-1, keepdims=True)
    hi = jnp.sum(jnp.where(lane < 64, 0.0, z2), axis=-1, keepdims=True)
    return jnp.where(lane < 64, lax.rsqrt(lo * (1.0 / 64.0) + EPS), lax.rsqrt(hi * (1.0 / 64.0) + EPS))


def _proj0_kernel(x_ref, c_ref, mod_ref, g1_ref, w0_ref, qag_ref, wuq_ref, kvag_ref, wukv_ref,
                  aq_ref, bq_ref, ak_ref, bk_ref, gqb_ref, gkb_ref,
                  qa_ref, ka_ref, va_ref, qb_ref, kb_ref, vb_ref):
    x = _read_tokens(x_ref, c_ref)
    h = x * _rms_rows(x) * g1_ref[...] * (1.0 + _mod_row(mod_ref, 1)) + _mod_row(mod_ref, 0)
    z = _rows_dot(h.astype(BF16), w0_ref[...])

    cq = z[:, :, W0_CQ:W0_CKV]
    cqn = (cq * _rms_rows(cq) * qag_ref[...]).astype(BF16)
    zq = _rows_dot(cqn, wuq_ref[...])
    aq, bq = aq_ref[...], bq_ref[...]
    for hd in range(MLA_HEADS):
        qm = zq[:, :, hd * LANES:(hd + 1) * LANES]
        qr = zq[:, :, MLA_SLOTS + hd * LANES:MLA_SLOTS + (hd + 1) * LANES]
        qa_ref[:, :, hd * LANES:(hd + 1) * LANES] = ((qm * aq + qr * bq) * _head_inv(qm, MLA_QK)).astype(BF16)

    ckv = z[:, :, W0_CKV:W0_KR]
    ckvn = (ckv * _rms_rows(ckv) * kvag_ref[...]).astype(BF16)
    zkv = _rows_dot(ckvn, wukv_ref[...])
    kr = z[:, :, W0_KR:W0_KRP]
    kr_rot = z[:, :, W0_KRP:W0_QN] * bk_ref[...]
    ak = ak_ref[...]
    for hd in range(MLA_HEADS):
        km = zkv[:, :, hd * LANES:(hd + 1) * LANES] + kr
        ka_ref[:, :, hd * LANES:(hd + 1) * LANES] = ((km * ak + kr_rot) * _head_inv(km, MLA_QK)).astype(BF16)
    va_ref[...] = zkv[:, :, MLA_SLOTS:].astype(BF16)

    for j in range(NA_DIM // LANES):
        cols = slice(j * LANES, (j + 1) * LANES)
        qn = z[:, :, W0_QN + j * LANES:W0_QN + (j + 1) * LANES]
        kn = z[:, :, W0_KN + j * LANES:W0_KN + (j + 1) * LANES]
        qb_ref[:, :, cols] = (qn * gqb_ref[...] * _half_inv(qn)).astype(BF16)
        kb_ref[:, :, cols] = (kn * gkb_ref[...] * _half_inv(kn)).astype(BF16)
    vb_ref[...] = z[:, :, W0_VN:W0_END].astype(BF16)


def _proj0(x, ctx, mod, p):
    bsz = x.shape[0]
    bt = _batch_tile(bsz, most=4)
    consts1 = [p['g1'], p['w0'], p['qag'], p['wuq'], p['kvag'], p['wukv']]
    tables = [p['aq'], p['bq'], p['ak'], p['bk']]
    consts2 = [p['gqb'], p['gkb']]
    widths = [MLA_SLOTS, MLA_SLOTS, MLA_VDIM, NA_DIM, NA_DIM, NA_DIM]
    return pl.pallas_call(
        _proj0_kernel,
        grid=(N_TILES, bsz // bt),
        in_specs=([_lat_spec(bt), _ctx_spec(bt), _mod_spec(bt)] + [_const_spec(a) for a in consts1]
                  + [_table_spec(LANES) for _ in tables] + [_const_spec(a) for a in consts2]),
        out_specs=[_tok_spec(bt, w) for w in widths],
        out_shape=[jax.ShapeDtypeStruct((bsz, N_TOK, w), BF16) for w in widths],
        compiler_params=_params(2),
        name="proj_layer0",
    )(x, ctx, mod, *consts1, *tables, *consts2)


N1_SEG = tuple(w // LANES for w in (GQA_Q, GQA_KV, DIFF_DIM, DIFF_DIM))
N1_V_GQA = sum(N1_SEG) * LANES
N1_V_DIFF = N1_V_GQA + GQA_KV
N1_MAIN = N1_V_DIFF + DIFF_DIM


def _proj1_kernel(x_ref, mod_ref, g1_ref, w1_ref, a_ref, b_ref,
                  qc_ref, kc_ref, vc_ref, qd_ref, kd_ref, vd_ref):
    x = x_ref[...]
    h = x * _rms_rows(x) * g1_ref[...] * (1.0 + _mod_row(mod_ref, 1)) + _mod_row(mod_ref, 0)
    z = _rows_dot(h.astype(BF16), w1_ref[...])
    outs = (qc_ref, kc_ref, qd_ref, kd_ref)
    blk = 0
    for seg, n_blk in enumerate(N1_SEG):
        a = a_ref[:, seg * LANES:(seg + 1) * LANES]
        b = b_ref[:, seg * LANES:(seg + 1) * LANES]
        for j in range(n_blk):
            zm = z[:, :, blk * LANES:(blk + 1) * LANES]
            rot = z[:, :, N1_MAIN + blk * LANES:N1_MAIN + (blk + 1) * LANES]
            outs[seg][:, :, j * LANES:(j + 1) * LANES] = ((zm * a + rot * b) * _half_inv(zm)).astype(BF16)
            blk += 1
    vc_ref[...] = z[:, :, N1_V_GQA:N1_V_DIFF].astype(BF16)
    vd_ref[...] = z[:, :, N1_V_DIFF:N1_MAIN].astype(BF16)


def _proj1(xa, mod, p):
    bsz = xa.shape[0]
    bt = _batch_tile(bsz, most=4)
    consts = [p['g1'], p['w1']]
    widths = [GQA_Q, GQA_KV, GQA_KV, DIFF_DIM, DIFF_DIM, DIFF_DIM]
    return pl.pallas_call(
        _proj1_kernel,
        grid=(N_TILES, bsz // bt),
        in_specs=([_tok_spec(bt, D_MODEL), _mod_spec(bt)] + [_const_spec(a) for a in consts]
                  + [_table_spec(4 * LANES), _table_spec(4 * LANES)]),
        out_specs=[_tok_spec(bt, w) for w in widths],
        out_shape=[jax.ShapeDtypeStruct((bsz, N_TOK, w), BF16) for w in widths],
        compiler_params=_params(2),
        name="proj_layer1",
    )(xa, mod, *consts, p['a'], p['b'])


def _mla_kernel(q_ref, k_ref, v_ref, o_ref, s_scr, mx_scr, kt_scr):
    for hh in range(2):
        kt_scr[hh] = k_ref[0, :, hh * LANES:(hh + 1) * LANES].T

    def scores(r0, slot, key_lo):
        for hh in range(2):
            cols = slice(hh * LANES, (hh + 1) * LANES)
            rows = slice(hh * MLA_TQ, (hh + 1) * MLA_TQ)
            _scores(q_ref[0, r0:r0 + MLA_TQ, cols], [kt_scr[hh, :, key_lo:]], [None],
                    s_scr.at[slot, rows], mx_scr.at[slot, rows], keys_t=True)

    def finish(r0, slot, key_lo):
        acc, l = _softmax_pv(s_scr.at[slot], mx_scr.at[slot], [v_ref[0, key_lo:, :]])
        r = acc / l
        o_ref[0, r0:r0 + MLA_TQ, :] = _merge_halves(r[:MLA_TQ], r[MLA_TQ:]).astype(BF16)

    units = [(functools.partial(scores, r0, key_lo=0), functools.partial(finish, r0, key_lo=0))
             for r0 in range(0, SEQ, MLA_TQ)]
    units.append((functools.partial(scores, SEQ, key_lo=SEQ),
                  functools.partial(finish, SEQ, key_lo=SEQ)))
    _run_skewed(units)


def _mla(qa, ka, va):
    bsz = qa.shape[0]
    assert CTX_LEN == MLA_TQ
    return pl.pallas_call(
        _mla_kernel,
        grid=(bsz, MLA_HEADS // 2),
        in_specs=[pl.BlockSpec((1, N_TOK, 2 * LANES), lambda b, j: (b, 0, j)),
                  pl.BlockSpec((1, N_TOK, 2 * LANES), lambda b, j: (b, 0, j)),
                  pl.BlockSpec((1, N_TOK, LANES), lambda b, j: (b, 0, j))],
        out_specs=pl.BlockSpec((1, N_TOK, LANES), lambda b, j: (b, 0, j)),
        out_shape=jax.ShapeDtypeStruct((bsz, N_TOK, MLA_HEADS * MLA_V), BF16),
        scratch_shapes=_att_scratch((), 2 * MLA_TQ, N_TOK) + [pltpu.VMEM((2, LANES, N_TOK), BF16)],
        compiler_params=_params(2),
        name="mla_attention",
    )(qa, ka, va)


def _na_kernel(q_ref, k_ref, v_ref, t_ref, o_ref, s_scr, mx_scr):
    n_pairs = NA_HEADS // 2

    def window_start(u):
        w0 = jnp.minimum(jnp.clip(NA_Q_ROWS * u - NA_ROWS_MAX // 2, 0, GRID_ROWS - NA_ROWS_MAX),
                         GRID_ROWS - NA_WIN_ROWS)
        return pl.multiple_of(w0 * GRID_W, GRID_W)

    def scores(u, slot):
        ws = window_start(u)
        r0 = _row0(u, NA_TQ)
        var = jnp.minimum(u, 2) + jnp.maximum(u - (NA_STEPS - 3), 0)
        for j in range(n_pairs):
            cols = slice(j * LANES, (j + 1) * LANES)
            qq = _stack_halves(q_ref[0, pl.ds(r0, NA_TQ), cols])
            _scores(qq, [k_ref[0, pl.ds(ws, NA_WIN), cols], k_ref[0, SEQ:, cols]], [t_ref[var, j], None],
                    s_scr.at[slot, j], mx_scr.at[slot, j])

    def finish(u, slot):
        ws = window_start(u)
        r0 = _row0(u, NA_TQ)
        for j in range(n_pairs):
            cols = slice(j * LANES, (j + 1) * LANES)
            acc, l = _softmax_pv(s_scr.at[slot, j], mx_scr.at[slot, j],
                                 [v_ref[0, pl.ds(ws, NA_WIN), cols], v_ref[0, SEQ:, cols]])
            r = acc / l
            o_ref[0, pl.ds(r0, NA_TQ), cols] = _merge_halves(r[:NA_TQ], r[NA_TQ:]).astype(BF16)

    _skewed_loop(NA_STEPS, scores, finish)

    def ctx_scores(r0, slot):
        for j in range(n_pairs):
            cols = slice(j * LANES, (j + 1) * LANES)
            qq = _stack_halves(q_ref[0, r0:r0 + NA_TQ, cols])
            _scores(qq, [k_ref[0, SEQ:, cols]], [None], s_scr.at[slot, j], mx_scr.at[slot, j])

    def ctx_finish(r0, slot):
        for j in range(n_pairs):
            cols = slice(j * LANES, (j + 1) * LANES)
            acc, l = _softmax_pv(s_scr.at[slot, j], mx_scr.at[slot, j], [v_ref[0, SEQ:, cols]])
            r = acc / l
            o_ref[0, r0:r0 + NA_TQ, cols] = _merge_halves(r[:NA_TQ], r[NA_TQ:]).astype(BF16)

    _run_skewed([(functools.partial(ctx_scores, r0), functools.partial(ctx_finish, r0))
                 for r0 in range(SEQ, N_TOK, NA_TQ)])


def _na(qb, kb, vb, table):
    bsz = qb.shape[0]
    tok = pl.BlockSpec((1, N_TOK, NA_DIM), lambda b: (b, 0, 0))
    n_pairs = NA_HEADS // 2
    return pl.pallas_call(
        _na_kernel,
        grid=(bsz,),
        in_specs=[tok, tok, tok, _const_spec(table)],
        out_specs=tok,
        out_shape=jax.ShapeDtypeStruct((bsz, N_TOK, NA_DIM), BF16),
        scratch_shapes=_att_scratch((n_pairs,), 2 * NA_TQ, NA_WIN + CTX_LEN),
        compiler_params=_params(1),
        name="neighbourhood_attention",
    )(qb, kb, vb, table)


def _gqa_kernel(sink_ref, q_ref, k_ref, v_ref, o_ref, s_scr, mx_scr):
    row = lax.broadcasted_iota(jnp.int32, (2 * GQA_TQ, GQA_BAND), 0)
    col = lax.broadcasted_iota(jnp.int32, (2 * GQA_TQ, GQA_BAND), 1)
    rel = jnp.where(row >= GQA_TQ, row - GQA_TQ, row) - col
    hi_rows = lax.broadcasted_iota(jnp.int32, (2 * GQA_TQ, 1), 0) >= GQA_TQ

    def band_start(r0):
        return min(max(r0 - GQA_WINDOW, 0), SEQ - GQA_BAND)

    def scores(n, slot):
        r0 = n * GQA_TQ
        start = band_start(r0)
        mask = jnp.where(jnp.abs(rel + (r0 - start)) <= GQA_WINDOW, 0.0, NEG_BIG)
        keys = [k_ref[0, start:start + GQA_BAND, :], k_ref[0, SEQ:, :]]
        for j in range(GQA_HEADS // 2):
            qq = _stack_halves(q_ref[0, r0:r0 + GQA_TQ, j * LANES:(j + 1) * LANES])
            _scores(qq, keys, [mask, None], s_scr.at[slot, j], mx_scr.at[slot, j])

    def finish(n, slot):
        r0 = n * GQA_TQ
        start = band_start(r0)
        vals = [v_ref[0, start:start + GQA_BAND, :], v_ref[0, SEQ:, :]]
        for j in range(GQA_HEADS // 2):
            sink = jnp.where(hi_rows, sink_ref[j + GQA_HEADS // 2], sink_ref[j]) * LOG2E
            acc, l = _softmax_pv(s_scr.at[slot, j], mx_scr.at[slot, j], vals, extra=sink)
            r = acc / l
            o_ref[0, r0:r0 + GQA_TQ, j * LANES:(j + 1) * LANES] = _merge_halves(r[:GQA_TQ], r[GQA_TQ:]).astype(BF16)

    _run_skewed([(functools.partial(scores, n), functools.partial(finish, n)) for n in range(SEQ // GQA_TQ)])


def _gqa(sink, qc, kc, vc):
    bsz = qc.shape[0]
    return pl.pallas_call(
        _gqa_kernel,
        grid=(bsz,),
        in_specs=[pl.BlockSpec(memory_space=pltpu.SMEM),
                  pl.BlockSpec((1, SEQ, GQA_Q), lambda b: (b, 0, 0)),
                  pl.BlockSpec((1, N_TOK, GQA_KV), lambda b: (b, 0, 0)),
                  pl.BlockSpec((1, N_TOK, GQA_KV), lambda b: (b, 0, 0))],
        out_specs=pl.BlockSpec((1, SEQ, GQA_Q), lambda b: (b, 0, 0)),
        out_shape=jax.ShapeDtypeStruct((bsz, SEQ, GQA_Q), BF16),
        scratch_shapes=_att_scratch((GQA_HEADS // 2,), 2 * GQA_TQ, GQA_BAND + CTX_LEN),
        compiler_params=_params(1),
        name="windowed_gqa",
    )(sink, qc, kc, vc)


def _diff_kernel(lv_ref, q_ref, k_ref, v_ref, g_ref, o_ref, s_scr, mx_scr, kt_scr, *, lam_init):
    kt_scr[...] = k_ref[0].T
    lv = lv_ref[...]
    lam = (jnp.exp(jnp.sum(lv[0:1] * lv[1:2], axis=-1, keepdims=True))
           - jnp.exp(jnp.sum(lv[2:3] * lv[3:4], axis=-1, keepdims=True)) + lam_init)
    hi_rows = lax.broadcasted_iota(jnp.int32, (2 * ATT_TQ, 1), 0) >= ATT_TQ
    coef = jnp.where(hi_rows, -lam, 1.0)
    g = g_ref[...] * (1.0 - lam_init)

    def scores(i, slot):
        qq = _stack_halves(q_ref[0, i * ATT_TQ:(i + 1) * ATT_TQ, :])
        _scores(qq, [kt_scr[...]], [None], s_scr.at[slot], mx_scr.at[slot], keys_t=True)

    def finish(i, slot):
        acc, l = _softmax_pv(s_scr.at[slot], mx_scr.at[slot], [v_ref[0]])
        r = acc * (coef / l)
        o = r[:ATT_TQ] + r[ATT_TQ:]
        o_ref[0, i * ATT_TQ:(i + 1) * ATT_TQ, :] = (o * _rms_rows(o) * g).astype(BF16)

    _run_skewed([(functools.partial(scores, i), functools.partial(finish, i)) for i in range(SEQ // ATT_TQ)])


def _diff(lv, qd, kd, vd, g, lam_init):
    bsz = qd.shape[0]
    return pl.pallas_call(
        functools.partial(_diff_kernel, lam_init=lam_init),
        grid=(bsz, DIFF_HEADS),
        in_specs=[_const_spec(lv),
                  pl.BlockSpec((1, SEQ, LANES), lambda b, h: (b, 0, h)),
                  pl.BlockSpec((1, N_TOK, LANES), lambda b, h: (b, 0, h)),
                  pl.BlockSpec((1, N_TOK, LANES), lambda b, h: (b, 0, h)),
                  _const_spec(g)],
        out_specs=pl.BlockSpec((1, SEQ, LANES), lambda b, h: (b, 0, h)),
        out_shape=jax.ShapeDtypeStruct((bsz, SEQ, DIFF_HEADS * 2 * DIFF_HEAD_DIM), BF16),
        scratch_shapes=_att_scratch((), 2 * ATT_TQ, N_TOK) + [pltpu.VMEM((LANES, N_TOK), BF16)],
        compiler_params=_params(2),
        name="diff_attention",
    )(lv, qd, kd, vd, g)


def _ffn_body(x, o1_ref, o2_ref, mod_ref, g2_ref, wo1_ref, wo2_ref, wg_ref, wu_ref, wd_ref, y_ref):
    attn = _rows_dot(o1_ref[...], wo1_ref[...]) + _rows_dot(o2_ref[...], wo2_ref[...])
    x1 = x + _mod_row(mod_ref, 2) * attn
    h = (x1 * _rms_rows(x1) * g2_ref[...] * (1.0 + _mod_row(mod_ref, 4)) + _mod_row(mod_ref, 3)).astype(BF16)
    gate = _rows_dot(h, wg_ref[...])
    up = _rows_dot(h, wu_ref[...])
    a = (gate * (1.0 / (1.0 + jnp.exp(-gate))) * up).astype(BF16)
    y_ref[...] = x1 + _mod_row(mod_ref, 5) * _rows_dot(a, wd_ref[...])


def _ffn0_kernel(x_ref, c_ref, *rest):
    _ffn_body(_read_tokens(x_ref, c_ref), *rest)


def _ffn1_kernel(x_ref, *rest):
    _ffn_body(x_ref[...], *rest)


def _ffn(xs, o1, o2, mod, p, n_tiles):
    bsz = xs[0].shape[0]
    bt = _batch_tile(bsz)
    consts = [p['g2'], p['wo1'], p['wo2'], p['wg'], p['wu'], p['wd']]
    x_specs = [_lat_spec(bt), _ctx_spec(bt)] if len(xs) == 2 else [_tok_spec(bt, D_MODEL)]
    return pl.pallas_call(
        _ffn0_kernel if len(xs) == 2 else _ffn1_kernel,
        grid=(n_tiles, bsz // bt),
        in_specs=(x_specs + [_tok_spec(bt, o1.shape[-1]), _tok_spec(bt, o2.shape[-1]), _mod_spec(bt)]
                  + [_const_spec(a) for a in consts]),
        out_specs=_tok_spec(bt, D_MODEL),
        out_shape=jax.ShapeDtypeStruct((bsz, n_tiles * TOK_TILE, D_MODEL), F32),
        compiler_params=_params(2),
        name="outproj_ffn",
    )(*xs, o1, o2, mod, *consts)


def _partner_cols(w, dr):
    q = dr // 4
    shp = w.shape
    return jnp.flip(w.reshape(shp[:-1] + (shp[-1] // dr, 2, 2, q)), axis=-2).reshape(shp)


def _rope_tables(dr):
    h = dr // 2
    q = dr // 4
    freqs = ROPE_THETA ** (-jnp.arange(0, h, 2, dtype=F32) / h)
    t = jnp.arange(SEQ)
    pos = jnp.stack([t // GRID_W, t % GRID_W], axis=1).astype(F32)
    ang = (pos[:, :, None, None] * freqs[None, None, None, :])
    ang = jnp.broadcast_to(ang, (SEQ, 2, 2, q)).reshape(SEQ, dr)
    sign = np.tile(np.repeat(np.array([-1.0, 1.0], np.float32), q), 2)
    cos = jnp.concatenate([jnp.cos(ang), jnp.ones((CTX_LEN, dr), F32)], axis=0)
    sin = jnp.concatenate([jnp.sin(ang) * sign, jnp.zeros((CTX_LEN, dr), F32)], axis=0)
    return cos, sin


def _layer0_params(w_in, qa_g, w_uq, kva_g, w_ukv, qn_g, kn_g, na_qn_g, na_kn_g, norm1_g):
    cos, sin = _rope_tables(MLA_ROPE)
    o_kr = MLA_Q_RANK + MLA_KV_RANK
    kr_w = w_in[:, o_kr:o_kr + MLA_ROPE]
    lane_pad = ((0, 0), (MLA_NOPE, LANES - MLA_QK))
    w0 = jnp.concatenate([w_in[:, :o_kr], jnp.pad(kr_w, lane_pad), jnp.pad(_partner_cols(kr_w, MLA_ROPE), lane_pad),
                          w_in[:, o_kr + MLA_ROPE:]], axis=1).astype(BF16)
    r = w_uq.shape[0]
    w3 = w_uq.reshape(r, MLA_HEADS, MLA_QK)
    main = jnp.pad(w3, ((0, 0), (0, 0), (0, LANES - MLA_QK)))
    rot = jnp.pad(_partner_cols(w3[:, :, MLA_NOPE:], MLA_ROPE), ((0, 0), (0, 0), (MLA_NOPE, LANES - MLA_QK)))
    wuq = jnp.concatenate([main.reshape(r, -1), rot.reshape(r, -1)], axis=1).astype(BF16)

    def tables(g, c):
        z32 = jnp.zeros((N_TOK, LANES - MLA_QK), F32)
        a = jnp.concatenate([jnp.broadcast_to(g[None, :MLA_NOPE], (N_TOK, MLA_NOPE)), g[None, MLA_NOPE:] * cos, z32], axis=1)
        b = jnp.concatenate([jnp.zeros((N_TOK, MLA_NOPE), F32), _partner_cols(g[None, MLA_NOPE:], MLA_ROPE) * sin, z32], axis=1)
        return a * c, b * c

    aq, bq = tables(qn_g, MLA_QK ** -0.5 * LOG2E)
    ak, bk = tables(kn_g, 1.0)
    rk = w_ukv.shape[0]
    k3 = w_ukv.reshape(rk, MLA_HEADS, MLA_NOPE + MLA_V)
    wukv = jnp.concatenate([jnp.pad(k3[:, :, :MLA_NOPE], ((0, 0), (0, 0), (0, LANES - MLA_NOPE))).reshape(rk, -1),
                            k3[:, :, MLA_NOPE:].reshape(rk, -1)], axis=1).astype(BF16)
    return dict(
        g1=norm1_g.reshape(1, -1), w0=w0, qag=qa_g.reshape(1, -1), wuq=wuq, kvag=kva_g.reshape(1, -1),
        wukv=wukv, aq=aq, bq=bq, ak=ak, bk=bk,
        gqb=(jnp.tile(na_qn_g, 2) * (NA_HEAD_DIM ** -0.5 * LOG2E)).reshape(1, -1),
        gkb=jnp.tile(na_kn_g, 2).reshape(1, -1))


def _pair_heads(w, axis):
    g = GQA_HEADS // GQA_KV_HEADS
    shp = w.shape
    w4 = w.reshape(shp[:axis] + (GQA_KV_HEADS, g, GQA_HEAD_DIM) + shp[axis + 1:])
    return jnp.swapaxes(w4, axis, axis + 1).reshape(shp)


def _layer1_params(w_in, gqa_qn_g, gqa_kn_g, diff_qn_g, diff_kn_g, norm1_g):
    cos, sin = _rope_tables(64)
    o_vg = GQA_Q + GQA_KV
    o_qd = o_vg + GQA_KV
    o_vd = o_qd + 2 * DIFF_DIM
    main = jnp.concatenate([_pair_heads(w_in[:, :GQA_Q], 1), w_in[:, GQA_Q:o_vg], w_in[:, o_qd:o_vd]], axis=1)
    w1 = jnp.concatenate([main, w_in[:, o_vg:o_qd], w_in[:, o_vd:], _partner_cols(main, GQA_HEAD_DIM)],
                         axis=1).astype(BF16)
    a_parts, b_parts = [], []
    for g, c in ((gqa_qn_g, GQA_HEAD_DIM ** -0.5 * LOG2E), (gqa_kn_g, 1.0),
                 (diff_qn_g, DIFF_HEAD_DIM ** -0.5 * LOG2E), (diff_kn_g, 1.0)):
        a_parts.append(jnp.tile(g[None, :] * cos * c, (1, 2)))
        b_parts.append(jnp.tile(_partner_cols(g[None, :], 64) * sin * c, (1, 2)))
    return dict(g1=norm1_g.reshape(1, -1), w1=w1,
                a=jnp.concatenate(a_parts, axis=1), b=jnp.concatenate(b_parts, axis=1))


def _na_table_kernel(w_ref, o_ref):
    row = lax.broadcasted_iota(jnp.int32, (GRID_W, LANES), 0)
    lane = lax.broadcasted_iota(jnp.int32, (GRID_W, LANES), 1)
    kc = jnp.where(lane < GRID_W, lane, lane - GRID_W)
    c0 = jnp.clip(row - NA_COLS // 2, 0, GRID_W - NA_COLS)
    band = (kc >= c0) & (kc < c0 + NA_COLS)
    shift = LANES - (GRID_W - 1)
    for pair in range(NA_HEADS // 2):
        for hf in range(2):
            for a in range(NA_Q_ROWS):
                for jp in range(NA_WIN_ROWS // 2):
                    n0 = a * NA_WIN_ROWS + 2 * jp
                    lo = jnp.broadcast_to(w_ref[pair, hf, 0, n0:n0 + 1, :], (GRID_W, LANES))
                    hi = jnp.broadcast_to(w_ref[pair, hf, 0, n0 + 1:n0 + 2, :], (GRID_W, LANES))
                    lo = pltpu.roll(lo, shift, 1, stride=1, stride_axis=0)
                    hi = pltpu.roll(hi, (shift + GRID_W) % LANES, 1, stride=1, stride_axis=0)
                    r0 = (hf * NA_Q_ROWS + a) * GRID_W
                    o_ref[0, pair, r0:r0 + GRID_W, jp * LANES:(jp + 1) * LANES] = jnp.where(
                        band, jnp.where(lane < GRID_W, lo, hi), NEG_BIG)


def _na_table(rpb):
    us = np.array(NA_VARIANT_UNITS)
    n_var = len(us)
    w0 = np.minimum(np.clip(NA_Q_ROWS * us - NA_ROWS_MAX // 2, 0, GRID_ROWS - NA_ROWS_MAX), GRID_ROWS - NA_WIN_ROWS)
    r = NA_Q_ROWS * us[:, None, None] + np.arange(NA_Q_ROWS)[None, :, None]
    key_r = w0[:, None, None] + np.arange(NA_WIN_ROWS)[None, None, :]
    r0 = np.clip(r - NA_ROWS_MAX // 2, 0, GRID_ROWS - NA_ROWS_MAX)
    row_valid = (key_r >= r0) & (key_r < r0 + NA_ROWS_MAX)
    off_r = np.clip(key_r - r + (NA_ROWS_MAX - 1), 0, 2 * NA_ROWS_MAX - 2)
    pick = (off_r.reshape(-1, 1) == np.arange(2 * NA_ROWS_MAX - 1)[None, :]).astype(np.float32)
    rows = jnp.einsum('nr,hrd->hnd', pick, rpb, precision=lax.Precision.HIGHEST) * LOG2E
    pad = GRID_W - NA_COLS
    w = jnp.pad(rows, ((0, 0), (0, 0), (pad, LANES - pad - (2 * NA_COLS - 1))))
    w = jnp.where(jnp.asarray(row_valid.reshape(1, -1, 1)), w, NEG_BIG)
    w = w.reshape(NA_HEADS // 2, 2, n_var, NA_Q_ROWS * NA_WIN_ROWS, LANES)
    return pl.pallas_call(
        _na_table_kernel,
        grid=(n_var,),
        in_specs=[pl.BlockSpec((NA_HEADS // 2, 2, 1, NA_Q_ROWS * NA_WIN_ROWS, LANES), lambda v: (0, 0, v, 0, 0))],
        out_specs=pl.BlockSpec((1, NA_HEADS // 2, 2 * NA_TQ, NA_WIN), lambda v: (v, 0, 0, 0)),
        out_shape=jax.ShapeDtypeStruct((n_var, NA_HEADS // 2, 2 * NA_TQ, NA_WIN), F32),
        compiler_params=_params(1),
        name="na_bias_table",
    )(w)


def _ffn_params(norm2_g, w_out, w_gate, w_up, w_down, pair_gqa=False):
    half = w_out.shape[0] // 2
    wo1 = _pair_heads(w_out[:half], 0) if pair_gqa else w_out[:half]
    return dict(g2=norm2_g.reshape(1, -1), wo1=wo1.astype(BF16), wo2=w_out[half:].astype(BF16),
                wg=w_gate.astype(BF16), wu=w_up.astype(BF16), wd=w_down.astype(BF16))


def _mod_rows(mod_all, bsz):
    lat = mod_all[:bsz].reshape(bsz, 1, 6, D_MODEL)
    cx = jnp.broadcast_to(mod_all[bsz].reshape(1, 1, 6, D_MODEL), (bsz, 1, 6, D_MODEL))
    return jnp.concatenate([lat, cx], axis=1)


def kernel(x, c, ctx, c_ctx, l0_ada_w, l0_ada_b, l0_norm1_g, l0_norm2_g, l0_w_in, l0_mla_qa_g, l0_mla_w_uq, l0_mla_kva_g, l0_mla_w_ukv, l0_mla_qn_g, l0_mla_kn_g, l0_na_qn_g, l0_na_kn_g, l0_na_rpb, l0_w_out, l0_ffn_w_gate, l0_ffn_w_up, l0_ffn_w_down, l1_ada_w, l1_ada_b, l1_norm1_g, l1_norm2_g, l1_w_in, l1_gqa_qn_g, l1_gqa_kn_g, l1_gqa_sink, l1_diff_qn_g, l1_diff_kn_g, l1_diff_lq1, l1_diff_lk1, l1_diff_lq2, l1_diff_lk2, l1_diff_subln_g, l1_w_out, l1_ffn_w_gate, l1_ffn_w_up, l1_ffn_w_down):
    bsz = x.shape[0]
    assert x.shape[1:] == (SEQ, D_MODEL) and ctx.shape[1:] == (CTX_LEN, D_MODEL)
    rows = -(-(bsz + 1) // 8) * 8
    cond = jnp.concatenate([c, c_ctx[None, :], jnp.zeros((rows - bsz - 1, D_MODEL), F32)], axis=0)
    mod0 = _mod_rows(_ada(cond, l0_ada_w, l0_ada_b), bsz)
    mod1 = _mod_rows(_ada(cond, l1_ada_w, l1_ada_b), bsz)

    p0 = _layer0_params(l0_w_in, l0_mla_qa_g, l0_mla_w_uq, l0_mla_kva_g, l0_mla_w_ukv, l0_mla_qn_g,
                        l0_mla_kn_g, l0_na_qn_g, l0_na_kn_g, l0_norm1_g)
    qa, ka, va, qb, kb, vb = _proj0(x, ctx, mod0, p0)
    o_a = _mla(qa, ka, va)
    o_b = _na(qb, kb, vb, _na_table(l0_na_rpb))
    f0 = _ffn_params(l0_norm2_g, l0_w_out, l0_ffn_w_gate, l0_ffn_w_up, l0_ffn_w_down)
    xa = _ffn((x, ctx), o_a, o_b, mod0, f0, N_TILES)

    p1 = _layer1_params(l1_w_in, l1_gqa_qn_g, l1_gqa_kn_g, l1_diff_qn_g, l1_diff_kn_g, l1_norm1_g)
    qc, kc, vc, qd, kd, vd = _proj1(xa, mod1, p1)
    o_c = _gqa(l1_gqa_sink, qc, kc, vc)
    lv = jnp.zeros((8, LANES), F32).at[:4, :DIFF_HEAD_DIM].set(
        jnp.stack([l1_diff_lq1, l1_diff_lk1, l1_diff_lq2, l1_diff_lk2]))
    lam_init = 0.8 - 0.6 * math.exp(-0.3 * 1)
    o_d = _diff(lv, qd, kd, vd, l1_diff_subln_g.reshape(1, -1), lam_init)
    f1 = _ffn_params(l1_norm2_g, l1_w_out, l1_ffn_w_gate, l1_ffn_w_up, l1_ffn_w_down, pair_gqa=True)
    return _ffn((xa,), o_c, o_d, mod1, f1, N_LAT_TILES)
```

```python
import functools
import math

import numpy as np
import jax
import jax.numpy as jnp
from jax import lax
from jax.experimental import pallas as pl
from jax.experimental.pallas import tpu as pltpu

F32 = jnp.float32
BF16 = jnp.bfloat16

D_MODEL = 1024
SEQ = 2048
GRID_W = 64
CTX_LEN = 256
N_TOK = SEQ + CTX_LEN
ROPE_THETA = 10000.0
EPS = 1e-6

MLA_HEADS = 8
MLA_Q_RANK = 256
MLA_KV_RANK = 128
MLA_NOPE = 64
MLA_ROPE = 32
MLA_V = 64
MLA_QK = MLA_NOPE + MLA_ROPE

NA_HEADS = 8
NA_HEAD_DIM = 64
NA_ROWS_MAX = 8
NA_COLS = 16
NA_DIM = NA_HEADS * NA_HEAD_DIM

GQA_HEADS = 8
GQA_KV_HEADS = 2
GQA_HEAD_DIM = 64
GQA_WINDOW = 128

DIFF_HEADS = 4
DIFF_HEAD_DIM = 64

MLA_VDIM = MLA_HEADS * MLA_V
GQA_Q = GQA_HEADS * GQA_HEAD_DIM
GQA_KV = GQA_KV_HEADS * GQA_HEAD_DIM
DIFF_DIM = DIFF_HEADS * 2 * DIFF_HEAD_DIM

LANES = 128
MLA_SLOTS = MLA_HEADS * LANES
TOK_TILE = 256
N_TILES = N_TOK // TOK_TILE
N_LAT_TILES = SEQ // TOK_TILE
VMEM_LIMIT = 56 * 1024 * 1024
LOG2E = 1.4426950408889634
NEG_BIG = -1e30

ATT_TQ = 256
MLA_TQ = 256
NA_Q_ROWS = 2
NA_TQ = NA_Q_ROWS * GRID_W
NA_WIN_ROWS = 10
NA_WIN = NA_WIN_ROWS * GRID_W
NA_STEPS = SEQ // NA_TQ
NA_VARIANT_UNITS = (0, 1, 2, NA_STEPS - 2, NA_STEPS - 1)
GRID_ROWS = SEQ // GRID_W
GQA_TQ = 128
GQA_BAND = GQA_TQ + 2 * GQA_WINDOW


def _dot(a, b):
    return jnp.dot(a, b, preferred_element_type=F32)


def _dot_nt(a, b):
    return lax.dot_general(a, b, (((1,), (1,)), ((), ())), preferred_element_type=F32)


def _rms_rows(x):
    return lax.rsqrt(jnp.mean(x * x, axis=-1, keepdims=True) + EPS)


def _const_spec(a):
    nd = a.ndim
    return pl.BlockSpec(a.shape, lambda *_: (0,) * nd)


def _params(n_grid):
    return pltpu.CompilerParams(dimension_semantics=("arbitrary",) * n_grid,
                                vmem_limit_bytes=VMEM_LIMIT)


def _fold_lanes(xs, op):
    r = None
    for x in xs:
        for lo in range(0, x.shape[1], LANES):
            blk = x[:, lo:lo + LANES]
            r = blk if r is None else op(r, blk)
    return r


def _scores(q, keys, biases, s_ref, mx_ref, keys_t=False):
    parts = []
    lo = 0
    for k, bias in zip(keys, biases):
        sc = _dot(q, k) if keys_t else _dot_nt(q, k)
        if bias is not None:
            sc = sc + bias
        s_ref[:, lo:lo + sc.shape[1]] = sc
        lo += sc.shape[1]
        parts.append(sc)
    mx_ref[...] = _fold_lanes(parts, jnp.maximum)


def _softmax_pv(s_ref, mx_ref, vals, extra=None):
    m = jnp.max(mx_ref[...], axis=-1, keepdims=True)
    if extra is not None:
        m = jnp.maximum(m, extra)
    ps = []
    lo = 0
    for v in vals:
        ps.append(jnp.exp2(s_ref[:, lo:lo + v.shape[0]] - m))
        lo += v.shape[0]
    l = jnp.sum(_fold_lanes(ps, jnp.add), axis=-1, keepdims=True)
    if extra is not None:
        l = l + jnp.exp2(extra - m)
    acc = _dot(ps[0].astype(BF16), vals[0])
    for p, v in zip(ps[1:], vals[1:]):
        acc = acc + _dot(p.astype(BF16), v)
    return acc, l


def _run_skewed(units):
    units[0][0](0)
    for i, (_, finish) in enumerate(units):
        if i + 1 < len(units):
            units[i + 1][0]((i + 1) % 2)
        finish(i % 2)


def _row0(i, size):
    return i * size if isinstance(i, int) else pl.multiple_of(i * size, size)


def _skewed_loop(n, scores, finish, per_trip=8):
    assert n % per_trip == 0 and per_trip % 2 == 0
    scores(0, 0)

    def body(j, carry):
        for d in range(per_trip):
            i = per_trip * j + d
            scores(jnp.minimum(i + 1, n - 1), (d + 1) % 2)
            finish(i, d % 2)
        return carry

    lax.fori_loop(0, n // per_trip, body, 0)


def _att_scratch(lead, rows, n_keys):
    lead = (2,) + tuple(lead)
    return [pltpu.VMEM(lead + (rows, n_keys), F32), pltpu.VMEM(lead + (rows, LANES), F32)]


def _stack_halves(q):
    lane = lax.broadcasted_iota(jnp.int32, q.shape, 1)
    zero = jnp.zeros_like(q)
    return jnp.concatenate([jnp.where(lane < 64, q, zero), jnp.where(lane >= 64, q, zero)], axis=0)


def _merge_halves(lo, hi):
    lane = lax.broadcasted_iota(jnp.int32, lo.shape, 1)
    return jnp.where(lane < 64, lo, hi)


def _ada_kernel(c_ref, w_ref, b_ref, o_ref):
    c = c_ref[...]
    a = (c * (1.0 / (1.0 + jnp.exp(-c)))).astype(BF16)
    o_ref[...] = _dot(a, w_ref[...].astype(BF16)) + b_ref[...]


def _ada(cond, w, b):
    n = w.shape[1]
    tn = D_MODEL
    return pl.pallas_call(
        _ada_kernel,
        grid=(n // tn,),
        in_specs=[pl.BlockSpec(cond.shape, lambda j: (0, 0)),
                  pl.BlockSpec((w.shape[0], tn), lambda j: (0, j)),
                  pl.BlockSpec((1, tn), lambda j: (0, j))],
        out_specs=pl.BlockSpec((cond.shape[0], tn), lambda j: (0, j)),
        out_shape=jax.ShapeDtypeStruct((cond.shape[0], n), F32),
        compiler_params=_params(1),
        name="ada_modulation",
    )(cond, w, b.reshape(1, n))


def _batch_tile(bsz, most=2):
    return max(bt for bt in (1, 2, 4) if bt <= most and bsz % bt == 0)


def _tok_spec(bt, width):
    return pl.BlockSpec((bt, TOK_TILE, width), lambda t, b: (b, t, 0))


def _lat_spec(bt):
    return pl.BlockSpec((bt, TOK_TILE, D_MODEL), lambda t, b: (b, jnp.minimum(t, N_LAT_TILES - 1), 0))


def _ctx_spec(bt):
    return pl.BlockSpec((bt, TOK_TILE, D_MODEL), lambda t, b: (b, 0, 0))


def _mod_spec(bt):
    return pl.BlockSpec((bt, 1, 6, D_MODEL), lambda t, b: (b, t // N_LAT_TILES, 0, 0))


def _table_spec(width):
    return pl.BlockSpec((TOK_TILE, width), lambda t, b: (t, 0))


def _read_tokens(x_ref, c_ref):
    return jnp.where(pl.program_id(0) < N_LAT_TILES, x_ref[...], c_ref[...])


def _mod_row(mod_ref, i):
    return mod_ref[:, 0, i:i + 1, :]


def _rows_dot(a, w):
    bt, rows, k = a.shape
    return _dot(a.reshape(bt * rows, k), w).reshape(bt, rows, w.shape[1])


W0_CQ = 0
W0_CKV = W0_CQ + MLA_Q_RANK
W0_KR = W0_CKV + MLA_KV_RANK
W0_KRP = W0_KR + LANES
W0_QN = W0_KRP + LANES
W0_KN = W0_QN + NA_DIM
W0_VN = W0_KN + NA_DIM
W0_END = W0_VN + NA_DIM


def _head_inv(z, n):
    return lax.rsqrt(jnp.sum(z * z, axis=-1, keepdims=True) * (1.0 / n) + EPS)


def _half_inv(z):
    z2 = z * z
    lane = lax.broadcasted_iota(jnp.int32, z.shape, z.ndim - 1)
    lo = jnp.sum(jnp.where(lane < 64, z2, 0.0), axis=-1, keepdims=True)
    hi = jnp.sum(jnp.where(lane < 64, 0.0, z2), axis=-1, keepdims=True)
    return jnp.where(lane < 64, lax.rsqrt(lo * (1.0 / 64.0) + EPS), lax.rsqrt(hi * (1.0 / 64.0) + EPS))


def _proj0_kernel(x_ref, c_ref, mod_ref, g1_ref, w0_ref, qag_ref, wuq_ref, kvag_ref, wukv_ref,
                  aq_ref, bq_ref, ak_ref, bk_ref, gqb_ref, gkb_ref,
                  qa_ref, ka_ref, va_ref, qb_ref, kb_ref, vb_ref):
    x = _read_tokens(x_ref, c_ref)
    h = x * _rms_rows(x) * g1_ref[...] * (1.0 + _mod_row(mod_ref, 1)) + _mod_row(mod_ref, 0)
    z = _rows_dot(h.astype(BF16), w0_ref[...])

    cq = z[:, :, W0_CQ:W0_CKV]
    cqn = (cq * _rms_rows(cq) * qag_ref[...]).astype(BF16)
    zq = _rows_dot(cqn, wuq_ref[...])
    aq, bq = aq_ref[...], bq_ref[...]
    for hd in range(MLA_HEADS):
        qm = zq[:, :, hd * LANES:(hd + 1) * LANES]
        qr = zq[:, :, MLA_SLOTS + hd * LANES:MLA_SLOTS + (hd + 1) * LANES]
        qa_ref[:, :, hd * LANES:(hd + 1) * LANES] = ((qm * aq + qr * bq) * _head_inv(qm, MLA_QK)).astype(BF16)

    ckv = z[:, :, W0_CKV:W0_KR]
    ckvn = (ckv * _rms_rows(ckv) * kvag_ref[...]).astype(BF16)
    zkv = _rows_dot(ckvn, wukv_ref[...])
    kr = z[:, :, W0_KR:W0_KRP]
    kr_rot = z[:, :, W0_KRP:W0_QN] * bk_ref[...]
    ak = ak_ref[...]
    for hd in range(MLA_HEADS):
        km = zkv[:, :, hd * LANES:(hd + 1) * LANES] + kr
        ka_ref[:, :, hd * LANES:(hd + 1) * LANES] = ((km * ak + kr_rot) * _head_inv(km, MLA_QK)).astype(BF16)
    va_ref[...] = zkv[:, :, MLA_SLOTS:].astype(BF16)

    for j in range(NA_DIM // LANES):
        cols = slice(j * LANES, (j + 1) * LANES)
        qn = z[:, :, W0_QN + j * LANES:W0_QN + (j + 1) * LANES]
        kn = z[:, :, W0_KN + j * LANES:W0_KN + (j + 1) * LANES]
        qb_ref[:, :, cols] = (qn * gqb_ref[...] * _half_inv(qn)).astype(BF16)
        kb_ref[:, :, cols] = (kn * gkb_ref[...] * _half_inv(kn)).astype(BF16)
    vb_ref[...] = z[:, :, W0_VN:W0_END].astype(BF16)


def _proj0(x, ctx, mod, p):
    bsz = x.shape[0]
    bt = _batch_tile(bsz, most=4)
    consts1 = [p['g1'], p['w0'], p['qag'], p['wuq'], p['kvag'], p['wukv']]
    tables = [p['aq'], p['bq'], p['ak'], p['bk']]
    consts2 = [p['gqb'], p['gkb']]
    widths = [MLA_SLOTS, MLA_SLOTS, MLA_VDIM, NA_DIM, NA_DIM, NA_DIM]
    return pl.pallas_call(
        _proj0_kernel,
        grid=(N_TILES, bsz // bt),
        in_specs=([_lat_spec(bt), _ctx_spec(bt), _mod_spec(bt)] + [_const_spec(a) for a in consts1]
                  + [_table_spec(LANES) for _ in tables] + [_const_spec(a) for a in consts2]),
        out_specs=[_tok_spec(bt, w) for w in widths],
        out_shape=[jax.ShapeDtypeStruct((bsz, N_TOK, w), BF16) for w in widths],
        compiler_params=_params(2),
        name="proj_layer0",
    )(x, ctx, mod, *consts1, *tables, *consts2)


N1_SEG = tuple(w // LANES for w in (GQA_Q, GQA_KV, DIFF_DIM, DIFF_DIM))
N1_V_GQA = sum(N1_SEG) * LANES
N1_V_DIFF = N1_V_GQA + GQA_KV
N1_MAIN = N1_V_DIFF + DIFF_DIM


def _proj1_kernel(x_ref, mod_ref, g1_ref, w1_ref, a_ref, b_ref,
                  qc_ref, kc_ref, vc_ref, qd_ref, kd_ref, vd_ref):
    x = x_ref[...]
    h = x * _rms_rows(x) * g1_ref[...] * (1.0 + _mod_row(mod_ref, 1)) + _mod_row(mod_ref, 0)
    z = _rows_dot(h.astype(BF16), w1_ref[...])
    outs = (qc_ref, kc_ref, qd_ref, kd_ref)
    blk = 0
    for seg, n_blk in enumerate(N1_SEG):
        a = a_ref[:, seg * LANES:(seg + 1) * LANES]
        b = b_ref[:, seg * LANES:(seg + 1) * LANES]
        for j in range(n_blk):
            zm = z[:, :, blk * LANES:(blk + 1) * LANES]
            rot = z[:, :, N1_MAIN + blk * LANES:N1_MAIN + (blk + 1) * LANES]
            outs[seg][:, :, j * LANES:(j + 1) * LANES] = ((zm * a + rot * b) * _half_inv(zm)).astype(BF16)
            blk += 1
    vc_ref[...] = z[:, :, N1_V_GQA:N1_V_DIFF].astype(BF16)
    vd_ref[...] = z[:, :, N1_V_DIFF:N1_MAIN].astype(BF16)


def _proj1(xa, mod, p):
    bsz = xa.shape[0]
    bt = _batch_tile(bsz, most=4)
    consts = [p['g1'], p['w1']]
    widths = [GQA_Q, GQA_KV, GQA_KV, DIFF_DIM, DIFF_DIM, DIFF_DIM]
    return pl.pallas_call(
        _proj1_kernel,
        grid=(N_TILES, bsz // bt),
        in_specs=([_tok_spec(bt, D_MODEL), _mod_spec(bt)] + [_const_spec(a) for a in consts]
                  + [_table_spec(4 * LANES), _table_spec(4 * LANES)]),
        out_specs=[_tok_spec(bt, w) for w in widths],
        out_shape=[jax.ShapeDtypeStruct((bsz, N_TOK, w), BF16) for w in widths],
        compiler_params=_params(2),
        name="proj_layer1",
    )(xa, mod, *consts, p['a'], p['b'])


def _mla_kernel(q_ref, k_ref, v_ref, o_ref, s_scr, mx_scr, kt_scr, vx_scr):
    vx_scr[:, :LANES] = v_ref[0]
    vx_scr[:, LANES:] = jnp.ones((N_TOK, LANES), BF16)
    for hh in range(2):
        kt_scr[hh] = k_ref[0, :, hh * LANES:(hh + 1) * LANES].T

    def scores(r0, slot, key_lo):
        for hh in range(2):
            cols = slice(hh * LANES, (hh + 1) * LANES)
            rows = slice(hh * MLA_TQ, (hh + 1) * MLA_TQ)
            _scores(q_ref[0, r0:r0 + MLA_TQ, cols], [kt_scr[hh, :, key_lo:]], [None],
                    s_scr.at[slot, rows], mx_scr.at[slot, rows], keys_t=True)

    def finish(r0, slot, key_lo):
        m = jnp.max(mx_scr[slot], axis=-1, keepdims=True)
        p = jnp.exp2(s_scr[slot, :, 0:N_TOK - key_lo] - m).astype(BF16)
        acc = _dot(p, vx_scr[key_lo:, :])
        r = acc[:, :LANES] / acc[:, LANES:LANES + 1]
        o_ref[0, r0:r0 + MLA_TQ, :] = _merge_halves(r[:MLA_TQ], r[MLA_TQ:]).astype(BF16)

    units = [(functools.partial(scores, r0, key_lo=0), functools.partial(finish, r0, key_lo=0))
             for r0 in range(0, SEQ, MLA_TQ)]
    units.append((functools.partial(scores, SEQ, key_lo=SEQ),
                  functools.partial(finish, SEQ, key_lo=SEQ)))
    _run_skewed(units)


def _mla(qa, ka, va):
    bsz = qa.shape[0]
    assert CTX_LEN == MLA_TQ
    return pl.pallas_call(
        _mla_kernel,
        grid=(bsz, MLA_HEADS // 2),
        in_specs=[pl.BlockSpec((1, N_TOK, 2 * LANES), lambda b, j: (b, 0, j)),
                  pl.BlockSpec((1, N_TOK, 2 * LANES), lambda b, j: (b, 0, j)),
                  pl.BlockSpec((1, N_TOK, LANES), lambda b, j: (b, 0, j))],
        out_specs=pl.BlockSpec((1, N_TOK, LANES), lambda b, j: (b, 0, j)),
        out_shape=jax.ShapeDtypeStruct((bsz, N_TOK, MLA_HEADS * MLA_V), BF16),
        scratch_shapes=(_att_scratch((), 2 * MLA_TQ, N_TOK)
                        + [pltpu.VMEM((2, LANES, N_TOK), BF16), pltpu.VMEM((N_TOK, 2 * LANES), BF16)]),
        compiler_params=_params(2),
        name="mla_attention",
    )(qa, ka, va)


def _na_kernel(q_ref, k_ref, v_ref, t_ref, o_ref, s_scr, mx_scr):
    n_pairs = NA_HEADS // 2

    def window_start(u):
        w0 = jnp.minimum(jnp.clip(NA_Q_ROWS * u - NA_ROWS_MAX // 2, 0, GRID_ROWS - NA_ROWS_MAX),
                         GRID_ROWS - NA_WIN_ROWS)
        return pl.multiple_of(w0 * GRID_W, GRID_W)

    def scores(u, slot):
        ws = window_start(u)
        r0 = _row0(u, NA_TQ)
        var = jnp.minimum(u, 2) + jnp.maximum(u - (NA_STEPS - 3), 0)
        for j in range(n_pairs):
            cols = slice(j * LANES, (j + 1) * LANES)
            qq = _stack_halves(q_ref[0, pl.ds(r0, NA_TQ), cols])
            _scores(qq, [k_ref[0, pl.ds(ws, NA_WIN), cols], k_ref[0, SEQ:, cols]], [t_ref[var, j], None],
                    s_scr.at[slot, j], mx_scr.at[slot, j])

    def finish(u, slot):
        ws = window_start(u)
        r0 = _row0(u, NA_TQ)
        for j in range(n_pairs):
            cols = slice(j * LANES, (j + 1) * LANES)
            acc, l = _softmax_pv(s_scr.at[slot, j], mx_scr.at[slot, j],
                                 [v_ref[0, pl.ds(ws, NA_WIN), cols], v_ref[0, SEQ:, cols]])
            r = acc / l
            o_ref[0, pl.ds(r0, NA_TQ), cols] = _merge_halves(r[:NA_TQ], r[NA_TQ:]).astype(BF16)

    _skewed_loop(NA_STEPS, scores, finish)

    def ctx_scores(r0, slot):
        for j in range(n_pairs):
            cols = slice(j * LANES, (j + 1) * LANES)
            qq = _stack_halves(q_ref[0, r0:r0 + NA_TQ, cols])
            _scores(qq, [k_ref[0, SEQ:, cols]], [None], s_scr.at[slot, j], mx_scr.at[slot, j])

    def ctx_finish(r0, slot):
        for j in range(n_pairs):
            cols = slice(j * LANES, (j + 1) * LANES)
            acc, l = _softmax_pv(s_scr.at[slot, j], mx_scr.at[slot, j], [v_ref[0, SEQ:, cols]])
            r = acc / l
            o_ref[0, r0:r0 + NA_TQ, cols] = _merge_halves(r[:NA_TQ], r[NA_TQ:]).astype(BF16)

    _run_skewed([(functools.partial(ctx_scores, r0), functools.partial(ctx_finish, r0))
                 for r0 in range(SEQ, N_TOK, NA_TQ)])


def _na(qb, kb, vb, table):
    bsz = qb.shape[0]
    tok = pl.BlockSpec((1, N_TOK, NA_DIM), lambda b: (b, 0, 0))
    n_pairs = NA_HEADS // 2
    return pl.pallas_call(
        _na_kernel,
        grid=(bsz,),
        in_specs=[tok, tok, tok, _const_spec(table)],
        out_specs=tok,
        out_shape=jax.ShapeDtypeStruct((bsz, N_TOK, NA_DIM), BF16),
        scratch_shapes=_att_scratch((n_pairs,), 2 * NA_TQ, NA_WIN + CTX_LEN),
        compiler_params=_params(1),
        name="neighbourhood_attention",
    )(qb, kb, vb, table)


def _gqa_kernel(sink_ref, q_ref, k_ref, v_ref, o_ref, s_scr, mx_scr):
    row = lax.broadcasted_iota(jnp.int32, (2 * GQA_TQ, GQA_BAND), 0)
    col = lax.broadcasted_iota(jnp.int32, (2 * GQA_TQ, GQA_BAND), 1)
    rel = jnp.where(row >= GQA_TQ, row - GQA_TQ, row) - col
    hi_rows = lax.broadcasted_iota(jnp.int32, (2 * GQA_TQ, 1), 0) >= GQA_TQ

    def band_start(r0):
        return min(max(r0 - GQA_WINDOW, 0), SEQ - GQA_BAND)

    def scores(n, slot):
        r0 = n * GQA_TQ
        start = band_start(r0)
        mask = jnp.where(jnp.abs(rel + (r0 - start)) <= GQA_WINDOW, 0.0, NEG_BIG)
        keys = [k_ref[0, start:start + GQA_BAND, :], k_ref[0, SEQ:, :]]
        for j in range(GQA_HEADS // 2):
            qq = _stack_halves(q_ref[0, r0:r0 + GQA_TQ, j * LANES:(j + 1) * LANES])
            _scores(qq, keys, [mask, None], s_scr.at[slot, j], mx_scr.at[slot, j])

    def finish(n, slot):
        r0 = n * GQA_TQ
        start = band_start(r0)
        vals = [v_ref[0, start:start + GQA_BAND, :], v_ref[0, SEQ:, :]]
        for j in range(GQA_HEADS // 2):
            sink = jnp.where(hi_rows, sink_ref[j + GQA_HEADS // 2], sink_ref[j]) * LOG2E
            acc, l = _softmax_pv(s_scr.at[slot, j], mx_scr.at[slot, j], vals, extra=sink)
            r = acc / l
            o_ref[0, r0:r0 + GQA_TQ, j * LANES:(j + 1) * LANES] = _merge_halves(r[:GQA_TQ], r[GQA_TQ:]).astype(BF16)

    _run_skewed([(functools.partial(scores, n), functools.partial(finish, n)) for n in range(SEQ // GQA_TQ)])


def _gqa(sink, qc, kc, vc):
    bsz = qc.shape[0]
    return pl.pallas_call(
        _gqa_kernel,
        grid=(bsz,),
        in_specs=[pl.BlockSpec(memory_space=pltpu.SMEM),
                  pl.BlockSpec((1, SEQ, GQA_Q), lambda b: (b, 0, 0)),
                  pl.BlockSpec((1, N_TOK, GQA_KV), lambda b: (b, 0, 0)),
                  pl.BlockSpec((1, N_TOK, GQA_KV), lambda b: (b, 0, 0))],
        out_specs=pl.BlockSpec((1, SEQ, GQA_Q), lambda b: (b, 0, 0)),
        out_shape=jax.ShapeDtypeStruct((bsz, SEQ, GQA_Q), BF16),
        scratch_shapes=_att_scratch((GQA_HEADS // 2,), 2 * GQA_TQ, GQA_BAND + CTX_LEN),
        compiler_params=_params(1),
        name="windowed_gqa",
    )(sink, qc, kc, vc)


def _diff_kernel(lv_ref, q_ref, k_ref, v_ref, g_ref, o_ref, s_scr, mx_scr, kt_scr, vx_scr, *, lam_init):
    kt_scr[...] = k_ref[0].T
    vx_scr[:, :LANES] = v_ref[0]
    vx_scr[:, LANES:] = jnp.ones((N_TOK, LANES), BF16)
    lv = lv_ref[...]
    lam = (jnp.exp(jnp.sum(lv[0:1] * lv[1:2], axis=-1, keepdims=True))
           - jnp.exp(jnp.sum(lv[2:3] * lv[3:4], axis=-1, keepdims=True)) + lam_init)
    hi_rows = lax.broadcasted_iota(jnp.int32, (2 * ATT_TQ, 1), 0) >= ATT_TQ
    coef = jnp.where(hi_rows, -lam, 1.0)
    g = g_ref[...] * (1.0 - lam_init)

    def scores(i, slot):
        qq = _stack_halves(q_ref[0, i * ATT_TQ:(i + 1) * ATT_TQ, :])
        _scores(qq, [kt_scr[...]], [None], s_scr.at[slot], mx_scr.at[slot], keys_t=True)

    def finish(i, slot):
        m = jnp.max(mx_scr[slot], axis=-1, keepdims=True)
        p = jnp.exp2(s_scr[slot] - m).astype(BF16)
        acc = _dot(p, vx_scr[...])
        r = acc[:, :LANES] * (coef / acc[:, LANES:LANES + 1])
        o = r[:ATT_TQ] + r[ATT_TQ:]
        o_ref[0, i * ATT_TQ:(i + 1) * ATT_TQ, :] = (o * _rms_rows(o) * g).astype(BF16)

    _run_skewed([(functools.partial(scores, i), functools.partial(finish, i)) for i in range(SEQ // ATT_TQ)])


def _diff(lv, qd, kd, vd, g, lam_init):
    bsz = qd.shape[0]
    return pl.pallas_call(
        functools.partial(_diff_kernel, lam_init=lam_init),
        grid=(bsz, DIFF_HEADS),
        in_specs=[_const_spec(lv),
                  pl.BlockSpec((1, SEQ, LANES), lambda b, h: (b, 0, h)),
                  pl.BlockSpec((1, N_TOK, LANES), lambda b, h: (b, 0, h)),
                  pl.BlockSpec((1, N_TOK, LANES), lambda b, h: (b, 0, h)),
                  _const_spec(g)],
        out_specs=pl.BlockSpec((1, SEQ, LANES), lambda b, h: (b, 0, h)),
        out_shape=jax.ShapeDtypeStruct((bsz, SEQ, DIFF_HEADS * 2 * DIFF_HEAD_DIM), BF16),
        scratch_shapes=(_att_scratch((), 2 * ATT_TQ, N_TOK)
                        + [pltpu.VMEM((LANES, N_TOK), BF16), pltpu.VMEM((N_TOK, 2 * LANES), BF16)]),
        compiler_params=_params(2),
        name="diff_attention",
    )(lv, qd, kd, vd, g)


def _ffn_body(x, o1_ref, o2_ref, mod_ref, g2_ref, wo1_ref, wo2_ref, wg_ref, wu_ref, wd_ref, y_ref):
    attn = _rows_dot(o1_ref[...], wo1_ref[...]) + _rows_dot(o2_ref[...], wo2_ref[...])
    x1 = x + _mod_row(mod_ref, 2) * attn
    h = (x1 * _rms_rows(x1) * g2_ref[...] * (1.0 + _mod_row(mod_ref, 4)) + _mod_row(mod_ref, 3)).astype(BF16)
    gate = _rows_dot(h, wg_ref[...])
    up = _rows_dot(h, wu_ref[...])
    a = (gate * (1.0 / (1.0 + jnp.exp(-gate))) * up).astype(BF16)
    y_ref[...] = x1 + _mod_row(mod_ref, 5) * _rows_dot(a, wd_ref[...])


def _ffn0_kernel(x_ref, c_ref, *rest):
    _ffn_body(_read_tokens(x_ref, c_ref), *rest)


def _ffn1_kernel(x_ref, *rest):
    _ffn_body(x_ref[...], *rest)


def _ffn(xs, o1, o2, mod, p, n_tiles):
    bsz = xs[0].shape[0]
    bt = _batch_tile(bsz)
    consts = [p['g2'], p['wo1'], p['wo2'], p['wg'], p['wu'], p['wd']]
    x_specs = [_lat_spec(bt), _ctx_spec(bt)] if len(xs) == 2 else [_tok_spec(bt, D_MODEL)]
    return pl.pallas_call(
        _ffn0_kernel if len(xs) == 2 else _ffn1_kernel,
        grid=(n_tiles, bsz // bt),
        in_specs=(x_specs + [_tok_spec(bt, o1.shape[-1]), _tok_spec(bt, o2.shape[-1]), _mod_spec(bt)]
                  + [_const_spec(a) for a in consts]),
        out_specs=_tok_spec(bt, D_MODEL),
        out_shape=jax.ShapeDtypeStruct((bsz, n_tiles * TOK_TILE, D_MODEL), F32),
        compiler_params=_params(2),
        name="outproj_ffn",
    )(*xs, o1, o2, mod, *consts)


def _partner_cols(w, dr):
    q = dr // 4
    shp = w.shape
    return jnp.flip(w.reshape(shp[:-1] + (shp[-1] // dr, 2, 2, q)), axis=-2).reshape(shp)


def _rope_tables(dr):
    h = dr // 2
    q = dr // 4
    freqs = ROPE_THETA ** (-jnp.arange(0, h, 2, dtype=F32) / h)
    t = jnp.arange(SEQ)
    pos = jnp.stack([t // GRID_W, t % GRID_W], axis=1).astype(F32)
    ang = (pos[:, :, None, None] * freqs[None, None, None, :])
    ang = jnp.broadcast_to(ang, (SEQ, 2, 2, q)).reshape(SEQ, dr)
    sign = np.tile(np.repeat(np.array([-1.0, 1.0], np.float32), q), 2)
    cos = jnp.concatenate([jnp.cos(ang), jnp.ones((CTX_LEN, dr), F32)], axis=0)
    sin = jnp.concatenate([jnp.sin(ang) * sign, jnp.zeros((CTX_LEN, dr), F32)], axis=0)
    return cos, sin


def _layer0_params(w_in, qa_g, w_uq, kva_g, w_ukv, qn_g, kn_g, na_qn_g, na_kn_g, norm1_g):
    cos, sin = _rope_tables(MLA_ROPE)
    o_kr = MLA_Q_RANK + MLA_KV_RANK
    kr_w = w_in[:, o_kr:o_kr + MLA_ROPE]
    lane_pad = ((0, 0), (MLA_NOPE, LANES - MLA_QK))
    w0 = jnp.concatenate([w_in[:, :o_kr], jnp.pad(kr_w, lane_pad), jnp.pad(_partner_cols(kr_w, MLA_ROPE), lane_pad),
                          w_in[:, o_kr + MLA_ROPE:]], axis=1).astype(BF16)
    r = w_uq.shape[0]
    w3 = w_uq.reshape(r, MLA_HEADS, MLA_QK)
    main = jnp.pad(w3, ((0, 0), (0, 0), (0, LANES - MLA_QK)))
    rot = jnp.pad(_partner_cols(w3[:, :, MLA_NOPE:], MLA_ROPE), ((0, 0), (0, 0), (MLA_NOPE, LANES - MLA_QK)))
    wuq = jnp.concatenate([main.reshape(r, -1), rot.reshape(r, -1)], axis=1).astype(BF16)

    def tables(g, c):
        z32 = jnp.zeros((N_TOK, LANES - MLA_QK), F32)
        a = jnp.concatenate([jnp.broadcast_to(g[None, :MLA_NOPE], (N_TOK, MLA_NOPE)), g[None, MLA_NOPE:] * cos, z32], axis=1)
        b = jnp.concatenate([jnp.zeros((N_TOK, MLA_NOPE), F32), _partner_cols(g[None, MLA_NOPE:], MLA_ROPE) * sin, z32], axis=1)
        return a * c, b * c

    aq, bq = tables(qn_g, MLA_QK ** -0.5 * LOG2E)
    ak, bk = tables(kn_g, 1.0)
    rk = w_ukv.shape[0]
    k3 = w_ukv.reshape(rk, MLA_HEADS, MLA_NOPE + MLA_V)
    wukv = jnp.concatenate([jnp.pad(k3[:, :, :MLA_NOPE], ((0, 0), (0, 0), (0, LANES - MLA_NOPE))).reshape(rk, -1),
                            k3[:, :, MLA_NOPE:].reshape(rk, -1)], axis=1).astype(BF16)
    return dict(
        g1=norm1_g.reshape(1, -1), w0=w0, qag=qa_g.reshape(1, -1), wuq=wuq, kvag=kva_g.reshape(1, -1),
        wukv=wukv, aq=aq, bq=bq, ak=ak, bk=bk,
        gqb=(jnp.tile(na_qn_g, 2) * (NA_HEAD_DIM ** -0.5 * LOG2E)).reshape(1, -1),
        gkb=jnp.tile(na_kn_g, 2).reshape(1, -1))


def _pair_heads(w, axis):
    g = GQA_HEADS // GQA_KV_HEADS
    shp = w.shape
    w4 = w.reshape(shp[:axis] + (GQA_KV_HEADS, g, GQA_HEAD_DIM) + shp[axis + 1:])
    return jnp.swapaxes(w4, axis, axis + 1).reshape(shp)


def _layer1_params(w_in, gqa_qn_g, gqa_kn_g, diff_qn_g, diff_kn_g, norm1_g):
    cos, sin = _rope_tables(64)
    o_vg = GQA_Q + GQA_KV
    o_qd = o_vg + GQA_KV
    o_vd = o_qd + 2 * DIFF_DIM
    main = jnp.concatenate([_pair_heads(w_in[:, :GQA_Q], 1), w_in[:, GQA_Q:o_vg], w_in[:, o_qd:o_vd]], axis=1)
    w1 = jnp.concatenate([main, w_in[:, o_vg:o_qd], w_in[:, o_vd:], _partner_cols(main, GQA_HEAD_DIM)],
                         axis=1).astype(BF16)
    a_parts, b_parts = [], []
    for g, c in ((gqa_qn_g, GQA_HEAD_DIM ** -0.5 * LOG2E), (gqa_kn_g, 1.0),
                 (diff_qn_g, DIFF_HEAD_DIM ** -0.5 * LOG2E), (diff_kn_g, 1.0)):
        a_parts.append(jnp.tile(g[None, :] * cos * c, (1, 2)))
        b_parts.append(jnp.tile(_partner_cols(g[None, :], 64) * sin * c, (1, 2)))
    return dict(g1=norm1_g.reshape(1, -1), w1=w1,
                a=jnp.concatenate(a_parts, axis=1), b=jnp.concatenate(b_parts, axis=1))


def _na_table_kernel(w_ref, o_ref):
    row = lax.broadcasted_iota(jnp.int32, (GRID_W, LANES), 0)
    lane = lax.broadcasted_iota(jnp.int32, (GRID_W, LANES), 1)
    kc = jnp.where(lane < GRID_W, lane, lane - GRID_W)
    c0 = jnp.clip(row - NA_COLS // 2, 0, GRID_W - NA_COLS)
    band = (kc >= c0) & (kc < c0 + NA_COLS)
    shift = LANES - (GRID_W - 1)
    for pair in range(NA_HEADS // 2):
        for hf in range(2):
            for a in range(NA_Q_ROWS):
                for jp in range(NA_WIN_ROWS // 2):
                    n0 = a * NA_WIN_ROWS + 2 * jp
                    lo = jnp.broadcast_to(w_ref[pair, hf, 0, n0:n0 + 1, :], (GRID_W, LANES))
                    hi = jnp.broadcast_to(w_ref[pair, hf, 0, n0 + 1:n0 + 2, :], (GRID_W, LANES))
                    lo = pltpu.roll(lo, shift, 1, stride=1, stride_axis=0)
                    hi = pltpu.roll(hi, (shift + GRID_W) % LANES, 1, stride=1, stride_axis=0)
                    r0 = (hf * NA_Q_ROWS + a) * GRID_W
                    o_ref[0, pair, r0:r0 + GRID_W, jp * LANES:(jp + 1) * LANES] = jnp.where(
                        band, jnp.where(lane < GRID_W, lo, hi), NEG_BIG)


def _na_table(rpb):
    us = np.array(NA_VARIANT_UNITS)
    n_var = len(us)
    w0 = np.minimum(np.clip(NA_Q_ROWS * us - NA_ROWS_MAX // 2, 0, GRID_ROWS - NA_ROWS_MAX), GRID_ROWS - NA_WIN_ROWS)
    r = NA_Q_ROWS * us[:, None, None] + np.arange(NA_Q_ROWS)[None, :, None]
    key_r = w0[:, None, None] + np.arange(NA_WIN_ROWS)[None, None, :]
    r0 = np.clip(r - NA_ROWS_MAX // 2, 0, GRID_ROWS - NA_ROWS_MAX)
    row_valid = (key_r >= r0) & (key_r < r0 + NA_ROWS_MAX)
    off_r = np.clip(key_r - r + (NA_ROWS_MAX - 1), 0, 2 * NA_ROWS_MAX - 2)
    pick = (off_r.reshape(-1, 1) == np.arange(2 * NA_ROWS_MAX - 1)[None, :]).astype(np.float32)
    rows = jnp.einsum('nr,hrd->hnd', pick, rpb, precision=lax.Precision.HIGHEST) * LOG2E
    pad = GRID_W - NA_COLS
    w = jnp.pad(rows, ((0, 0), (0, 0), (pad, LANES - pad - (2 * NA_COLS - 1))))
    w = jnp.where(jnp.asarray(row_valid.reshape(1, -1, 1)), w, NEG_BIG)
    w = w.reshape(NA_HEADS // 2, 2, n_var, NA_Q_ROWS * NA_WIN_ROWS, LANES)
    return pl.pallas_call(
        _na_table_kernel,
        grid=(n_var,),
        in_specs=[pl.BlockSpec((NA_HEADS // 2, 2, 1, NA_Q_ROWS * NA_WIN_ROWS, LANES), lambda v: (0, 0, v, 0, 0))],
        out_specs=pl.BlockSpec((1, NA_HEADS // 2, 2 * NA_TQ, NA_WIN), lambda v: (v, 0, 0, 0)),
        out_shape=jax.ShapeDtypeStruct((n_var, NA_HEADS // 2, 2 * NA_TQ, NA_WIN), F32),
        compiler_params=_params(1),
        name="na_bias_table",
    )(w)


def _ffn_params(norm2_g, w_out, w_gate, w_up, w_down, pair_gqa=False):
    half = w_out.shape[0] // 2
    wo1 = _pair_heads(w_out[:half], 0) if pair_gqa else w_out[:half]
    return dict(g2=norm2_g.reshape(1, -1), wo1=wo1.astype(BF16), wo2=w_out[half:].astype(BF16),
                wg=w_gate.astype(BF16), wu=w_up.astype(BF16), wd=w_down.astype(BF16))


def _mod_rows(mod_all, bsz):
    lat = mod_all[:bsz].reshape(bsz, 1, 6, D_MODEL)
    cx = jnp.broadcast_to(mod_all[bsz].reshape(1, 1, 6, D_MODEL), (bsz, 1, 6, D_MODEL))
    return jnp.concatenate([lat, cx], axis=1)


def kernel(x, c, ctx, c_ctx, l0_ada_w, l0_ada_b, l0_norm1_g, l0_norm2_g, l0_w_in, l0_mla_qa_g, l0_mla_w_uq, l0_mla_kva_g, l0_mla_w_ukv, l0_mla_qn_g, l0_mla_kn_g, l0_na_qn_g, l0_na_kn_g, l0_na_rpb, l0_w_out, l0_ffn_w_gate, l0_ffn_w_up, l0_ffn_w_down, l1_ada_w, l1_ada_b, l1_norm1_g, l1_norm2_g, l1_w_in, l1_gqa_qn_g, l1_gqa_kn_g, l1_gqa_sink, l1_diff_qn_g, l1_diff_kn_g, l1_diff_lq1, l1_diff_lk1, l1_diff_lq2, l1_diff_lk2, l1_diff_subln_g, l1_w_out, l1_ffn_w_gate, l1_ffn_w_up, l1_ffn_w_down):
    bsz = x.shape[0]
    assert x.shape[1:] == (SEQ, D_MODEL) and ctx.shape[1:] == (CTX_LEN, D_MODEL)
    rows = -(-(bsz + 1) // 8) * 8
    cond = jnp.concatenate([c, c_ctx[None, :], jnp.zeros((rows - bsz - 1, D_MODEL), F32)], axis=0)
    mod0 = _mod_rows(_ada(cond, l0_ada_w, l0_ada_b), bsz)
    mod1 = _mod_rows(_ada(cond, l1_ada_w, l1_ada_b), bsz)

    p0 = _layer0_params(l0_w_in, l0_mla_qa_g, l0_mla_w_uq, l0_mla_kva_g, l0_mla_w_ukv, l0_mla_qn_g,
                        l0_mla_kn_g, l0_na_qn_g, l0_na_kn_g, l0_norm1_g)
    qa, ka, va, qb, kb, vb = _proj0(x, ctx, mod0, p0)
    o_a = _mla(qa, ka, va)
    o_b = _na(qb, kb, vb, _na_table(l0_na_rpb))
    f0 = _ffn_params(l0_norm2_g, l0_w_out, l0_ffn_w_gate, l0_ffn_w_up, l0_ffn_w_down)
    xa = _ffn((x, ctx), o_a, o_b, mod0, f0, N_TILES)

    p1 = _layer1_params(l1_w_in, l1_gqa_qn_g, l1_gqa_kn_g, l1_diff_qn_g, l1_diff_kn_g, l1_norm1_g)
    qc, kc, vc, qd, kd, vd = _proj1(xa, mod1, p1)
    o_c = _gqa(l1_gqa_sink, qc, kc, vc)
    lv = jnp.zeros((8, LANES), F32).at[:4, :DIFF_HEAD_DIM].set(
        jnp.stack([l1_diff_lq1, l1_diff_lk1, l1_diff_lq2, l1_diff_lk2]))
    lam_init = 0.8 - 0.6 * math.exp(-0.3 * 1)
    o_d = _diff(lv, qd, kd, vd, l1_diff_subln_g.reshape(1, -1), lam_init)
    f1 = _ffn_params(l1_norm2_g, l1_w_out, l1_ffn_w_gate, l1_ffn_w_up, l1_ffn_w_down, pair_gqa=True)
    return _ffn((xa,), o_c, o_d, mod1, f1, N_LAT_TILES)
```

```python
import functools
import math

import numpy as np
import jax
import jax.numpy as jnp
from jax import lax
from jax.experimental import pallas as pl
from jax.experimental.pallas import tpu as pltpu

F32 = jnp.float32
BF16 = jnp.bfloat16

D_MODEL = 1024
SEQ = 2048
GRID_W = 64
CTX_LEN = 256
N_TOK = SEQ + CTX_LEN
ROPE_THETA = 10000.0
EPS = 1e-6

MLA_HEADS = 8
MLA_Q_RANK = 256
MLA_KV_RANK = 128
MLA_NOPE = 64
MLA_ROPE = 32
MLA_V = 64
MLA_QK = MLA_NOPE + MLA_ROPE

NA_HEADS = 8
NA_HEAD_DIM = 64
NA_ROWS_MAX = 8
NA_COLS = 16
NA_DIM = NA_HEADS * NA_HEAD_DIM

GQA_HEADS = 8
GQA_KV_HEADS = 2
GQA_HEAD_DIM = 64
GQA_WINDOW = 128

DIFF_HEADS = 4
DIFF_HEAD_DIM = 64

MLA_VDIM = MLA_HEADS * MLA_V
GQA_Q = GQA_HEADS * GQA_HEAD_DIM
GQA_KV = GQA_KV_HEADS * GQA_HEAD_DIM
DIFF_DIM = DIFF_HEADS * 2 * DIFF_HEAD_DIM

LANES = 128
MLA_SLOTS = MLA_HEADS * LANES
TOK_TILE = 256
N_TILES = N_TOK // TOK_TILE
N_LAT_TILES = SEQ // TOK_TILE
VMEM_LIMIT = 56 * 1024 * 1024
LOG2E = 1.4426950408889634
NEG_BIG = -1e30

ATT_TQ = 256
MLA_TQ = 256
NA_Q_ROWS = 2
NA_TQ = NA_Q_ROWS * GRID_W
NA_WIN_ROWS = 10
NA_WIN = NA_WIN_ROWS * GRID_W
NA_STEPS = SEQ // NA_TQ
NA_VARIANT_UNITS = (0, 1, 2, NA_STEPS - 2, NA_STEPS - 1)
GRID_ROWS = SEQ // GRID_W
GQA_TQ = 128
GQA_BAND = GQA_TQ + 2 * GQA_WINDOW


def _dot(a, b):
    return jnp.dot(a, b, preferred_element_type=F32)


def _dot_nt(a, b):
    return lax.dot_general(a, b, (((1,), (1,)), ((), ())), preferred_element_type=F32)


def _rms_rows(x):
    return lax.rsqrt(jnp.mean(x * x, axis=-1, keepdims=True) + EPS)


def _const_spec(a):
    nd = a.ndim
    return pl.BlockSpec(a.shape, lambda *_: (0,) * nd)


def _params(n_grid):
    return pltpu.CompilerParams(dimension_semantics=("arbitrary",) * n_grid,
                                vmem_limit_bytes=VMEM_LIMIT)


def _fold_lanes(xs, op):
    r = None
    for x in xs:
        for lo in range(0, x.shape[1], LANES):
            blk = x[:, lo:lo + LANES]
            r = blk if r is None else op(r, blk)
    return r


def _scores(q, keys, biases, s_ref, mx_ref, keys_t=False):
    parts = []
    lo = 0
    for k, bias in zip(keys, biases):
        sc = _dot(q, k) if keys_t else _dot_nt(q, k)
        if bias is not None:
            sc = sc + bias
        s_ref[:, lo:lo + sc.shape[1]] = sc
        lo += sc.shape[1]
        parts.append(sc)
    mx_ref[...] = _fold_lanes(parts, jnp.maximum)


def _softmax_pv(s_ref, mx_ref, vals, extra=None):
    m = jnp.max(mx_ref[...], axis=-1, keepdims=True)
    if extra is not None:
        m = jnp.maximum(m, extra)
    ps = []
    lo = 0
    for v in vals:
        ps.append(jnp.exp2(s_ref[:, lo:lo + v.shape[0]] - m))
        lo += v.shape[0]
    l = jnp.sum(_fold_lanes(ps, jnp.add), axis=-1, keepdims=True)
    if extra is not None:
        l = l + jnp.exp2(extra - m)
    acc = _dot(ps[0].astype(BF16), vals[0])
    for p, v in zip(ps[1:], vals[1:]):
        acc = acc + _dot(p.astype(BF16), v)
    return acc, l


def _run_skewed(units):
    units[0][0](0)
    for i, (_, finish) in enumerate(units):
        if i + 1 < len(units):
            units[i + 1][0]((i + 1) % 2)
        finish(i % 2)


def _row0(i, size):
    return i * size if isinstance(i, int) else pl.multiple_of(i * size, size)


def _skewed_loop(n, scores, finish, per_trip=8):
    assert n % per_trip == 0 and per_trip % 2 == 0
    scores(0, 0)

    def body(j, carry):
        for d in range(per_trip):
            i = per_trip * j + d
            scores(jnp.minimum(i + 1, n - 1), (d + 1) % 2)
            finish(i, d % 2)
        return carry

    lax.fori_loop(0, n // per_trip, body, 0)


def _att_scratch(lead, rows, n_keys):
    lead = (2,) + tuple(lead)
    return [pltpu.VMEM(lead + (rows, n_keys), F32), pltpu.VMEM(lead + (rows, LANES), F32)]


def _stack_halves(q):
    lane = lax.broadcasted_iota(jnp.int32, q.shape, 1)
    zero = jnp.zeros_like(q)
    return jnp.concatenate([jnp.where(lane < 64, q, zero), jnp.where(lane >= 64, q, zero)], axis=0)


def _merge_halves(lo, hi):
    lane = lax.broadcasted_iota(jnp.int32, lo.shape, 1)
    return jnp.where(lane < 64, lo, hi)


def _ada_kernel(c_ref, w0_ref, b0_ref, w1_ref, b1_ref, o_ref, *, n_chunks):
    c = c_ref[...]
    a = (c * (1.0 / (1.0 + jnp.exp(-c)))).astype(BF16)

    @pl.when(pl.program_id(0) < n_chunks)
    def _():
        o_ref[0] = _dot(a, w0_ref[...].astype(BF16)) + b0_ref[...]

    @pl.when(pl.program_id(0) >= n_chunks)
    def _():
        o_ref[0] = _dot(a, w1_ref[...].astype(BF16)) + b1_ref[...]


def _ada(cond, w0, b0, w1, b1):
    n = w0.shape[1]
    tn = D_MODEL
    nc = n // tn
    first = lambda j: (0, jnp.minimum(j, nc - 1))
    second = lambda j: (0, jnp.maximum(j - nc, 0))
    return pl.pallas_call(
        functools.partial(_ada_kernel, n_chunks=nc),
        grid=(2 * nc,),
        in_specs=[pl.BlockSpec(cond.shape, lambda j: (0, 0)),
                  pl.BlockSpec((w0.shape[0], tn), first), pl.BlockSpec((1, tn), first),
                  pl.BlockSpec((w1.shape[0], tn), second), pl.BlockSpec((1, tn), second)],
        out_specs=pl.BlockSpec((1, cond.shape[0], tn), lambda j: (j // nc, 0, j % nc)),
        out_shape=jax.ShapeDtypeStruct((2, cond.shape[0], n), F32),
        compiler_params=_params(1),
        name="ada_modulation",
    )(cond, w0, b0.reshape(1, n), w1, b1.reshape(1, n))


def _batch_tile(bsz, most=2):
    return max(bt for bt in (1, 2, 4) if bt <= most and bsz % bt == 0)


def _tok_spec(bt, width):
    return pl.BlockSpec((bt, TOK_TILE, width), lambda t, b: (b, t, 0))


def _lat_spec(bt):
    return pl.BlockSpec((bt, TOK_TILE, D_MODEL), lambda t, b: (b, jnp.minimum(t, N_LAT_TILES - 1), 0))


def _ctx_spec(bt):
    return pl.BlockSpec((bt, TOK_TILE, D_MODEL), lambda t, b: (b, 0, 0))


def _mod_spec(bt):
    return pl.BlockSpec((bt, 1, 6, D_MODEL), lambda t, b: (b, t // N_LAT_TILES, 0, 0))


def _table_spec(width):
    return pl.BlockSpec((TOK_TILE, width), lambda t, b: (t, 0))


def _read_tokens(x_ref, c_ref):
    return jnp.where(pl.program_id(0) < N_LAT_TILES, x_ref[...], c_ref[...])


def _mod_row(mod_ref, i):
    return mod_ref[:, 0, i:i + 1, :]


def _rows_dot(a, w):
    bt, rows, k = a.shape
    return _dot(a.reshape(bt * rows, k), w).reshape(bt, rows, w.shape[1])


W0_CQ = 0
W0_CKV = W0_CQ + MLA_Q_RANK
W0_KR = W0_CKV + MLA_KV_RANK
W0_KRP = W0_KR + LANES
W0_QN = W0_KRP + LANES
W0_KN = W0_QN + NA_DIM
W0_VN = W0_KN + NA_DIM
W0_END = W0_VN + NA_DIM


def _head_inv(z, n):
    return lax.rsqrt(jnp.sum(z * z, axis=-1, keepdims=True) * (1.0 / n) + EPS)


def _half_inv(z):
    z2 = z * z
    lane = lax.broadcasted_iota(jnp.int32, z.shape, z.ndim - 1)
    lo = jnp.sum(jnp.where(lane < 64, z2, 0.0), axis=-1, keepdims=True)
    hi = jnp.sum(jnp.where(lane < 64, 0.0, z2), axis=-1, keepdims=True)
    return jnp.where(lane < 64, lax.rsqrt(lo * (1.0 / 64.0) + EPS), lax.rsqrt(hi * (1.0 / 64.0) + EPS))


def _proj0_kernel(x_ref, c_ref, mod_ref, g1_ref, w0_ref, qag_ref, wuq_ref, kvag_ref, wukv_ref,
                  aq_ref, bq_ref, ak_ref, bk_ref, gqb_ref, gkb_ref,
                  qa_ref, ka_ref, va_ref, qb_ref, kb_ref, vb_ref):
    x = _read_tokens(x_ref, c_ref)
    h = x * _rms_rows(x) * g1_ref[...] * (1.0 + _mod_row(mod_ref, 1)) + _mod_row(mod_ref, 0)
    z = _rows_dot(h.astype(BF16), w0_ref[...])

    cq = z[:, :, W0_CQ:W0_CKV]
    cqn = (cq * _rms_rows(cq) * qag_ref[...]).astype(BF16)
    zq = _rows_dot(cqn, wuq_ref[...])
    aq, bq = aq_ref[...], bq_ref[...]
    for hd in range(MLA_HEADS):
        qm = zq[:, :, hd * LANES:(hd + 1) * LANES]
        qr = zq[:, :, MLA_SLOTS + hd * LANES:MLA_SLOTS + (hd + 1) * LANES]
        qa_ref[:, :, hd * LANES:(hd + 1) * LANES] = ((qm * aq + qr * bq) * _head_inv(qm, MLA_QK)).astype(BF16)

    ckv = z[:, :, W0_CKV:W0_KR]
    ckvn = (ckv * _rms_rows(ckv) * kvag_ref[...]).astype(BF16)
    zkv = _rows_dot(ckvn, wukv_ref[...])
    kr = z[:, :, W0_KR:W0_KRP]
    kr_rot = z[:, :, W0_KRP:W0_QN] * bk_ref[...]
    ak = ak_ref[...]
    for hd in range(MLA_HEADS):
        km = zkv[:, :, hd * LANES:(hd + 1) * LANES] + kr
        ka_ref[:, :, hd * LANES:(hd + 1) * LANES] = ((km * ak + kr_rot) * _head_inv(km, MLA_QK)).astype(BF16)
    va_ref[...] = zkv[:, :, MLA_SLOTS:].astype(BF16)

    for j in range(NA_DIM // LANES):
        cols = slice(j * LANES, (j + 1) * LANES)
        qn = z[:, :, W0_QN + j * LANES:W0_QN + (j + 1) * LANES]
        kn = z[:, :, W0_KN + j * LANES:W0_KN + (j + 1) * LANES]
        qb_ref[:, :, cols] = (qn * gqb_ref[...] * _half_inv(qn)).astype(BF16)
        kb_ref[:, :, cols] = (kn * gkb_ref[...] * _half_inv(kn)).astype(BF16)
    vb_ref[...] = z[:, :, W0_VN:W0_END].astype(BF16)


def _proj0(x, ctx, mod, p):
    bsz = x.shape[0]
    bt = _batch_tile(bsz, most=4)
    consts1 = [p['g1'], p['w0'], p['qag'], p['wuq'], p['kvag'], p['wukv']]
    tables = [p['aq'], p['bq'], p['ak'], p['bk']]
    consts2 = [p['gqb'], p['gkb']]
    widths = [MLA_SLOTS, MLA_SLOTS, MLA_VDIM, NA_DIM, NA_DIM, NA_DIM]
    return pl.pallas_call(
        _proj0_kernel,
        grid=(N_TILES, bsz // bt),
        in_specs=([_lat_spec(bt), _ctx_spec(bt), _mod_spec(bt)] + [_const_spec(a) for a in consts1]
                  + [_table_spec(LANES) for _ in tables] + [_const_spec(a) for a in consts2]),
        out_specs=[_tok_spec(bt, w) for w in widths],
        out_shape=[jax.ShapeDtypeStruct((bsz, N_TOK, w), BF16) for w in widths],
        compiler_params=_params(2),
        name="proj_layer0",
    )(x, ctx, mod, *consts1, *tables, *consts2)


N1_SEG = tuple(w // LANES for w in (GQA_Q, GQA_KV, DIFF_DIM, DIFF_DIM))
N1_V_GQA = sum(N1_SEG) * LANES
N1_V_DIFF = N1_V_GQA + GQA_KV
N1_MAIN = N1_V_DIFF + DIFF_DIM


def _proj1_kernel(x_ref, mod_ref, g1_ref, w1_ref, a_ref, b_ref,
                  qc_ref, kc_ref, vc_ref, qd_ref, kd_ref, vd_ref):
    x = x_ref[...]
    h = x * _rms_rows(x) * g1_ref[...] * (1.0 + _mod_row(mod_ref, 1)) + _mod_row(mod_ref, 0)
    z = _rows_dot(h.astype(BF16), w1_ref[...])
    outs = (qc_ref, kc_ref, qd_ref, kd_ref)
    blk = 0
    for seg, n_blk in enumerate(N1_SEG):
        a = a_ref[:, seg * LANES:(seg + 1) * LANES]
        b = b_ref[:, seg * LANES:(seg + 1) * LANES]
        for j in range(n_blk):
            zm = z[:, :, blk * LANES:(blk + 1) * LANES]
            rot = z[:, :, N1_MAIN + blk * LANES:N1_MAIN + (blk + 1) * LANES]
            outs[seg][:, :, j * LANES:(j + 1) * LANES] = ((zm * a + rot * b) * _half_inv(zm)).astype(BF16)
            blk += 1
    vc_ref[...] = z[:, :, N1_V_GQA:N1_V_DIFF].astype(BF16)
    vd_ref[...] = z[:, :, N1_V_DIFF:N1_MAIN].astype(BF16)


def _proj1(xa, mod, p):
    bsz = xa.shape[0]
    bt = _batch_tile(bsz, most=4)
    consts = [p['g1'], p['w1']]
    widths = [GQA_Q, GQA_KV, GQA_KV, DIFF_DIM, DIFF_DIM, DIFF_DIM]
    return pl.pallas_call(
        _proj1_kernel,
        grid=(N_TILES, bsz // bt),
        in_specs=([_tok_spec(bt, D_MODEL), _mod_spec(bt)] + [_const_spec(a) for a in consts]
                  + [_table_spec(4 * LANES), _table_spec(4 * LANES)]),
        out_specs=[_tok_spec(bt, w) for w in widths],
        out_shape=[jax.ShapeDtypeStruct((bsz, N_TOK, w), BF16) for w in widths],
        compiler_params=_params(2),
        name="proj_layer1",
    )(xa, mod, *consts, p['a'], p['b'])


def _mla_kernel(q_ref, k_ref, v_ref, o_ref, s_scr, mx_scr, kt_scr, vx_scr):
    vx_scr[:, :LANES] = v_ref[0]
    vx_scr[:, LANES:] = jnp.ones((N_TOK, LANES), BF16)
    for hh in range(2):
        kt_scr[hh] = k_ref[0, :, hh * LANES:(hh + 1) * LANES].T

    def scores(r0, slot, key_lo):
        for hh in range(2):
            cols = slice(hh * LANES, (hh + 1) * LANES)
            rows = slice(hh * MLA_TQ, (hh + 1) * MLA_TQ)
            _scores(q_ref[0, r0:r0 + MLA_TQ, cols], [kt_scr[hh, :, key_lo:]], [None],
                    s_scr.at[slot, rows], mx_scr.at[slot, rows], keys_t=True)

    def finish(r0, slot, key_lo):
        m = jnp.max(mx_scr[slot], axis=-1, keepdims=True)
        p = jnp.exp2(s_scr[slot, :, 0:N_TOK - key_lo] - m).astype(BF16)
        acc = _dot(p, vx_scr[key_lo:, :])
        r = acc[:, :LANES] / acc[:, LANES:LANES + 1]
        o_ref[0, r0:r0 + MLA_TQ, :] = _merge_halves(r[:MLA_TQ], r[MLA_TQ:]).astype(BF16)

    units = [(functools.partial(scores, r0, key_lo=0), functools.partial(finish, r0, key_lo=0))
             for r0 in range(0, SEQ, MLA_TQ)]
    units.append((functools.partial(scores, SEQ, key_lo=SEQ),
                  functools.partial(finish, SEQ, key_lo=SEQ)))
    _run_skewed(units)


def _mla(qa, ka, va):
    bsz = qa.shape[0]
    assert CTX_LEN == MLA_TQ
    return pl.pallas_call(
        _mla_kernel,
        grid=(bsz, MLA_HEADS // 2),
        in_specs=[pl.BlockSpec((1, N_TOK, 2 * LANES), lambda b, j: (b, 0, j)),
                  pl.BlockSpec((1, N_TOK, 2 * LANES), lambda b, j: (b, 0, j)),
                  pl.BlockSpec((1, N_TOK, LANES), lambda b, j: (b, 0, j))],
        out_specs=pl.BlockSpec((1, N_TOK, LANES), lambda b, j: (b, 0, j)),
        out_shape=jax.ShapeDtypeStruct((bsz, N_TOK, MLA_HEADS * MLA_V), BF16),
        scratch_shapes=(_att_scratch((), 2 * MLA_TQ, N_TOK)
                        + [pltpu.VMEM((2, LANES, N_TOK), BF16), pltpu.VMEM((N_TOK, 2 * LANES), BF16)]),
        compiler_params=_params(2),
        name="mla_attention",
    )(qa, ka, va)


def _na_kernel(q_ref, k_ref, v_ref, t_ref, o_ref, s_scr, mx_scr):
    n_pairs = NA_HEADS // 2

    def window_start(u):
        w0 = jnp.minimum(jnp.clip(NA_Q_ROWS * u - NA_ROWS_MAX // 2, 0, GRID_ROWS - NA_ROWS_MAX),
                         GRID_ROWS - NA_WIN_ROWS)
        return pl.multiple_of(w0 * GRID_W, GRID_W)

    def scores(u, slot):
        ws = window_start(u)
        r0 = _row0(u, NA_TQ)
        var = jnp.minimum(u, 2) + jnp.maximum(u - (NA_STEPS - 3), 0)
        for j in range(n_pairs):
            cols = slice(j * LANES, (j + 1) * LANES)
            qq = _stack_halves(q_ref[0, pl.ds(r0, NA_TQ), cols])
            _scores(qq, [k_ref[0, pl.ds(ws, NA_WIN), cols], k_ref[0, SEQ:, cols]], [t_ref[var, j], None],
                    s_scr.at[slot, j], mx_scr.at[slot, j])

    def finish(u, slot):
        ws = window_start(u)
        r0 = _row0(u, NA_TQ)
        for j in range(n_pairs):
            cols = slice(j * LANES, (j + 1) * LANES)
            acc, l = _softmax_pv(s_scr.at[slot, j], mx_scr.at[slot, j],
                                 [v_ref[0, pl.ds(ws, NA_WIN), cols], v_ref[0, SEQ:, cols]])
            r = acc / l
            o_ref[0, pl.ds(r0, NA_TQ), cols] = _merge_halves(r[:NA_TQ], r[NA_TQ:]).astype(BF16)

    _skewed_loop(NA_STEPS, scores, finish)

    def ctx_scores(r0, slot):
        for j in range(n_pairs):
            cols = slice(j * LANES, (j + 1) * LANES)
            qq = _stack_halves(q_ref[0, r0:r0 + NA_TQ, cols])
            _scores(qq, [k_ref[0, SEQ:, cols]], [None], s_scr.at[slot, j], mx_scr.at[slot, j])

    def ctx_finish(r0, slot):
        for j in range(n_pairs):
            cols = slice(j * LANES, (j + 1) * LANES)
            acc, l = _softmax_pv(s_scr.at[slot, j], mx_scr.at[slot, j], [v_ref[0, SEQ:, cols]])
            r = acc / l
            o_ref[0, r0:r0 + NA_TQ, cols] = _merge_halves(r[:NA_TQ], r[NA_TQ:]).astype(BF16)

    _run_skewed([(functools.partial(ctx_scores, r0), functools.partial(ctx_finish, r0))
                 for r0 in range(SEQ, N_TOK, NA_TQ)])


def _na(qb, kb, vb, table):
    bsz = qb.shape[0]
    tok = pl.BlockSpec((1, N_TOK, NA_DIM), lambda b: (b, 0, 0))
    n_pairs = NA_HEADS // 2
    return pl.pallas_call(
        _na_kernel,
        grid=(bsz,),
        in_specs=[tok, tok, tok, _const_spec(table)],
        out_specs=tok,
        out_shape=jax.ShapeDtypeStruct((bsz, N_TOK, NA_DIM), BF16),
        scratch_shapes=_att_scratch((n_pairs,), 2 * NA_TQ, NA_WIN + CTX_LEN),
        compiler_params=_params(1),
        name="neighbourhood_attention",
    )(qb, kb, vb, table)


def _gqa_kernel(sink_ref, q_ref, k_ref, v_ref, o_ref, s_scr, mx_scr):
    row = lax.broadcasted_iota(jnp.int32, (2 * GQA_TQ, GQA_BAND), 0)
    col = lax.broadcasted_iota(jnp.int32, (2 * GQA_TQ, GQA_BAND), 1)
    rel = jnp.where(row >= GQA_TQ, row - GQA_TQ, row) - col
    hi_rows = lax.broadcasted_iota(jnp.int32, (2 * GQA_TQ, 1), 0) >= GQA_TQ

    def band_start(r0):
        return min(max(r0 - GQA_WINDOW, 0), SEQ - GQA_BAND)

    def scores(n, slot):
        r0 = n * GQA_TQ
        start = band_start(r0)
        mask = jnp.where(jnp.abs(rel + (r0 - start)) <= GQA_WINDOW, 0.0, NEG_BIG)
        keys = [k_ref[0, start:start + GQA_BAND, :], k_ref[0, SEQ:, :]]
        for j in range(GQA_HEADS // 2):
            qq = _stack_halves(q_ref[0, r0:r0 + GQA_TQ, j * LANES:(j + 1) * LANES])
            _scores(qq, keys, [mask, None], s_scr.at[slot, j], mx_scr.at[slot, j])

    def finish(n, slot):
        r0 = n * GQA_TQ
        start = band_start(r0)
        vals = [v_ref[0, start:start + GQA_BAND, :], v_ref[0, SEQ:, :]]
        for j in range(GQA_HEADS // 2):
            sink = jnp.where(hi_rows, sink_ref[j + GQA_HEADS // 2], sink_ref[j]) * LOG2E
            acc, l = _softmax_pv(s_scr.at[slot, j], mx_scr.at[slot, j], vals, extra=sink)
            r = acc / l
            o_ref[0, r0:r0 + GQA_TQ, j * LANES:(j + 1) * LANES] = _merge_halves(r[:GQA_TQ], r[GQA_TQ:]).astype(BF16)

    _run_skewed([(functools.partial(scores, n), functools.partial(finish, n)) for n in range(SEQ // GQA_TQ)])


def _gqa(sink, qc, kc, vc):
    bsz = qc.shape[0]
    return pl.pallas_call(
        _gqa_kernel,
        grid=(bsz,),
        in_specs=[pl.BlockSpec(memory_space=pltpu.SMEM),
                  pl.BlockSpec((1, SEQ, GQA_Q), lambda b: (b, 0, 0)),
                  pl.BlockSpec((1, N_TOK, GQA_KV), lambda b: (b, 0, 0)),
                  pl.BlockSpec((1, N_TOK, GQA_KV), lambda b: (b, 0, 0))],
        out_specs=pl.BlockSpec((1, SEQ, GQA_Q), lambda b: (b, 0, 0)),
        out_shape=jax.ShapeDtypeStruct((bsz, SEQ, GQA_Q), BF16),
        scratch_shapes=_att_scratch((GQA_HEADS // 2,), 2 * GQA_TQ, GQA_BAND + CTX_LEN),
        compiler_params=_params(1),
        name="windowed_gqa",
    )(sink, qc, kc, vc)


def _diff_kernel(lv_ref, q_ref, k_ref, v_ref, g_ref, o_ref, s_scr, mx_scr, kt_scr, vx_scr, *, lam_init):
    kt_scr[...] = k_ref[0].T
    vx_scr[:, :LANES] = v_ref[0]
    vx_scr[:, LANES:] = jnp.ones((N_TOK, LANES), BF16)
    lv = lv_ref[...]
    lam = (jnp.exp(jnp.sum(lv[0:1] * lv[1:2], axis=-1, keepdims=True))
           - jnp.exp(jnp.sum(lv[2:3] * lv[3:4], axis=-1, keepdims=True)) + lam_init)
    hi_rows = lax.broadcasted_iota(jnp.int32, (2 * ATT_TQ, 1), 0) >= ATT_TQ
    coef = jnp.where(hi_rows, -lam, 1.0)
    g = g_ref[...] * (1.0 - lam_init)

    def scores(i, slot):
        qq = _stack_halves(q_ref[0, i * ATT_TQ:(i + 1) * ATT_TQ, :])
        _scores(qq, [kt_scr[...]], [None], s_scr.at[slot], mx_scr.at[slot], keys_t=True)

    def finish(i, slot):
        m = jnp.max(mx_scr[slot], axis=-1, keepdims=True)
        p = jnp.exp2(s_scr[slot] - m).astype(BF16)
        acc = _dot(p, vx_scr[...])
        r = acc[:, :LANES] * (coef / acc[:, LANES:LANES + 1])
        o = r[:ATT_TQ] + r[ATT_TQ:]
        o_ref[0, i * ATT_TQ:(i + 1) * ATT_TQ, :] = (o * _rms_rows(o) * g).astype(BF16)

    _run_skewed([(functools.partial(scores, i), functools.partial(finish, i)) for i in range(SEQ // ATT_TQ)])


def _diff(lv, qd, kd, vd, g, lam_init):
    bsz = qd.shape[0]
    return pl.pallas_call(
        functools.partial(_diff_kernel, lam_init=lam_init),
        grid=(bsz, DIFF_HEADS),
        in_specs=[_const_spec(lv),
                  pl.BlockSpec((1, SEQ, LANES), lambda b, h: (b, 0, h)),
                  pl.BlockSpec((1, N_TOK, LANES), lambda b, h: (b, 0, h)),
                  pl.BlockSpec((1, N_TOK, LANES), lambda b, h: (b, 0, h)),
                  _const_spec(g)],
        out_specs=pl.BlockSpec((1, SEQ, LANES), lambda b, h: (b, 0, h)),
        out_shape=jax.ShapeDtypeStruct((bsz, SEQ, DIFF_HEADS * 2 * DIFF_HEAD_DIM), BF16),
        scratch_shapes=(_att_scratch((), 2 * ATT_TQ, N_TOK)
                        + [pltpu.VMEM((LANES, N_TOK), BF16), pltpu.VMEM((N_TOK, 2 * LANES), BF16)]),
        compiler_params=_params(2),
        name="diff_attention",
    )(lv, qd, kd, vd, g)


def _ffn_body(x, o1_ref, o2_ref, mod_ref, g2_ref, wo1_ref, wo2_ref, wg_ref, wu_ref, wd_ref, y_ref):
    attn = _rows_dot(o1_ref[...], wo1_ref[...]) + _rows_dot(o2_ref[...], wo2_ref[...])
    x1 = x + _mod_row(mod_ref, 2) * attn
    h = (x1 * _rms_rows(x1) * g2_ref[...] * (1.0 + _mod_row(mod_ref, 4)) + _mod_row(mod_ref, 3)).astype(BF16)
    gate = _rows_dot(h, wg_ref[...])
    up = _rows_dot(h, wu_ref[...])
    a = (gate * (1.0 / (1.0 + jnp.exp(-gate))) * up).astype(BF16)
    y_ref[...] = x1 + _mod_row(mod_ref, 5) * _rows_dot(a, wd_ref[...])


def _ffn0_kernel(x_ref, c_ref, *rest):
    _ffn_body(_read_tokens(x_ref, c_ref), *rest)


def _ffn1_kernel(x_ref, *rest):
    _ffn_body(x_ref[...], *rest)


def _ffn(xs, o1, o2, mod, p, n_tiles):
    bsz = xs[0].shape[0]
    bt = _batch_tile(bsz)
    consts = [p['g2'], p['wo1'], p['wo2'], p['wg'], p['wu'], p['wd']]
    x_specs = [_lat_spec(bt), _ctx_spec(bt)] if len(xs) == 2 else [_tok_spec(bt, D_MODEL)]
    return pl.pallas_call(
        _ffn0_kernel if len(xs) == 2 else _ffn1_kernel,
        grid=(n_tiles, bsz // bt),
        in_specs=(x_specs + [_tok_spec(bt, o1.shape[-1]), _tok_spec(bt, o2.shape[-1]), _mod_spec(bt)]
                  + [_const_spec(a) for a in consts]),
        out_specs=_tok_spec(bt, D_MODEL),
        out_shape=jax.ShapeDtypeStruct((bsz, n_tiles * TOK_TILE, D_MODEL), F32),
        compiler_params=_params(2),
        name="outproj_ffn",
    )(*xs, o1, o2, mod, *consts)


def _partner_cols(w, dr):
    q = dr // 4
    shp = w.shape
    return jnp.flip(w.reshape(shp[:-1] + (shp[-1] // dr, 2, 2, q)), axis=-2).reshape(shp)


def _rope_tables(dr):
    h = dr // 2
    q = dr // 4
    freqs = ROPE_THETA ** (-jnp.arange(0, h, 2, dtype=F32) / h)
    t = jnp.arange(SEQ)
    pos = jnp.stack([t // GRID_W, t % GRID_W], axis=1).astype(F32)
    ang = (pos[:, :, None, None] * freqs[None, None, None, :])
    ang = jnp.broadcast_to(ang, (SEQ, 2, 2, q)).reshape(SEQ, dr)
    sign = np.tile(np.repeat(np.array([-1.0, 1.0], np.float32), q), 2)
    cos = jnp.concatenate([jnp.cos(ang), jnp.ones((CTX_LEN, dr), F32)], axis=0)
    sin = jnp.concatenate([jnp.sin(ang) * sign, jnp.zeros((CTX_LEN, dr), F32)], axis=0)
    return cos, sin


def _layer0_params(w_in, qa_g, w_uq, kva_g, w_ukv, qn_g, kn_g, na_qn_g, na_kn_g, norm1_g):
    cos, sin = _rope_tables(MLA_ROPE)
    o_kr = MLA_Q_RANK + MLA_KV_RANK
    kr_w = w_in[:, o_kr:o_kr + MLA_ROPE]
    lane_pad = ((0, 0), (MLA_NOPE, LANES - MLA_QK))
    w0 = jnp.concatenate([w_in[:, :o_kr], jnp.pad(kr_w, lane_pad), jnp.pad(_partner_cols(kr_w, MLA_ROPE), lane_pad),
                          w_in[:, o_kr + MLA_ROPE:]], axis=1).astype(BF16)
    r = w_uq.shape[0]
    w3 = w_uq.reshape(r, MLA_HEADS, MLA_QK)
    main = jnp.pad(w3, ((0, 0), (0, 0), (0, LANES - MLA_QK)))
    rot = jnp.pad(_partner_cols(w3[:, :, MLA_NOPE:], MLA_ROPE), ((0, 0), (0, 0), (MLA_NOPE, LANES - MLA_QK)))
    wuq = jnp.concatenate([main.reshape(r, -1), rot.reshape(r, -1)], axis=1).astype(BF16)

    def tables(g, c):
        z32 = jnp.zeros((N_TOK, LANES - MLA_QK), F32)
        a = jnp.concatenate([jnp.broadcast_to(g[None, :MLA_NOPE], (N_TOK, MLA_NOPE)), g[None, MLA_NOPE:] * cos, z32], axis=1)
        b = jnp.concatenate([jnp.zeros((N_TOK, MLA_NOPE), F32), _partner_cols(g[None, MLA_NOPE:], MLA_ROPE) * sin, z32], axis=1)
        return a * c, b * c

    aq, bq = tables(qn_g, MLA_QK ** -0.5 * LOG2E)
    ak, bk = tables(kn_g, 1.0)
    rk = w_ukv.shape[0]
    k3 = w_ukv.reshape(rk, MLA_HEADS, MLA_NOPE + MLA_V)
    wukv = jnp.concatenate([jnp.pad(k3[:, :, :MLA_NOPE], ((0, 0), (0, 0), (0, LANES - MLA_NOPE))).reshape(rk, -1),
                            k3[:, :, MLA_NOPE:].reshape(rk, -1)], axis=1).astype(BF16)
    return dict(
        g1=norm1_g.reshape(1, -1), w0=w0, qag=qa_g.reshape(1, -1), wuq=wuq, kvag=kva_g.reshape(1, -1),
        wukv=wukv, aq=aq, bq=bq, ak=ak, bk=bk,
        gqb=(jnp.tile(na_qn_g, 2) * (NA_HEAD_DIM ** -0.5 * LOG2E)).reshape(1, -1),
        gkb=jnp.tile(na_kn_g, 2).reshape(1, -1))


def _pair_heads(w, axis):
    g = GQA_HEADS // GQA_KV_HEADS
    shp = w.shape
    w4 = w.reshape(shp[:axis] + (GQA_KV_HEADS, g, GQA_HEAD_DIM) + shp[axis + 1:])
    return jnp.swapaxes(w4, axis, axis + 1).reshape(shp)


def _layer1_params(w_in, gqa_qn_g, gqa_kn_g, diff_qn_g, diff_kn_g, norm1_g):
    cos, sin = _rope_tables(64)
    o_vg = GQA_Q + GQA_KV
    o_qd = o_vg + GQA_KV
    o_vd = o_qd + 2 * DIFF_DIM
    main = jnp.concatenate([_pair_heads(w_in[:, :GQA_Q], 1), w_in[:, GQA_Q:o_vg], w_in[:, o_qd:o_vd]], axis=1)
    w1 = jnp.concatenate([main, w_in[:, o_vg:o_qd], w_in[:, o_vd:], _partner_cols(main, GQA_HEAD_DIM)],
                         axis=1).astype(BF16)
    a_parts, b_parts = [], []
    for g, c in ((gqa_qn_g, GQA_HEAD_DIM ** -0.5 * LOG2E), (gqa_kn_g, 1.0),
                 (diff_qn_g, DIFF_HEAD_DIM ** -0.5 * LOG2E), (diff_kn_g, 1.0)):
        a_parts.append(jnp.tile(g[None, :] * cos * c, (1, 2)))
        b_parts.append(jnp.tile(_partner_cols(g[None, :], 64) * sin * c, (1, 2)))
    return dict(g1=norm1_g.reshape(1, -1), w1=w1,
                a=jnp.concatenate(a_parts, axis=1), b=jnp.concatenate(b_parts, axis=1))


def _na_table_kernel(w_ref, o_ref):
    row = lax.broadcasted_iota(jnp.int32, (GRID_W, LANES), 0)
    lane = lax.broadcasted_iota(jnp.int32, (GRID_W, LANES), 1)
    kc = jnp.where(lane < GRID_W, lane, lane - GRID_W)
    c0 = jnp.clip(row - NA_COLS // 2, 0, GRID_W - NA_COLS)
    band = (kc >= c0) & (kc < c0 + NA_COLS)
    shift = LANES - (GRID_W - 1)
    for pair in range(NA_HEADS // 2):
        for hf in range(2):
            for a in range(NA_Q_ROWS):
                for jp in range(NA_WIN_ROWS // 2):
                    n0 = a * NA_WIN_ROWS + 2 * jp
                    lo = jnp.broadcast_to(w_ref[pair, hf, 0, n0:n0 + 1, :], (GRID_W, LANES))
                    hi = jnp.broadcast_to(w_ref[pair, hf, 0, n0 + 1:n0 + 2, :], (GRID_W, LANES))
                    lo = pltpu.roll(lo, shift, 1, stride=1, stride_axis=0)
                    hi = pltpu.roll(hi, (shift + GRID_W) % LANES, 1, stride=1, stride_axis=0)
                    r0 = (hf * NA_Q_ROWS + a) * GRID_W
                    o_ref[0, pair, r0:r0 + GRID_W, jp * LANES:(jp + 1) * LANES] = jnp.where(
                        band, jnp.where(lane < GRID_W, lo, hi), NEG_BIG)


def _na_table(rpb):
    us = np.array(NA_VARIANT_UNITS)
    n_var = len(us)
    w0 = np.minimum(np.clip(NA_Q_ROWS * us - NA_ROWS_MAX // 2, 0, GRID_ROWS - NA_ROWS_MAX), GRID_ROWS - NA_WIN_ROWS)
    r = NA_Q_ROWS * us[:, None, None] + np.arange(NA_Q_ROWS)[None, :, None]
    key_r = w0[:, None, None] + np.arange(NA_WIN_ROWS)[None, None, :]
    r0 = np.clip(r - NA_ROWS_MAX // 2, 0, GRID_ROWS - NA_ROWS_MAX)
    row_valid = (key_r >= r0) & (key_r < r0 + NA_ROWS_MAX)
    off_r = np.clip(key_r - r + (NA_ROWS_MAX - 1), 0, 2 * NA_ROWS_MAX - 2)
    pick = (off_r.reshape(-1, 1) == np.arange(2 * NA_ROWS_MAX - 1)[None, :]).astype(np.float32)
    rows = jnp.einsum('nr,hrd->hnd', pick, rpb, precision=lax.Precision.HIGHEST) * LOG2E
    pad = GRID_W - NA_COLS
    w = jnp.pad(rows, ((0, 0), (0, 0), (pad, LANES - pad - (2 * NA_COLS - 1))))
    w = jnp.where(jnp.asarray(row_valid.reshape(1, -1, 1)), w, NEG_BIG)
    w = w.reshape(NA_HEADS // 2, 2, n_var, NA_Q_ROWS * NA_WIN_ROWS, LANES)
    return pl.pallas_call(
        _na_table_kernel,
        grid=(n_var,),
        in_specs=[pl.BlockSpec((NA_HEADS // 2, 2, 1, NA_Q_ROWS * NA_WIN_ROWS, LANES), lambda v: (0, 0, v, 0, 0))],
        out_specs=pl.BlockSpec((1, NA_HEADS // 2, 2 * NA_TQ, NA_WIN), lambda v: (v, 0, 0, 0)),
        out_shape=jax.ShapeDtypeStruct((n_var, NA_HEADS // 2, 2 * NA_TQ, NA_WIN), F32),
        compiler_params=_params(1),
        name="na_bias_table",
    )(w)


def _ffn_params(norm2_g, w_out, w_gate, w_up, w_down, pair_gqa=False):
    half = w_out.shape[0] // 2
    wo1 = _pair_heads(w_out[:half], 0) if pair_gqa else w_out[:half]
    return dict(g2=norm2_g.reshape(1, -1), wo1=wo1.astype(BF16), wo2=w_out[half:].astype(BF16),
                wg=w_gate.astype(BF16), wu=w_up.astype(BF16), wd=w_down.astype(BF16))


def _mod_rows(mod_all, bsz):
    lat = mod_all[:bsz].reshape(bsz, 1, 6, D_MODEL)
    cx = jnp.broadcast_to(mod_all[bsz].reshape(1, 1, 6, D_MODEL), (bsz, 1, 6, D_MODEL))
    return jnp.concatenate([lat, cx], axis=1)


def kernel(x, c, ctx, c_ctx, l0_ada_w, l0_ada_b, l0_norm1_g, l0_norm2_g, l0_w_in, l0_mla_qa_g, l0_mla_w_uq, l0_mla_kva_g, l0_mla_w_ukv, l0_mla_qn_g, l0_mla_kn_g, l0_na_qn_g, l0_na_kn_g, l0_na_rpb, l0_w_out, l0_ffn_w_gate, l0_ffn_w_up, l0_ffn_w_down, l1_ada_w, l1_ada_b, l1_norm1_g, l1_norm2_g, l1_w_in, l1_gqa_qn_g, l1_gqa_kn_g, l1_gqa_sink, l1_diff_qn_g, l1_diff_kn_g, l1_diff_lq1, l1_diff_lk1, l1_diff_lq2, l1_diff_lk2, l1_diff_subln_g, l1_w_out, l1_ffn_w_gate, l1_ffn_w_up, l1_ffn_w_down):
    bsz = x.shape[0]
    assert x.shape[1:] == (SEQ, D_MODEL) and ctx.shape[1:] == (CTX_LEN, D_MODEL)
    rows = -(-(bsz + 1) // 8) * 8
    cond = jnp.concatenate([c, c_ctx[None, :], jnp.zeros((rows - bsz - 1, D_MODEL), F32)], axis=0)
    mod_all = _ada(cond, l0_ada_w, l0_ada_b, l1_ada_w, l1_ada_b)
    mod0 = _mod_rows(mod_all[0], bsz)
    mod1 = _mod_rows(mod_all[1], bsz)

    p0 = _layer0_params(l0_w_in, l0_mla_qa_g, l0_mla_w_uq, l0_mla_kva_g, l0_mla_w_ukv, l0_mla_qn_g,
                        l0_mla_kn_g, l0_na_qn_g, l0_na_kn_g, l0_norm1_g)
    qa, ka, va, qb, kb, vb = _proj0(x, ctx, mod0, p0)
    o_a = _mla(qa, ka, va)
    o_b = _na(qb, kb, vb, _na_table(l0_na_rpb))
    f0 = _ffn_params(l0_norm2_g, l0_w_out, l0_ffn_w_gate, l0_ffn_w_up, l0_ffn_w_down)
    xa = _ffn((x, ctx), o_a, o_b, mod0, f0, N_TILES)

    p1 = _layer1_params(l1_w_in, l1_gqa_qn_g, l1_gqa_kn_g, l1_diff_qn_g, l1_diff_kn_g, l1_norm1_g)
    qc, kc, vc, qd, kd, vd = _proj1(xa, mod1, p1)
    o_c = _gqa(l1_gqa_sink, qc, kc, vc)
    lv = jnp.zeros((8, LANES), F32).at[:4, :DIFF_HEAD_DIM].set(
        jnp.stack([l1_diff_lq1, l1_diff_lk1, l1_diff_lq2, l1_diff_lk2]))
    lam_init = 0.8 - 0.6 * math.exp(-0.3 * 1)
    o_d = _diff(lv, qd, kd, vd, l1_diff_subln_g.reshape(1, -1), lam_init)
    f1 = _ffn_params(l1_norm2_g, l1_w_out, l1_ffn_w_gate, l1_ffn_w_up, l1_ffn_w_down, pair_gqa=True)
    return _ffn((xa,), o_c, o_d, mod1, f1, N_LAT_TILES)
```
